```python
import math
import jax
import jax.numpy as jnp
from jax import lax
import numpy as np

D_MODEL = 2048
BATCH = 4
SEQ = 2048
DEPTH = 1

GRID_W = 64
CTX_LEN = 256
HEAD_DIM = 128
N_Q_HEADS = D_MODEL // HEAD_DIM
N_KV_HEADS = N_Q_HEADS // 4
Q_PER_KV = N_Q_HEADS // N_KV_HEADS
ATTN_W = N_Q_HEADS * HEAD_DIM
KV_W = N_KV_HEADS * HEAD_DIM
ROPE_AXIS_DIM = HEAD_DIM // 2
ROPE_THETA = 10000.0
Q_BLOCK = 128
ATTN_SCALE = HEAD_DIM ** -0.5
SSM_W = D_MODEL // 2
SSM_GROUP = 16
SSM_GROUPS = SSM_W // SSM_GROUP
SSM_STATE = 64
DT_MIN = 1e-3
DT_MAX = 1e-1
D_FF = ((8 * D_MODEL // 3 + 255) // 256) * 256
N_MOD = 9
N_MOD_CTX_LAST = 5
NORM_EPS = 1e-6
CTX_IN_W = 2 * KV_W + SSM_W
IN_W = CTX_IN_W + ATTN_W + 2 * D_MODEL
SPLITS = [KV_W, 2 * KV_W, CTX_IN_W, CTX_IN_W + ATTN_W]

kernel_name = 'hybrid_s5_gqa_macaron_dit_layer'


def _rms_norm(x, g):
    xf = x.astype(jnp.float32)
    xf = xf * lax.rsqrt(jnp.mean(xf * xf, axis=-1, keepdims=True) + NORM_EPS)
    return xf.astype(x.dtype) * g


def _modulate(h, shift, scale):
    return h * (1 + scale) + shift


def _swiglu(h, w_gate, w_up, w_down):
    return (jax.nn.silu(h @ w_gate) * (h @ w_up)) @ w_down


def _axial_rope_tables(L):
    rows = L // GRID_W
    row_ids = jnp.broadcast_to(jnp.arange(rows)[:, None], (rows, GRID_W)).reshape(-1)
    col_ids = jnp.broadcast_to(jnp.arange(GRID_W)[None, :], (rows, GRID_W)).reshape(-1)
    half = ROPE_AXIS_DIM // 2
    inv_freq = ROPE_THETA ** (-jnp.arange(half, dtype=jnp.float32) / half)
    ang_r = row_ids.astype(jnp.float32)[:, None, None] * inv_freq
    ang_c = col_ids.astype(jnp.float32)[:, None, None] * inv_freq
    return (jnp.cos(ang_r), jnp.sin(ang_r), jnp.cos(ang_c), jnp.sin(ang_c))


def _rope_half(x, cos, sin):
    cos = cos.astype(x.dtype)
    sin = sin.astype(x.dtype)
    x1, x2 = jnp.split(x, 2, axis=-1)
    return jnp.concatenate([x1 * cos - x2 * sin, x2 * cos + x1 * sin], axis=-1)


def _axial_rope(x, tables):
    cos_r, sin_r, cos_c, sin_c = tables
    return jnp.concatenate([_rope_half(x[..., :ROPE_AXIS_DIM], cos_r, sin_r),
                            _rope_half(x[..., ROPE_AXIS_DIM:], cos_c, sin_c)], axis=-1)


def _attend_block(qb, k, v):
    B, T = qb.shape[0], qb.shape[1]
    qg = qb.reshape(B, T, N_KV_HEADS, Q_PER_KV, HEAD_DIM)
    s = jnp.einsum('bqkrd,bskd->bkrqs', qg, k).astype(jnp.float32) * ATTN_SCALE
    p = jax.nn.softmax(s, axis=-1).astype(v.dtype)
    o = jnp.einsum('bkrqs,bskd->bqkrd', p, v)
    return o.reshape(B, T, ATTN_W)


def _blocked_attention(q, k, v):
    B, L = q.shape[0], q.shape[1]
    nb = L // Q_BLOCK
    qb = q.reshape(B, nb, Q_BLOCK, N_Q_HEADS, HEAD_DIM).swapaxes(0, 1)
    o = lax.map(lambda qi: _attend_block(qi, k, v), qb)
    return o.swapaxes(0, 1).reshape(B, L, ATTN_W)


def _zoh(a_re, a_im, log_dt):
    a_re = a_re.astype(jnp.float32)
    a_im = a_im.astype(jnp.float32)
    dt = jnp.exp(log_dt.astype(jnp.float32))[:, None]
    mag = jnp.exp(a_re * dt)
    lb_re = mag * jnp.cos(a_im * dt)
    lb_im = mag * jnp.sin(a_im * dt)
    den = a_re * a_re + a_im * a_im
    coef_re = ((lb_re - 1.0) * a_re + lb_im * a_im) / den
    coef_im = (lb_im * a_re - (lb_re - 1.0) * a_im) / den
    return lb_re, lb_im, coef_re, coef_im


def _drive(u, b_re, b_im, coef_re, coef_im):
    bu_re = jnp.einsum('blgc,gpc->blgp', u, b_re)
    bu_im = jnp.einsum('blgc,gpc->blgp', u, b_im)
    return coef_re * bu_re - coef_im * bu_im, coef_re * bu_im + coef_im * bu_re


def _combine(e1, e2):
    a1r, a1i, b1r, b1i = e1
    a2r, a2i, b2r, b2i = e2
    return (a2r * a1r - a2i * a1i,
            a2r * a1i + a2i * a1r,
            a2r * b1r - a2i * b1i + b2r,
            a2r * b1i + a2i * b1r + b2i)


def _scan(lb_re, lb_im, bu_re, bu_im, reverse, h0=None):
    L = bu_re.shape[1]
    a_re = jnp.broadcast_to(lb_re, (1, L) + lb_re.shape)
    a_im = jnp.broadcast_to(lb_im, (1, L) + lb_im.shape)
    A_re, A_im, s_re, s_im = lax.associative_scan(_combine, (a_re, a_im, bu_re, bu_im),
                                                  reverse=reverse, axis=1)
    if h0 is None:
        return s_re, s_im
    h0_re, h0_im = h0
    return (s_re + A_re * h0_re - A_im * h0_im, s_im + A_re * h0_im + A_im * h0_re)


def _readout(h_re, h_im, c_re, c_im):
    return (jnp.einsum('blgp,gcp->blgc', h_re, c_re)
            - jnp.einsum('blgp,gcp->blgc', h_im, c_im))


def _s5_mixer(u, uc, a_re, a_im, log_dt, b_re, b_im, c_re, c_im, d, ctx_out):
    B, L = u.shape[0], u.shape[1]
    Lc = uc.shape[1]
    uf = u.astype(jnp.float32).reshape(B, L, SSM_GROUPS, SSM_GROUP)
    ucf = uc.astype(jnp.float32).reshape(B, Lc, SSM_GROUPS, SSM_GROUP)
    d_g = d.astype(jnp.float32).reshape(SSM_GROUPS, SSM_GROUP)
    y = d_g * uf
    yc = d_g * ucf if ctx_out else None
    for direction, reverse in ((0, False), (1, True)):
        lb_re, lb_im, coef_re, coef_im = _zoh(a_re[direction], a_im[direction], log_dt[direction])
        br = b_re[direction].astype(jnp.float32)
        bi = b_im[direction].astype(jnp.float32)
        cr = c_re[direction].astype(jnp.float32)
        ci = c_im[direction].astype(jnp.float32)
        dc_re, dc_im = _drive(ucf, br, bi, coef_re, coef_im)
        hc_re, hc_im = _scan(lb_re, lb_im, dc_re, dc_im, reverse)
        edge = slice(0, 1) if reverse else slice(Lc - 1, Lc)
        h0 = (hc_re[:, edge], hc_im[:, edge])
        dl_re, dl_im = _drive(uf, br, bi, coef_re, coef_im)
        h_re, h_im = _scan(lb_re, lb_im, dl_re, dl_im, reverse, h0)
        y = y + _readout(h_re, h_im, cr, ci)
        if ctx_out:
            yc = yc + _readout(hc_re, hc_im, cr, ci)
    y = y.reshape(B, L, SSM_W).astype(u.dtype)
    if ctx_out:
        yc = yc.reshape(B, Lc, SSM_W).astype(u.dtype)
    return y, yc


def _merge(attn, ssm, gate, w_glu, b_glu, w_br_attn, w_br_ssm, w_out):
    y = jax.nn.gelu(ssm)
    y = y * jax.nn.sigmoid(y @ w_glu + b_glu)
    g_attn, g_ssm = jnp.split(jax.nn.sigmoid(gate), 2, axis=-1)
    merged = g_attn * (attn @ w_br_attn) + g_ssm * (y @ w_br_ssm)
    return merged @ w_out


def _token_mixer(h, hc, rope, w_in, q_g, k_g, a_re, a_im, log_dt, b_re, b_im, c_re, c_im, d,
                 w_glu, b_glu, w_br_attn, w_br_ssm, w_out, ctx_out):
    B, L = h.shape[0], h.shape[1]
    Lc = hc.shape[1]
    k, v, u, q, gate = jnp.split(h @ w_in, SPLITS, axis=-1)
    pc = hc @ (w_in if ctx_out else w_in[:, :CTX_IN_W])
    kc, vc, uc = pc[..., :KV_W], pc[..., KV_W:2 * KV_W], pc[..., 2 * KV_W:CTX_IN_W]
    q = _axial_rope(_rms_norm(q.reshape(B, L, N_Q_HEADS, HEAD_DIM), q_g), rope)
    k = _axial_rope(_rms_norm(k.reshape(B, L, N_KV_HEADS, HEAD_DIM), k_g), rope)
    kc = _rms_norm(kc.reshape(B, Lc, N_KV_HEADS, HEAD_DIM), k_g)
    vc = vc.reshape(B, Lc, N_KV_HEADS, HEAD_DIM)
    v = v.reshape(B, L, N_KV_HEADS, HEAD_DIM)
    k_all = jnp.concatenate([kc, k], axis=1)
    v_all = jnp.concatenate([vc, v], axis=1)
    attn = _blocked_attention(q, k_all, v_all)
    ssm, ssm_c = _s5_mixer(u, uc, a_re, a_im, log_dt, b_re, b_im, c_re, c_im, d, ctx_out)
    out = _merge(attn, ssm, gate, w_glu, b_glu, w_br_attn, w_br_ssm, w_out)
    out_c = None
    if ctx_out:
        qc = _rms_norm(pc[..., CTX_IN_W:CTX_IN_W + ATTN_W].reshape(B, Lc, N_Q_HEADS, HEAD_DIM), q_g)
        attn_c = _attend_block(qc, kc, vc)
        out_c = _merge(attn_c, ssm_c, pc[..., CTX_IN_W + ATTN_W:], w_glu, b_glu,
                       w_br_attn, w_br_ssm, w_out)
    return out, out_c


def setup_inputs(seed: int = 0) -> dict:
    key = jax.random.key(seed)
    ks = jax.random.split(key, 32)
    f32 = jnp.float32

    def nrm(k, shape, scale):
        return jax.random.normal(k, shape, f32) * scale

    G, P, E = SSM_GROUPS, SSM_STATE, SSM_GROUP
    n_idx = jnp.arange(P, dtype=f32)
    return {
        'x': nrm(ks[0], (BATCH, SEQ, D_MODEL), 1.0),
        'c': nrm(ks[1], (BATCH, D_MODEL), 1.0),
        'ctx': nrm(ks[2], (BATCH, CTX_LEN, D_MODEL), 1.0),
        'c_ctx': nrm(ks[3], (D_MODEL,), 1.0),
        'w_mod': nrm(ks[4], (DEPTH, D_MODEL, N_MOD * D_MODEL), 0.5 * D_MODEL ** -0.5),
        'b_mod': nrm(ks[5], (DEPTH, N_MOD * D_MODEL), 0.01),
        'norm_g': 1.0 + nrm(ks[6], (DEPTH, 3, D_MODEL), 0.02),
        'w_ffn1_gate': nrm(ks[7], (DEPTH, D_MODEL, D_FF), D_MODEL ** -0.5),
        'w_ffn1_up': nrm(ks[8], (DEPTH, D_MODEL, D_FF), D_MODEL ** -0.5),
        'w_ffn1_down': nrm(ks[9], (DEPTH, D_FF, D_MODEL), D_FF ** -0.5),
        'w_in': nrm(ks[10], (DEPTH, D_MODEL, IN_W), D_MODEL ** -0.5),
        'q_norm_g': 1.0 + nrm(ks[11], (DEPTH, HEAD_DIM), 0.02),
        'k_norm_g': 1.0 + nrm(ks[12], (DEPTH, HEAD_DIM), 0.02),
        'ssm_a_re': -0.5 + nrm(ks[13], (DEPTH, 2, G, P), 0.01),
        'ssm_a_im': math.pi * n_idx + nrm(ks[14], (DEPTH, 2, G, P), 0.01),
        'ssm_log_dt': jax.random.uniform(ks[15], (DEPTH, 2, G), f32,
                                         math.log(DT_MIN), math.log(DT_MAX)),
        'ssm_b_re': nrm(ks[16], (DEPTH, 2, G, P, E), (2 * E) ** -0.5),
        'ssm_b_im': nrm(ks[17], (DEPTH, 2, G, P, E), (2 * E) ** -0.5),
        'ssm_c_re': nrm(ks[18], (DEPTH, 2, G, E, P), P ** -0.5),
        'ssm_c_im': nrm(ks[19], (DEPTH, 2, G, E, P), P ** -0.5),
        'ssm_d': nrm(ks[20], (DEPTH, SSM_W), 1.0),
        'w_glu': nrm(ks[21], (DEPTH, SSM_W, SSM_W), SSM_W ** -0.5),
        'b_glu': nrm(ks[22], (DEPTH, SSM_W), 0.01),
        'w_br_attn': nrm(ks[23], (DEPTH, ATTN_W, D_MODEL), ATTN_W ** -0.5),
        'w_br_ssm': nrm(ks[24], (DEPTH, SSM_W, D_MODEL), SSM_W ** -0.5),
        'w_out': nrm(ks[25], (DEPTH, D_MODEL, D_MODEL), D_MODEL ** -0.5),
        'w_ffn2_gate': nrm(ks[26], (DEPTH, D_MODEL, D_FF), D_MODEL ** -0.5),
        'w_ffn2_up': nrm(ks[27], (DEPTH, D_MODEL, D_FF), D_MODEL ** -0.5),
        'w_ffn2_down': nrm(ks[28], (DEPTH, D_FF, D_MODEL), D_FF ** -0.5),
    }


def reference(x, c, ctx, c_ctx, w_mod, b_mod, norm_g, w_ffn1_gate, w_ffn1_up, w_ffn1_down,
              w_in, q_norm_g, k_norm_g, ssm_a_re, ssm_a_im, ssm_log_dt, ssm_b_re, ssm_b_im,
              ssm_c_re, ssm_c_im, ssm_d, w_glu, b_glu, w_br_attn, w_br_ssm, w_out,
              w_ffn2_gate, w_ffn2_up, w_ffn2_down):
    L = x.shape[1]
    rope = _axial_rope_tables(L)
    silu_c = jax.nn.silu(c)
    silu_cc = jax.nn.silu(c_ctx)
    for l in range(DEPTH):
        last = l == DEPTH - 1
        n_ctx_mod = N_MOD_CTX_LAST if last else N_MOD
        mod = (silu_c @ w_mod[l] + b_mod[l])[:, None, :]
        sh1, sc1, g1, sh2, sc2, g2, sh3, sc3, g3 = jnp.split(mod, N_MOD, axis=-1)
        mod_c = silu_cc @ w_mod[l][:, :n_ctx_mod * D_MODEL] + b_mod[l][:n_ctx_mod * D_MODEL]
        mc = jnp.split(mod_c, n_ctx_mod, axis=-1)
        ffn1 = (w_ffn1_gate[l], w_ffn1_up[l], w_ffn1_down[l])
        ffn2 = (w_ffn2_gate[l], w_ffn2_up[l], w_ffn2_down[l])
        x = x + 0.5 * g1 * _swiglu(_modulate(_rms_norm(x, norm_g[l, 0]), sh1, sc1), *ffn1)
        ctx = ctx + 0.5 * mc[2] * _swiglu(_modulate(_rms_norm(ctx, norm_g[l, 0]), mc[0], mc[1]), *ffn1)
        h = _modulate(_rms_norm(x, norm_g[l, 1]), sh2, sc2)
        hc = _modulate(_rms_norm(ctx, norm_g[l, 1]), mc[3], mc[4])
        mix, mix_c = _token_mixer(h, hc, rope, w_in[l], q_norm_g[l], k_norm_g[l],
                                  ssm_a_re[l], ssm_a_im[l], ssm_log_dt[l], ssm_b_re[l], ssm_b_im[l],
                                  ssm_c_re[l], ssm_c_im[l], ssm_d[l], w_glu[l], b_glu[l],
                                  w_br_attn[l], w_br_ssm[l], w_out[l], not last)
        x = x + g2 * mix
        x = x + 0.5 * g3 * _swiglu(_modulate(_rms_norm(x, norm_g[l, 2]), sh3, sc3), *ffn2)
        if not last:
            ctx = ctx + mc[5] * mix_c
            ctx = ctx + 0.5 * mc[8] * _swiglu(_modulate(_rms_norm(ctx, norm_g[l, 2]), mc[6], mc[7]), *ffn2)
    return x
```

```python
import functools

import jax
import jax.numpy as jnp
from jax import lax
from jax.experimental import pallas as pl
from jax.experimental.pallas import tpu as pltpu

F32 = jnp.float32
BF16 = jnp.bfloat16

GRID_W = 64
ROPE_THETA = 10000.0
NORM_EPS = 1e-6
N_MOD = 9
Q_PER_KV = 4
SSM_CHUNK = 16
VMEM_LIMIT_BYTES = 56 * 1024 * 1024


def _tile(n, pref, mult=8):
    if n <= pref:
        return n
    for t in range(pref, 0, -1):
        if n % t == 0 and t % mult == 0:
            return t
    return n


def _params(*sem):
    return pltpu.CompilerParams(dimension_semantics=sem, vmem_limit_bytes=VMEM_LIMIT_BYTES)


def _mod_kernel(c_ref, w_ref, b_ref, o_ref):
    c = c_ref[...]
    s = c * jax.nn.sigmoid(c)
    o_ref[...] = jnp.dot(s.astype(BF16), w_ref[...].astype(BF16),
                         preferred_element_type=F32) + b_ref[...]


def _modulation(cc, w_mod, b_mod):
    D, M = w_mod.shape
    tn = _tile(M, 1024, 128)
    return pl.pallas_call(
        _mod_kernel,
        grid=(M // tn,),
        in_specs=[pl.BlockSpec((cc.shape[0], D), lambda j: (0, 0)),
                  pl.BlockSpec((D, tn), lambda j: (0, j)),
                  pl.BlockSpec((1, tn), lambda j: (0, j))],
        out_specs=pl.BlockSpec((cc.shape[0], tn), lambda j: (0, j)),
        out_shape=jax.ShapeDtypeStruct((cc.shape[0], M), F32),
        compiler_params=_params("arbitrary"),
        name="modulation",
    )(cc, w_mod, b_mod)


def _norm_mod_kernel(x_ref, g_ref, sh_ref, sc_ref, o_ref):
    x = x_ref[...]
    ms = jnp.mean(x * x, axis=-1, keepdims=True)
    h = (x * lax.rsqrt(ms + NORM_EPS)) * g_ref[...]
    o_ref[...] = (h * (1.0 + sc_ref[...]) + sh_ref[...]).astype(o_ref.dtype)


def _norm_mod(x, g, sh, sc, rows_per_mod):
    N, D = x.shape
    tm = _tile(rows_per_mod, 256)
    tpb = rows_per_mod // tm
    vec = pl.BlockSpec((None, 1, D), lambda i: (i // tpb, 0, 0))
    return pl.pallas_call(
        _norm_mod_kernel,
        grid=(N // tm,),
        in_specs=[pl.BlockSpec((tm, D), lambda i: (i, 0)),
                  pl.BlockSpec((1, D), lambda i: (0, 0)), vec, vec],
        out_specs=pl.BlockSpec((tm, D), lambda i: (i, 0)),
        out_shape=jax.ShapeDtypeStruct((N, D), BF16),
        compiler_params=_params("arbitrary"),
        name="norm_mod",
    )(x, g, sh, sc)


def _ffn_kernel(h_ref, x_ref, gate_ref, wg_ref, wu_ref, wd_ref, o_ref):
    j = pl.program_id(1)
    h = h_ref[...]
    g = jnp.dot(h, wg_ref[...], preferred_element_type=F32)
    u = jnp.dot(h, wu_ref[...], preferred_element_type=F32)
    a = ((g * jax.nn.sigmoid(g)) * u).astype(BF16)
    part = jnp.dot(a, wd_ref[...], preferred_element_type=F32)

    @pl.when(j == 0)
    def _():
        o_ref[...] = part

    @pl.when(j > 0)
    def _():
        o_ref[...] += part

    @pl.when(j == pl.num_programs(1) - 1)
    def _():
        o_ref[...] = x_ref[...] + (0.5 * gate_ref[...]) * o_ref[...]


def _ffn(h, x, gate, wg, wu, wd, rows_per_mod):
    N, D = x.shape
    F = wg.shape[1]
    tm = _tile(rows_per_mod, 512)
    tf = _tile(F, 512, 128)
    tpb = rows_per_mod // tm
    return pl.pallas_call(
        _ffn_kernel,
        grid=(N // tm, F // tf),
        in_specs=[pl.BlockSpec((tm, D), lambda i, j: (i, 0)),
                  pl.BlockSpec((tm, D), lambda i, j: (i, 0)),
                  pl.BlockSpec((None, 1, D), lambda i, j: (i // tpb, 0, 0)),
                  pl.BlockSpec((D, tf), lambda i, j: (0, j)),
                  pl.BlockSpec((D, tf), lambda i, j: (0, j)),
                  pl.BlockSpec((tf, D), lambda i, j: (j, 0))],
        out_specs=pl.BlockSpec((tm, D), lambda i, j: (i, 0)),
        out_shape=jax.ShapeDtypeStruct((N, D), F32),
        compiler_params=_params("arbitrary", "arbitrary"),
        name="ffn",
    )(h, x, gate, wg, wu, wd)


def _proj_kernel(h_ref, w_ref, cos_ref, sin_ref, g_ref, o_ref, *, mode, head_dim):
    acc = jnp.dot(h_ref[...], w_ref[...], preferred_element_type=F32)
    if mode == "plain":
        o_ref[...] = acc.astype(o_ref.dtype)
        return
    g = g_ref[...]
    quarter = head_dim // 4
    for hh in range(acc.shape[1] // head_dim):
        sl = slice(hh * head_dim, (hh + 1) * head_dim)
        xh = acc[:, sl]
        ms = jnp.mean(xh * xh, axis=-1, keepdims=True)
        xh = (xh * lax.rsqrt(ms + NORM_EPS)) * g
        if mode == "norm_rope":
            lane = lax.broadcasted_iota(jnp.int32, xh.shape, 1)
            partner = jnp.where((lane % (2 * quarter)) < quarter,
                                pltpu.roll(xh, head_dim - quarter, 1),
                                pltpu.roll(xh, quarter, 1))
            xh = xh * cos_ref[...] + partner * sin_ref[...]
        o_ref[:, sl] = xh.astype(o_ref.dtype)


def _proj(h, w, col0, ncols, out_dtype, mode, rows_per_seq, cos, sin, g):
    N, D = h.shape
    head_dim = g.shape[1]
    tm = _tile(rows_per_seq, 512)
    tn = _tile(ncols, 512, 128)
    assert col0 % tn == 0
    tpb = rows_per_seq // tm
    off = col0 // tn
    return pl.pallas_call(
        functools.partial(_proj_kernel, mode=mode, head_dim=head_dim),
        grid=(N // tm, ncols // tn),
        in_specs=[pl.BlockSpec((tm, D), lambda i, j: (i, 0)),
                  pl.BlockSpec((D, tn), lambda i, j: (0, j + off)),
                  pl.BlockSpec((tm, head_dim), lambda i, j: (i % tpb, 0)),
                  pl.BlockSpec((tm, head_dim), lambda i, j: (i % tpb, 0)),
                  pl.BlockSpec((1, head_dim), lambda i, j: (0, 0))],
        out_specs=pl.BlockSpec((tm, tn), lambda i, j: (i, j)),
        out_shape=jax.ShapeDtypeStruct((N, ncols), out_dtype),
        compiler_params=_params("arbitrary", "arbitrary"),
        name="proj_" + mode,
    )(h, w, cos, sin, g)


def _rope_tables(L, head_dim):
    t = jnp.arange(L)
    row = (t // GRID_W).astype(F32)
    col = (t % GRID_W).astype(F32)
    half = head_dim // 4
    inv_freq = ROPE_THETA ** (-jnp.arange(half, dtype=F32) / half)
    ar = row[:, None] * inv_freq
    ac = col[:, None] * inv_freq
    cos = jnp.concatenate([jnp.cos(ar), jnp.cos(ar), jnp.cos(ac), jnp.cos(ac)], axis=-1)
    sin = jnp.concatenate([-jnp.sin(ar), jnp.sin(ar), -jnp.sin(ac), jnp.sin(ac)], axis=-1)
    return cos, sin


def _attn_kernel(q_ref, k_ref, v_ref, o_ref, *, head_dim):
    k = k_ref[...]
    v = v_ref[...]
    for r in range(q_ref.shape[1] // head_dim):
        sl = slice(r * head_dim, (r + 1) * head_dim)
        s = lax.dot_general(q_ref[:, sl], k, (((1,), (1,)), ((), ())),
                            preferred_element_type=F32)
        m = jnp.max(s, axis=-1, keepdims=True)
        p = jnp.exp(s - m)
        l = jnp.sum(p, axis=-1, keepdims=True)
        o = jnp.dot(p.astype(BF16), v, preferred_element_type=F32)
        o_ref[:, sl] = (o / l).astype(o_ref.dtype)


def _attention(q, k_all, v_all, L, head_dim):
    N, W = q.shape
    B, S, KW = k_all.shape
    kvh = KW // head_dim
    gw = W // kvh
    tq = _tile(L, 256)
    tpb = L // tq
    return pl.pallas_call(
        functools.partial(_attn_kernel, head_dim=head_dim),
        grid=(B, kvh, tpb),
        in_specs=[pl.BlockSpec((tq, gw), lambda b, h, i: (b * tpb + i, h)),
                  pl.BlockSpec((None, S, head_dim), lambda b, h, i: (b, 0, h)),
                  pl.BlockSpec((None, S, head_dim), lambda b, h, i: (b, 0, h))],
        out_specs=pl.BlockSpec((tq, gw), lambda b, h, i: (b * tpb + i, h)),
        out_shape=jax.ShapeDtypeStruct((N, W), BF16),
        compiler_params=_params("arbitrary", "arbitrary", "arbitrary"),
        name="attention",
    )(q, k_all, v_all)


def _ssm_operators(a_re, a_im, log_dt, b_re, b_im, c_re, c_im, T):
    hp = lax.Precision.HIGHEST
    a_re, a_im = a_re.astype(F32), a_im.astype(F32)
    dt = jnp.exp(log_dt.astype(F32))[..., None]
    mag = jnp.exp(a_re * dt)
    lr = mag * jnp.cos(a_im * dt)
    li = mag * jnp.sin(a_im * dt)
    den = a_re * a_re + a_im * a_im
    cr = ((lr - 1.0) * a_re + li * a_im) / den
    ci = (li * a_re - (lr - 1.0) * a_im) / den
    pr, pi = [jnp.ones_like(lr)], [jnp.zeros_like(lr)]
    for _ in range(T):
        pr.append(pr[-1] * lr - pi[-1] * li)
        pi.append(pr[-2] * li + pi[-1] * lr)
    pw_r, pw_i = jnp.stack(pr), jnp.stack(pi)
    b_re, b_im = b_re.astype(F32), b_im.astype(F32)
    c_re, c_im = c_re.astype(F32), c_im.astype(F32)
    bb_r = cr[..., None] * b_re - ci[..., None] * b_im
    bb_i = cr[..., None] * b_im + ci[..., None] * b_re
    _, G, P, E = b_re.shape
    cp_r = c_re[None] * pw_r[:T, :, :, None, :] - c_im[None] * pw_i[:T, :, :, None, :]
    cp_i = c_re[None] * pw_i[:T, :, :, None, :] + c_im[None] * pw_r[:T, :, :, None, :]
    K = (jnp.einsum("kdgep,dgpc->kdgec", cp_r, bb_r, precision=hp)
         - jnp.einsum("kdgep,dgpc->kdgec", cp_i, bb_i, precision=hp))
    s_idx = jnp.arange(T)[:, None]
    t_idx = jnp.arange(T)[None, :]

    def toeplitz(Kd, lag):
        blk = Kd[jnp.clip(lag, 0, T - 1)]
        blk = jnp.where((lag >= 0)[:, :, None, None, None], blk, 0.0)
        return blk.transpose(2, 0, 4, 1, 3).reshape(G, T * E, T * E)

    m_intra = toeplitz(K[:, 0], t_idx - s_idx) + toeplitz(K[:, 1], s_idx - t_idx)

    def state_in(d, lag):
        wr, wi = pw_r[lag, d], pw_i[lag, d]
        re = wr[:, :, :, None] * bb_r[d][None] - wi[:, :, :, None] * bb_i[d][None]
        im = wr[:, :, :, None] * bb_i[d][None] + wi[:, :, :, None] * bb_r[d][None]
        both = jnp.concatenate([re, im], axis=2)
        return both.transpose(1, 0, 3, 2).reshape(G, T * E, 2 * P)

    def state_out(d, lag):
        wr, wi = pw_r[lag, d], pw_i[lag, d]
        re = c_re[d][None] * wr[:, :, None, :] - c_im[d][None] * wi[:, :, None, :]
        im = c_re[d][None] * wi[:, :, None, :] + c_im[d][None] * wr[:, :, None, :]
        both = jnp.concatenate([re, -im], axis=3)
        return both.transpose(1, 3, 0, 2).reshape(G, 2 * P, T * E)

    steps = jnp.arange(T)
    m_in = jnp.stack([state_in(0, T - 1 - steps), state_in(1, steps)])
    m_out = jnp.stack([state_out(0, steps + 1), state_out(1, T - steps)])
    decay = jnp.stack([pw_r[T], pw_i[T]], axis=1)
    return m_intra.astype(BF16), m_in.astype(BF16), m_out.astype(BF16), decay


def _ssm_state_kernel(u_ref, min_ref, s_ref):
    for g in range(u_ref.shape[0]):
        u = u_ref[g]
        for d in range(2):
            s_ref[d, g] = jnp.dot(u, min_ref[d, g], preferred_element_type=F32)


def _ssm_scan_kernel(sre_ref, sim_ref, dre_ref, dim_ref, hre_ref, him_ref, *, nc_ctx):
    d = pl.program_id(0)
    nc = sre_ref.shape[0]
    ar = dre_ref[...]
    ai = dim_ref[...]

    def body(n, carry):
        hr, hi = carry
        idx_b = jnp.where(n < nc_ctx, nc_ctx - 1 - n, nc - 1 - n + nc_ctx)
        idx = jnp.where(d == 0, n, idx_b)
        hre_ref[idx] = hr.astype(hre_ref.dtype)
        him_ref[idx] = hi.astype(him_ref.dtype)
        return (ar * hr - ai * hi + sre_ref[idx], ar * hi + ai * hr + sim_ref[idx])

    zero = jnp.zeros(ar.shape, F32)
    lax.fori_loop(0, nc, body, (zero, zero))


def _ssm_out_kernel(u_ref, h_ref, mi_ref, mo_ref, y_ref):
    for g in range(u_ref.shape[0]):
        y = jnp.dot(u_ref[g], mi_ref[g], preferred_element_type=F32)
        for d in range(2):
            y += jnp.dot(h_ref[d, g], mo_ref[d, g], preferred_element_type=F32)
        y_ref[g] = y


def _s5(u_cat, nc_ctx, ops):
    m_intra, m_in, m_out, decay = ops
    B, S, W = u_cat.shape
    G = m_intra.shape[0]
    E = W // G
    T = SSM_CHUNK
    NC = S // T
    R = NC * B
    P = decay.shape[-1]
    TE = T * E
    ug = u_cat.astype(BF16).reshape(B, NC, T, G, E).transpose(3, 1, 0, 2, 4).reshape(G, R, TE)
    gb = _tile(G, 8, 1)
    s = pl.pallas_call(
        _ssm_state_kernel,
        grid=(G // gb,),
        in_specs=[pl.BlockSpec((gb, R, TE), lambda g: (g, 0, 0)),
                  pl.BlockSpec((2, gb, TE, 2 * P), lambda g: (0, g, 0, 0))],
        out_specs=pl.BlockSpec((2, gb, R, 2 * P), lambda g: (0, g, 0, 0)),
        out_shape=jax.ShapeDtypeStruct((2, G, R, 2 * P), F32),
        compiler_params=_params("arbitrary"),
        name="ssm_state",
    )(ug, m_in)
    RR = G * B * P // 128
    s = s.reshape(2, G, NC, B, 2, P).transpose(0, 4, 2, 1, 3, 5).reshape(2, 2, NC, RR, 128)
    dec = jnp.broadcast_to(decay[:, :, :, None, :], (2, 2, G, B, P)).reshape(2, 2, RR, 128)
    rb = _tile(RR, 32)
    sspec = [pl.BlockSpec((None, None, NC, rb, 128), lambda d, r, c=c: (d, c, 0, r, 0)) for c in range(2)]
    dspec = [pl.BlockSpec((None, None, rb, 128), lambda d, r, c=c: (d, c, r, 0)) for c in range(2)]
    h_re, h_im = pl.pallas_call(
        functools.partial(_ssm_scan_kernel, nc_ctx=nc_ctx),
        grid=(2, RR // rb),
        in_specs=sspec + dspec,
        out_specs=[pl.BlockSpec((None, NC, rb, 128), lambda d, r: (d, 0, r, 0))] * 2,
        out_shape=[jax.ShapeDtypeStruct((2, NC, RR, 128), BF16)] * 2,
        compiler_params=_params("arbitrary", "arbitrary"),
        name="ssm_scan",
    )(s, s, dec, dec)
    h = jnp.stack([h_re, h_im], axis=1)
    h = h.reshape(2, 2, NC, G, B, P).transpose(0, 3, 2, 4, 1, 5).reshape(2, G, R, 2 * P)
    y = pl.pallas_call(
        _ssm_out_kernel,
        grid=(G // gb,),
        in_specs=[pl.BlockSpec((gb, R, TE), lambda g: (g, 0, 0)),
                  pl.BlockSpec((2, gb, R, 2 * P), lambda g: (0, g, 0, 0)),
                  pl.BlockSpec((gb, TE, TE), lambda g: (g, 0, 0)),
                  pl.BlockSpec((2, gb, 2 * P, TE), lambda g: (0, g, 0, 0))],
        out_specs=pl.BlockSpec((gb, R, TE), lambda g: (g, 0, 0)),
        out_shape=jax.ShapeDtypeStruct((G, R, TE), F32),
        compiler_params=_params("arbitrary"),
        name="ssm_out",
    )(ug, h, m_intra, m_out)
    return y.reshape(G, NC, B, T, E).transpose(2, 1, 3, 0, 4).reshape(B, S, W)


def _glu_kernel(y_ref, u_ref, d_ref, w_ref, b_ref, o_ref):
    y = y_ref[...] + d_ref[...] * u_ref[...]
    y = jax.nn.gelu(y)
    z = jnp.dot(y.astype(BF16), w_ref[...], preferred_element_type=F32) + b_ref[...]
    o_ref[...] = (y * jax.nn.sigmoid(z)).astype(o_ref.dtype)


def _glu(y, u, d, w, b):
    N, W = y.shape
    tm = _tile(N, 512)
    row = pl.BlockSpec((tm, W), lambda i: (i, 0))
    vec = pl.BlockSpec((1, W), lambda i: (0, 0))
    return pl.pallas_call(
        _glu_kernel,
        grid=(N // tm,),
        in_specs=[row, row, vec, pl.BlockSpec((W, W), lambda i: (0, 0)), vec],
        out_specs=row,
        out_shape=jax.ShapeDtypeStruct((N, W), BF16),
        compiler_params=_params("arbitrary"),
        name="ssm_glu",
    )(y, u, d, w, b)


def _merge_kernel(a_ref, y_ref, ga_ref, gs_ref, wa_ref, ws_ref, o_ref):
    pa = jnp.dot(a_ref[...], wa_ref[...], preferred_element_type=F32)
    ps = jnp.dot(y_ref[...], ws_ref[...], preferred_element_type=F32)
    o_ref[...] = (jax.nn.sigmoid(ga_ref[...]) * pa
                  + jax.nn.sigmoid(gs_ref[...]) * ps).astype(o_ref.dtype)


def _merge(attn, y, gate, wa, ws):
    N, WA = attn.shape
    WS = y.shape[1]
    D = wa.shape[1]
    tm = _tile(N, 512)
    tn = _tile(D, 512, 128)
    nd = D // tn
    return pl.pallas_call(
        _merge_kernel,
        grid=(N // tm, nd),
        in_specs=[pl.BlockSpec((tm, WA), lambda i, j: (i, 0)),
                  pl.BlockSpec((tm, WS), lambda i, j: (i, 0)),
                  pl.BlockSpec((tm, tn), lambda i, j: (i, j)),
                  pl.BlockSpec((tm, tn), lambda i, j: (i, j + nd)),
                  pl.BlockSpec((WA, tn), lambda i, j: (0, j)),
                  pl.BlockSpec((WS, tn), lambda i, j: (0, j))],
        out_specs=pl.BlockSpec((tm, tn), lambda i, j: (i, j)),
        out_shape=jax.ShapeDtypeStruct((N, D), BF16),
        compiler_params=_params("arbitrary", "arbitrary"),
        name="merge",
    )(attn, y, gate, gate, wa, ws)


def _out_kernel(m_ref, x_ref, g_ref, w_ref, o_ref):
    o_ref[...] = x_ref[...] + g_ref[...] * jnp.dot(m_ref[...], w_ref[...],
                                                   preferred_element_type=F32)


def _out_proj(m, x, gate, w, rows_per_mod):
    N, D = x.shape
    K = m.shape[1]
    tm = _tile(rows_per_mod, 512)
    tn = _tile(D, 512, 128)
    tpb = rows_per_mod // tm
    return pl.pallas_call(
        _out_kernel,
        grid=(N // tm, D // tn),
        in_specs=[pl.BlockSpec((tm, K), lambda i, j: (i, 0)),
                  pl.BlockSpec((tm, tn), lambda i, j: (i, j)),
                  pl.BlockSpec((None, 1, tn), lambda i, j: (i // tpb, 0, j)),
                  pl.BlockSpec((K, tn), lambda i, j: (0, j))],
        out_specs=pl.BlockSpec((tm, tn), lambda i, j: (i, j)),
        out_shape=jax.ShapeDtypeStruct((N, D), F32),
        compiler_params=_params("arbitrary", "arbitrary"),
        name="out_proj",
    )(m, x, gate, w)


def kernel(x, c, ctx, c_ctx, w_mod, b_mod, norm_g, w_ffn1_gate, w_ffn1_up, w_ffn1_down, w_in, q_norm_g, k_norm_g, ssm_a_re, ssm_a_im, ssm_log_dt, ssm_b_re, ssm_b_im, ssm_c_re, ssm_c_im, ssm_d, w_glu, b_glu, w_br_attn, w_br_ssm, w_out, w_ffn2_gate, w_ffn2_up, w_ffn2_down):
    B, L, D = x.shape
    Lc = ctx.shape[1]
    depth = w_mod.shape[0]
    assert depth == 1, "only the single (last) layer configuration is implemented"
    hd = q_norm_g.shape[1]
    G = ssm_a_re.shape[2]
    ssm_w = w_glu.shape[1]
    attn_w = w_br_attn.shape[1]
    kv_w = attn_w // Q_PER_KV
    ctx_in_w = 2 * kv_w + ssm_w
    assert L % GRID_W == 0 and L % SSM_CHUNK == 0 and Lc % SSM_CHUNK == 0 and B <= 7
    N, Nc = B * L, B * Lc
    l = 0

    cc = jnp.zeros((8, D), F32).at[:B].set(c).at[B].set(c_ctx)
    mod = _modulation(cc, w_mod[l], b_mod[l][None, :])

    def mvec(k, lo, hi):
        return mod[lo:hi, k * D:(k + 1) * D][:, None, :]

    xm = [mvec(k, 0, B) for k in range(N_MOD)]
    cm = [mvec(k, B, B + 1) for k in range(5)]
    ng = norm_g[l][:, None, :]

    x2 = x.reshape(N, D)
    c2 = ctx.reshape(Nc, D)
    w1 = [w.astype(BF16) for w in (w_ffn1_gate[l], w_ffn1_up[l], w_ffn1_down[l])]
    w2 = [w.astype(BF16) for w in (w_ffn2_gate[l], w_ffn2_up[l], w_ffn2_down[l])]
    w_in_b = w_in[l].astype(BF16)

    x2 = _ffn(_norm_mod(x2, ng[0], xm[0], xm[1], L), x2, xm[2], *w1, L)
    c2 = _ffn(_norm_mod(c2, ng[0], cm[0], cm[1], Nc), c2, cm[2], *w1, Nc)

    h = _norm_mod(x2, ng[1], xm[3], xm[4], L)
    hc = _norm_mod(c2, ng[1], cm[3], cm[4], Nc)
    cos, sin = _rope_tables(L, hd)
    kg = k_norm_g[l][None, :]
    qg = q_norm_g[l][None, :] * (hd ** -0.5)
    k = _proj(h, w_in_b, 0, kv_w, BF16, "norm_rope", L, cos, sin, kg)
    v = _proj(h, w_in_b, kv_w, kv_w, BF16, "plain", L, cos, sin, kg)
    u = _proj(h, w_in_b, 2 * kv_w, ssm_w, F32, "plain", L, cos, sin, kg)
    q = _proj(h, w_in_b, ctx_in_w, attn_w, BF16, "norm_rope", L, cos, sin, qg)
    gate = _proj(h, w_in_b, ctx_in_w + attn_w, 2 * D, F32, "plain", L, cos, sin, kg)
    kc = _proj(hc, w_in_b, 0, kv_w, BF16, "norm", Lc, cos, sin, kg)
    vc = _proj(hc, w_in_b, kv_w, kv_w, BF16, "plain", Lc, cos, sin, kg)
    uc = _proj(hc, w_in_b, 2 * kv_w, ssm_w, F32, "plain", Lc, cos, sin, kg)

    k_all = jnp.concatenate([kc.reshape(B, Lc, kv_w), k.reshape(B, L, kv_w)], axis=1)
    v_all = jnp.concatenate([vc.reshape(B, Lc, kv_w), v.reshape(B, L, kv_w)], axis=1)
    attn = _attention(q, k_all, v_all, L, hd)

    ops = _ssm_operators(ssm_a_re[l], ssm_a_im[l], ssm_log_dt[l], ssm_b_re[l], ssm_b_im[l],
                         ssm_c_re[l], ssm_c_im[l], SSM_CHUNK)
    u_cat = jnp.concatenate([uc.reshape(B, Lc, ssm_w), u.reshape(B, L, ssm_w)], axis=1)
    y_ssm = _s5(u_cat, Lc // SSM_CHUNK, ops)[:, Lc:].reshape(N, ssm_w)

    y = _glu(y_ssm, u, ssm_d[l][None, :], w_glu[l].astype(BF16), b_glu[l][None, :])
    merged = _merge(attn, y, gate, w_br_attn[l].astype(BF16), w_br_ssm[l].astype(BF16))
    x2 = _out_proj(merged, x2, xm[5], w_out[l].astype(BF16), L)

    x2 = _ffn(_norm_mod(x2, ng[2], xm[6], xm[7], L), x2, xm[8], *w2, L)
    return x2.reshape(B, L, D)
```

```python
import functools
import math

import jax
import jax.numpy as jnp
from jax import lax
from jax.experimental import pallas as pl
from jax.experimental.pallas import tpu as pltpu

F32 = jnp.float32
BF16 = jnp.bfloat16

GRID_W = 64
ROPE_THETA = 10000.0
NORM_EPS = 1e-6
N_MOD = 9
Q_PER_KV = 4
SSM_CHUNK = 16
LANES = 128
VMEM_LIMIT_BYTES = 58 * 1024 * 1024


def _tile(n, pref, mult=8):
    if n <= pref:
        return n
    for t in range(pref, 0, -1):
        if n % t == 0 and t % mult == 0:
            return t
    return n


def _params(*sem):
    return pltpu.CompilerParams(dimension_semantics=sem, vmem_limit_bytes=VMEM_LIMIT_BYTES)


def _norm_mod(x, g, sh, sc):
    ms = jnp.mean(x * x, axis=-1, keepdims=True)
    h = (x * lax.rsqrt(ms + NORM_EPS)) * g
    return (h * (1.0 + sc) + sh).astype(BF16)


def _mod_kernel(c_ref, w_ref, b_ref, o_ref):
    c = c_ref[...]
    s = c * jax.nn.sigmoid(c)
    o_ref[...] = jnp.dot(s.astype(BF16), w_ref[...].astype(BF16),
                         preferred_element_type=F32) + b_ref[...]


def _modulation(cc, w_mod, b_mod):
    D, M = w_mod.shape
    tn = _tile(M, 1024, LANES)
    return pl.pallas_call(
        _mod_kernel,
        grid=(M // tn,),
        in_specs=[pl.BlockSpec((cc.shape[0], D), lambda j: (0, 0)),
                  pl.BlockSpec((D, tn), lambda j: (0, j)),
                  pl.BlockSpec((1, tn), lambda j: (0, j))],
        out_specs=pl.BlockSpec((cc.shape[0], tn), lambda j: (0, j)),
        out_shape=jax.ShapeDtypeStruct((cc.shape[0], M), F32),
        compiler_params=_params("arbitrary"),
        name="modulation",
    )(cc, w_mod, b_mod)


def _ffn_kernel(x_ref, ng_ref, sh_ref, sc_ref, gate_ref, wg_ref, wu_ref, wd_ref, o_ref, h_ref,
                *, sub):
    j = pl.program_id(1)
    tm = x_ref.shape[0]

    @pl.when(j == 0)
    def _():
        for r in range(0, tm, sub):
            h_ref[r:r + sub, :] = _norm_mod(x_ref[r:r + sub, :], ng_ref[...], sh_ref[...], sc_ref[...])

    for r in range(0, tm, sub):
        h = h_ref[r:r + sub, :]
        g = jnp.dot(h, wg_ref[...], preferred_element_type=F32)
        u = jnp.dot(h, wu_ref[...], preferred_element_type=F32)
        a = ((g * jax.nn.sigmoid(g)) * u).astype(BF16)
        part = jnp.dot(a, wd_ref[...], preferred_element_type=F32)

        @pl.when(j == 0)
        def _():
            o_ref[r:r + sub, :] = part

        @pl.when(j > 0)
        def _():
            o_ref[r:r + sub, :] += part

    @pl.when(j == pl.num_programs(1) - 1)
    def _():
        half_gate = 0.5 * gate_ref[...]
        for r in range(0, tm, sub):
            o_ref[r:r + sub, :] = x_ref[r:r + sub, :] + half_gate * o_ref[r:r + sub, :]


def _ffn(x, ng, sh, sc, gate, wg, wu, wd, rows_per_mod):
    N, D = x.shape
    F = wg.shape[1]
    tm = _tile(rows_per_mod, 1024)
    tf = _tile(F, 512, LANES)
    tpb = rows_per_mod // tm
    vec = pl.BlockSpec((None, 1, D), lambda i, j: (i // tpb, 0, 0))
    return pl.pallas_call(
        functools.partial(_ffn_kernel, sub=_tile(tm, 512)),
        grid=(N // tm, F // tf),
        in_specs=[pl.BlockSpec((tm, D), lambda i, j: (i, 0), pipeline_mode=pl.Buffered(1)),
                  pl.BlockSpec((1, D), lambda i, j: (0, 0)), vec, vec, vec,
                  pl.BlockSpec((D, tf), lambda i, j: (0, j)),
                  pl.BlockSpec((D, tf), lambda i, j: (0, j)),
                  pl.BlockSpec((tf, D), lambda i, j: (j, 0))],
        out_specs=pl.BlockSpec((tm, D), lambda i, j: (i, 0)),
        out_shape=jax.ShapeDtypeStruct((N, D), F32),
        scratch_shapes=[pltpu.VMEM((tm, D), BF16)],
        compiler_params=_params("arbitrary", "arbitrary"),
        name="ffn",
    )(x, ng, sh, sc, gate, wg, wu, wd)


def _head_norm(acc, g, cos, sin, head_dim):
    quarter = head_dim // 4
    out = []
    for hh in range(acc.shape[1] // head_dim):
        xh = acc[:, hh * head_dim:(hh + 1) * head_dim]
        ms = jnp.mean(xh * xh, axis=-1, keepdims=True)
        xh = (xh * lax.rsqrt(ms + NORM_EPS)) * g
        if cos is not None:
            lane = lax.broadcasted_iota(jnp.int32, xh.shape, 1)
            partner = jnp.where((lane % (2 * quarter)) < quarter,
                                pltpu.roll(xh, head_dim - quarter, 1),
                                pltpu.roll(xh, quarter, 1))
            xh = xh * cos + partner * sin
        out.append(xh.astype(BF16))
    return out[0] if len(out) == 1 else jnp.concatenate(out, axis=1)


def _proj_kernel(x_ref, ng_ref, sh_ref, sc_ref, w_ref, cos_ref, sin_ref, kg_ref, qg_ref,
                 k_ref, v_ref, u_ref, *rest, bounds, rope, head_dim, sub):
    h_ref = rest[-1]
    j = pl.program_id(1)
    tm = x_ref.shape[0]
    jv, ju, jq, jg = bounds

    @pl.when(j == 0)
    def _():
        for r in range(0, tm, sub):
            h_ref[r:r + sub, :] = _norm_mod(x_ref[r:r + sub, :], ng_ref[...], sh_ref[...], sc_ref[...])

    acc = jnp.dot(h_ref[...], w_ref[...], preferred_element_type=F32)
    cos = cos_ref[...] if rope else None
    sin = sin_ref[...] if rope else None

    @pl.when(j < jv)
    def _():
        k_ref[...] = _head_norm(acc, kg_ref[...], cos, sin, head_dim)

    @pl.when((j >= jv) & (j < ju))
    def _():
        v_ref[...] = acc.astype(BF16)

    @pl.when((j >= ju) & (j < jq))
    def _():
        u_ref[...] = acc

    if len(rest) == 3:
        q_ref, gate_ref = rest[0], rest[1]

        @pl.when((j >= jq) & (j < jg))
        def _():
            q_ref[...] = _head_norm(acc, qg_ref[...], cos, sin, head_dim)

        @pl.when(j >= jg)
        def _():
            gate_ref[...] = jax.nn.sigmoid(acc).astype(BF16)


def _proj(x, ng, sh, sc, w, cos, sin, kg, qg, rows_per_seq, widths, latent):
    N, D = x.shape
    kv_w, ssm_w, attn_w, gate_w = widths
    head_dim = kg.shape[1]
    tn = kv_w
    assert tn % LANES == 0 and ssm_w % tn == 0 and attn_w % tn == 0 and gate_w % tn == 0
    jv, ju = 1, 2
    jq = ju + ssm_w // tn
    jg = jq + attn_w // tn
    nj = jg + gate_w // tn if latent else jq
    tm = _tile(rows_per_seq, 1024)
    tpb = rows_per_seq // tm
    vec = pl.BlockSpec((None, 1, D), lambda i, j: (i // tpb, 0, 0))
    tab = pl.BlockSpec((tm, head_dim), lambda i, j: (i % tpb, 0))
    hvec = pl.BlockSpec((1, head_dim), lambda i, j: (0, 0))

    def out_spec(j0, nblk):
        return pl.BlockSpec((tm, tn), lambda i, j: (i, jnp.clip(j - j0, 0, nblk - 1)))

    out_specs = [out_spec(0, 1), out_spec(jv, 1), out_spec(ju, jq - ju)]
    out_shape = [jax.ShapeDtypeStruct((N, kv_w), BF16), jax.ShapeDtypeStruct((N, kv_w), BF16),
                 jax.ShapeDtypeStruct((N, ssm_w), F32)]
    if latent:
        out_specs += [out_spec(jq, jg - jq), out_spec(jg, nj - jg)]
        out_shape += [jax.ShapeDtypeStruct((N, attn_w), BF16), jax.ShapeDtypeStruct((N, gate_w), BF16)]
    return pl.pallas_call(
        functools.partial(_proj_kernel, bounds=(jv, ju, jq, jg), rope=latent, head_dim=head_dim,
                          sub=_tile(tm, 512)),
        grid=(N // tm, nj),
        in_specs=[pl.BlockSpec((tm, D), lambda i, j: (i, 0)),
                  pl.BlockSpec((1, D), lambda i, j: (0, 0)), vec, vec,
                  pl.BlockSpec((D, tn), lambda i, j: (0, j)),
                  tab, tab, hvec, hvec],
        out_specs=out_specs,
        out_shape=out_shape,
        scratch_shapes=[pltpu.VMEM((tm, D), BF16)],
        compiler_params=_params("arbitrary", "arbitrary"),
        name="proj_latent" if latent else "proj_context",
    )(x, ng, sh, sc, w, cos, sin, kg, qg)


def _rope_tables(L, head_dim):
    t = jnp.arange(L)
    row = (t // GRID_W).astype(F32)
    col = (t % GRID_W).astype(F32)
    half = head_dim // 4
    inv_freq = ROPE_THETA ** (-jnp.arange(half, dtype=F32) / half)
    ar = row[:, None] * inv_freq
    ac = col[:, None] * inv_freq
    cos = jnp.concatenate([jnp.cos(ar), jnp.cos(ar), jnp.cos(ac), jnp.cos(ac)], axis=-1)
    sin = jnp.concatenate([-jnp.sin(ar), jnp.sin(ar), -jnp.sin(ac), jnp.sin(ac)], axis=-1)
    return cos, sin


def _attn_kernel(q_ref, kc_ref, k_ref, vc_ref, v_ref, o_ref, kall_ref, vext_ref, *, head_dim):
    lc = kc_ref.shape[0]

    @pl.when(pl.program_id(2) == 0)
    def _():
        kall_ref[0:lc, :] = kc_ref[...]
        kall_ref[lc:, :] = k_ref[...]
        vext_ref[0:lc, 0:head_dim] = vc_ref[...]
        vext_ref[lc:, 0:head_dim] = v_ref[...]
        vext_ref[:, head_dim:] = jnp.ones((vext_ref.shape[0], head_dim), BF16)

    k = kall_ref[...]
    vext = vext_ref[...]
    for r in range(q_ref.shape[1] // head_dim):
        sl = slice(r * head_dim, (r + 1) * head_dim)
        s = lax.dot_general(q_ref[:, sl], k, (((1,), (1,)), ((), ())),
                            preferred_element_type=F32)
        m = jnp.max(s, axis=-1, keepdims=True)
        p = jnp.exp2(s - m).astype(BF16)
        oe = jnp.dot(p, vext, preferred_element_type=F32)
        o_ref[:, sl] = (oe[:, :head_dim] / oe[:, head_dim:]).astype(o_ref.dtype)


def _attention(q, kc, k, vc, v, B, head_dim):
    N, W = q.shape
    L, Lc = N // B, kc.shape[0] // B
    kvh = k.shape[1] // head_dim
    gw = W // kvh
    tq = _tile(L, 512)
    tpb = L // tq
    kv_lat = pl.BlockSpec((L, head_dim), lambda b, h, i: (b, h))
    kv_ctx = pl.BlockSpec((Lc, head_dim), lambda b, h, i: (b, h))
    return pl.pallas_call(
        functools.partial(_attn_kernel, head_dim=head_dim),
        grid=(B, kvh, tpb),
        in_specs=[pl.BlockSpec((tq, gw), lambda b, h, i: (b * tpb + i, h)),
                  kv_ctx, kv_lat, kv_ctx, kv_lat],
        out_specs=pl.BlockSpec((tq, gw), lambda b, h, i: (b * tpb + i, h)),
        out_shape=jax.ShapeDtypeStruct((N, W), BF16),
        scratch_shapes=[pltpu.VMEM((L + Lc, head_dim), BF16),
                        pltpu.VMEM((L + Lc, 2 * head_dim), BF16)],
        compiler_params=_params("arbitrary", "arbitrary", "arbitrary"),
        name="attention",
    )(q, kc, k, vc, v)


def _ssm_operators(a_re, a_im, log_dt, b_re, b_im, c_re, c_im):
    T = SSM_CHUNK
    hp = lax.Precision.HIGHEST
    a_re, a_im = a_re.astype(F32), a_im.astype(F32)
    dt = jnp.exp(log_dt.astype(F32))[..., None]
    mag = jnp.exp(a_re * dt)
    lr = mag * jnp.cos(a_im * dt)
    li = mag * jnp.sin(a_im * dt)
    den = a_re * a_re + a_im * a_im
    cr = ((lr - 1.0) * a_re + li * a_im) / den
    ci = (li * a_re - (lr - 1.0) * a_im) / den
    pr, pi = [jnp.ones_like(lr)], [jnp.zeros_like(lr)]
    for _ in range(T):
        pr.append(pr[-1] * lr - pi[-1] * li)
        pi.append(pr[-2] * li + pi[-1] * lr)
    pw_r, pw_i = jnp.stack(pr), jnp.stack(pi)
    _, G, P, E = b_re.shape
    bt_r = jnp.swapaxes(b_re.astype(F32), 2, 3)
    bt_i = jnp.swapaxes(b_im.astype(F32), 2, 3)
    bb_r = cr[:, :, None, :] * bt_r - ci[:, :, None, :] * bt_i
    bb_i = cr[:, :, None, :] * bt_i + ci[:, :, None, :] * bt_r
    c_r, c_i = c_re.astype(F32), c_im.astype(F32)

    def cmul(xr, xi, wr, wi):
        wr = jnp.moveaxis(wr, 0, 2)[:, :, :, None, :]
        wi = jnp.moveaxis(wi, 0, 2)[:, :, :, None, :]
        xr, xi = xr[:, :, None], xi[:, :, None]
        return ((xr * wr - xi * wi).reshape(2, G, T * E, P),
                (xr * wi + xi * wr).reshape(2, G, T * E, P))

    steps = jnp.arange(T)

    def strip(d, lags):
        rr, ri = cmul(c_r, c_i, pw_r[lags], pw_i[lags])
        return (jnp.einsum("gcp,gjp->gcj", bb_r[d], rr[d], precision=hp)
                - jnp.einsum("gcp,gjp->gcj", bb_i[d], ri[d], precision=hp))

    w_strip = jnp.stack([strip(0, steps), strip(1, T - 1 - steps)])
    in_f = cmul(bb_r, bb_i, pw_r[T - 1 - steps], pw_i[T - 1 - steps])
    in_b = cmul(bb_r, bb_i, pw_r[steps], pw_i[steps])
    m_in = jnp.stack([jnp.concatenate([in_f[0][0], in_f[1][0], in_f[1][0], in_f[0][0]], axis=-1),
                      jnp.concatenate([in_b[0][1], in_b[1][1], in_b[1][1], in_b[0][1]], axis=-1)])
    out_f = cmul(c_r, c_i, pw_r[steps + 1], pw_i[steps + 1])
    out_b = cmul(c_r, c_i, pw_r[T - steps], pw_i[T - steps])
    m_out = jnp.stack([jnp.concatenate([out_f[0][0], -out_f[1][0]], axis=-1),
                       jnp.concatenate([out_b[0][1], -out_b[1][1]], axis=-1)])
    ar, ai = pw_r[T], pw_i[T]
    dec = jnp.stack([jnp.concatenate([ar, ar], -1), jnp.concatenate([-ai, ai], -1)], axis=2)
    return w_strip, m_in.astype(BF16), m_out.astype(BF16), dec


def _intra_operator(w_ref, g, e):
    T = SSM_CHUNK
    wf, wb = w_ref[0, g], w_ref[1, g]
    t_of_lane = lax.broadcasted_iota(jnp.int32, wf.shape, 1) // e
    rows = []
    for s in range(T):
        f = wf if s == 0 else jnp.where(t_of_lane >= s, pltpu.roll(wf, s * e, 1), 0.0)
        b = wb if s == T - 1 else jnp.where(t_of_lane <= s, pltpu.roll(wb, (s + 1) * e, 1), 0.0)
        rows.append(f + b)
    return jnp.concatenate(rows, axis=0).astype(BF16)


def _row_pitch(n):
    p = -(-n // 8)
    return 8 * (p if p % 2 else p + 1)


def _ssm_kernel(u_ref, uc_ref, d_ref, w_ref, min_ref, mo_ref, dec_ref, y_ref,
                z_ref, s1_ref, s2_ref, hp_ref, *, batch, e):
    T = SSM_CHUNK
    gpb = w_ref.shape[1]
    rp = z_ref.shape[1]
    n_lat, n_ctx = u_ref.shape[0] // T, uc_ref.shape[0] // T
    ncl, ncc = n_lat // batch, n_ctx // batch
    pl_, pc_ = _row_pitch(ncl), _row_pitch(ncc)
    lat0 = batch * pc_
    r = lat0 + batch * pl_
    lanes = u_ref.shape[1]
    pw = hp_ref.shape[3]

    def padded(rows, n, pitch):
        if pitch == n:
            return [rows]
        out = []
        for b in range(batch):
            out += [rows[b * n:(b + 1) * n, :], jnp.zeros((pitch - n, lanes), F32)]
        return out

    xt = []
    for s in range(T):
        parts = (padded(uc_ref[pl.ds(s, n_ctx, stride=T), :], ncc, pc_)
                 + padded(u_ref[pl.ds(s, n_lat, stride=T), :], ncl, pl_))
        if rp > r:
            parts.append(jnp.zeros((rp - r, lanes), F32))
        xt.append(jnp.concatenate(parts, axis=0).T)
    for g in range(gpb):
        zt = jnp.concatenate([xt[s][g * e:(g + 1) * e, :] for s in range(T)], axis=0)
        z_ref[g] = zt.T.astype(BF16)
    for d in range(2):
        for g in range(gpb):
            st = jnp.dot(z_ref[g], min_ref[d, g], preferred_element_type=F32)
            s1_ref[d, g] = st[:, :pw]
            s2_ref[d, g] = st[:, pw:]

    def sweep(base, count, stride, carry):
        def body(it, carry):
            out = []
            for d in range(2):
                n = it if d == 0 else count - 1 - it
                rows = pl.ds(base + n, batch, stride=stride)
                for g in range(gpb):
                    h1, h2 = carry[2 * (d * gpb + g)], carry[2 * (d * gpb + g) + 1]
                    hp_ref[d, g, rows, :] = h1
                    a1 = dec_ref[d, g, 0:1, :]
                    a2 = dec_ref[d, g, 1:2, :]
                    out.append(a1 * h1 + a2 * h2 + s1_ref[d, g, rows, :])
                    out.append(a1 * h2 - a2 * h1 + s2_ref[d, g, rows, :])
            return tuple(out)
        return lax.fori_loop(0, count, body, carry)

    for d in range(2):
        for g in range(gpb):
            for b in range(batch):
                if pc_ > ncc:
                    hp_ref[d, g, b * pc_ + ncc:(b + 1) * pc_, :] = jnp.zeros((pc_ - ncc, pw), F32)
                if pl_ > ncl:
                    hp_ref[d, g, lat0 + b * pl_ + ncl:lat0 + (b + 1) * pl_, :] = jnp.zeros((pl_ - ncl, pw), F32)
            if rp > r:
                hp_ref[d, g, r:rp, :] = jnp.zeros((rp - r, pw), F32)
    zero = jnp.zeros((batch, pw), F32)
    carry = sweep(0, ncc, pc_, tuple([zero] * (4 * gpb)))
    sweep(lat0, ncl, pl_, carry)
    yt = []
    for g in range(gpb):
        y = jnp.dot(z_ref[g], _intra_operator(w_ref, g, e), preferred_element_type=F32)
        for d in range(2):
            y += lax.dot_general(hp_ref[d, g].astype(BF16), mo_ref[d, g],
                                 (((1,), (1,)), ((), ())), preferred_element_type=F32)
        yt.append(y.T)
    dvec = d_ref[...]
    for t in range(T):
        blk = jnp.concatenate([yt[g][t * e:(t + 1) * e, :] for g in range(gpb)], axis=0).T
        for b in range(batch):
            rows = pl.ds(b * ncl * T + t, ncl, stride=T)
            y_ref[rows, :] = (blk[lat0 + b * pl_:lat0 + b * pl_ + ncl, :] + dvec * u_ref[rows, :])


def _s5(u, uc, d, ops, batch):
    w_strip, m_in, m_out, dec = ops
    N, W = u.shape
    Nc = uc.shape[0]
    G = w_strip.shape[1]
    E = W // G
    gpb = LANES // E
    TE = SSM_CHUNK * E
    P2 = dec.shape[-1]
    assert P2 == LANES and W % LANES == 0
    r = batch * (_row_pitch(N // SSM_CHUNK // batch) + _row_pitch(Nc // SSM_CHUNK // batch))
    rp = -(-r // LANES) * LANES
    return pl.pallas_call(
        functools.partial(_ssm_kernel, batch=batch, e=E),
        grid=(W // LANES,),
        in_specs=[pl.BlockSpec((N, LANES), lambda j: (0, j)),
                  pl.BlockSpec((Nc, LANES), lambda j: (0, j)),
                  pl.BlockSpec((1, LANES), lambda j: (0, j)),
                  pl.BlockSpec((2, gpb, E, TE), lambda j: (0, j, 0, 0)),
                  pl.BlockSpec((2, gpb, TE, 2 * P2), lambda j: (0, j, 0, 0)),
                  pl.BlockSpec((2, gpb, TE, P2), lambda j: (0, j, 0, 0)),
                  pl.BlockSpec((2, gpb, 2, P2), lambda j: (0, j, 0, 0))],
        out_specs=pl.BlockSpec((N, LANES), lambda j: (0, j)),
        out_shape=jax.ShapeDtypeStruct((N, W), F32),
        scratch_shapes=[pltpu.VMEM((gpb, rp, TE), BF16),
                        pltpu.VMEM((2, gpb, rp, P2), F32),
                        pltpu.VMEM((2, gpb, rp, P2), F32),
                        pltpu.VMEM((2, gpb, rp, P2), F32)],
        compiler_params=_params("arbitrary"),
        name="s5",
    )(u, uc, d, w_strip, m_in, m_out, dec)


def _mixer_out_kernel(attn_ref, y_ref, ga_ref, gs_ref, x_ref, g2_ref, wglu_ref, bglu_ref,
                      wa_ref, ws_ref, wo_ref, o_ref, y2_ref, m_ref, *, sub):
    j = pl.program_id(1)
    nd = m_ref.shape[0]
    tm, tn = o_ref.shape

    @pl.when(j == 0)
    def _():
        for r in range(0, tm, sub):
            y = jax.nn.gelu(y_ref[r:r + sub, :])
            z = jnp.dot(y.astype(BF16), wglu_ref[...], preferred_element_type=F32) + bglu_ref[...]
            y2_ref[r:r + sub, :] = (y * jax.nn.sigmoid(z)).astype(BF16)

    @pl.when(j < nd)
    def _():
        pa = jnp.dot(attn_ref[...], wa_ref[...], preferred_element_type=F32)
        ps = jnp.dot(y2_ref[...], ws_ref[...], preferred_element_type=F32)
        m_ref[j] = (ga_ref[...].astype(F32) * pa + gs_ref[...].astype(F32) * ps).astype(BF16)

    @pl.when(j >= nd)
    def _():
        acc = jnp.dot(m_ref[0], wo_ref[0:tn, :], preferred_element_type=F32)
        for kk in range(1, nd):
            acc += jnp.dot(m_ref[kk], wo_ref[kk * tn:(kk + 1) * tn, :], preferred_element_type=F32)
        o_ref[...] = x_ref[...] + g2_ref[...] * acc


def _mixer_out(attn, y_ssm, gate, x, g2, w_glu, b_glu, wa, ws, wo, rows_per_mod):
    N, D = x.shape
    WA, WS = attn.shape[1], y_ssm.shape[1]
    tm = _tile(rows_per_mod, 1024)
    tn = _tile(D, 512, LANES)
    nd = D // tn
    tpb = rows_per_mod // tm
    lo = lambda j: jnp.minimum(j, nd - 1)
    hi = lambda j: jnp.maximum(j - nd, 0)
    return pl.pallas_call(
        functools.partial(_mixer_out_kernel, sub=_tile(tm, 512)),
        grid=(N // tm, 2 * nd),
        in_specs=[pl.BlockSpec((tm, WA), lambda i, j: (i, 0)),
                  pl.BlockSpec((tm, WS), lambda i, j: (i, 0)),
                  pl.BlockSpec((tm, tn), lambda i, j: (i, lo(j))),
                  pl.BlockSpec((tm, tn), lambda i, j: (i, nd + lo(j))),
                  pl.BlockSpec((tm, tn), lambda i, j: (i, hi(j))),
                  pl.BlockSpec((None, 1, tn), lambda i, j: (i // tpb, 0, hi(j))),
                  pl.BlockSpec((WS, WS), lambda i, j: (0, 0)),
                  pl.BlockSpec((1, WS), lambda i, j: (0, 0)),
                  pl.BlockSpec((WA, tn), lambda i, j: (0, lo(j))),
                  pl.BlockSpec((WS, tn), lambda i, j: (0, lo(j))),
                  pl.BlockSpec((D, tn), lambda i, j: (0, hi(j)))],
        out_specs=pl.BlockSpec((tm, tn), lambda i, j: (i, hi(j))),
        out_shape=jax.ShapeDtypeStruct((N, D), F32),
        scratch_shapes=[pltpu.VMEM((tm, WS), BF16), pltpu.VMEM((nd, tm, tn), BF16)],
        compiler_params=_params("arbitrary", "arbitrary"),
        name="mixer_out",
    )(attn, y_ssm, gate, gate, x, g2, w_glu, b_glu, wa, ws, wo)


def kernel(x, c, ctx, c_ctx, w_mod, b_mod, norm_g, w_ffn1_gate, w_ffn1_up, w_ffn1_down, w_in, q_norm_g, k_norm_g, ssm_a_re, ssm_a_im, ssm_log_dt, ssm_b_re, ssm_b_im, ssm_c_re, ssm_c_im, ssm_d, w_glu, b_glu, w_br_attn, w_br_ssm, w_out, w_ffn2_gate, w_ffn2_up, w_ffn2_down):
    B, L, D = x.shape
    Lc = ctx.shape[1]
    assert w_mod.shape[0] == 1, "only the single (last) layer configuration is implemented"
    hd = q_norm_g.shape[1]
    ssm_w = w_glu.shape[1]
    attn_w = w_br_attn.shape[1]
    kv_w = attn_w // Q_PER_KV
    widths = (kv_w, ssm_w, attn_w, 2 * D)
    assert L % GRID_W == 0 and L % SSM_CHUNK == 0 and Lc % SSM_CHUNK == 0 and B <= 7
    N, Nc = B * L, B * Lc
    l = 0

    cc = jnp.zeros((8, D), F32).at[:B].set(c).at[B].set(c_ctx)
    mod = _modulation(cc, w_mod[l], b_mod[l][None, :])

    def mvec(k, lo, hi):
        return mod[lo:hi, k * D:(k + 1) * D][:, None, :]

    xm = [mvec(k, 0, B) for k in range(N_MOD)]
    cm = [mvec(k, B, B + 1) for k in range(5)]
    ng = norm_g[l][:, None, :]

    x2 = x.reshape(N, D)
    c2 = ctx.reshape(Nc, D)
    w1 = [w.astype(BF16) for w in (w_ffn1_gate[l], w_ffn1_up[l], w_ffn1_down[l])]
    w2 = [w.astype(BF16) for w in (w_ffn2_gate[l], w_ffn2_up[l], w_ffn2_down[l])]
    w_in_b = w_in[l].astype(BF16)

    x2 = _ffn(x2, ng[0], xm[0], xm[1], xm[2], *w1, L)
    c2 = _ffn(c2, ng[0], cm[0], cm[1], cm[2], *w1, Nc)

    cos, sin = _rope_tables(L, hd)
    kg = k_norm_g[l][None, :]
    qg = q_norm_g[l][None, :] * (hd ** -0.5 * math.log2(math.e))
    k, v, u, q, gate = _proj(x2, ng[1], xm[3], xm[4], w_in_b, cos, sin, kg, qg, L, widths, True)
    kc, vc, uc = _proj(c2, ng[1], cm[3], cm[4], w_in_b, cos, sin, kg, qg, Nc, widths, False)

    attn = _attention(q, kc, k, vc, v, B, hd)
    ops = _ssm_operators(ssm_a_re[l], ssm_a_im[l], ssm_log_dt[l], ssm_b_re[l], ssm_b_im[l],
                         ssm_c_re[l], ssm_c_im[l])
    y_ssm = _s5(u, uc, ssm_d[l][None, :], ops, B)

    x2 = _mixer_out(attn, y_ssm, gate, x2, xm[5], w_glu[l].astype(BF16), b_glu[l][None, :],
                    w_br_attn[l].astype(BF16), w_br_ssm[l].astype(BF16), w_out[l].astype(BF16), L)

    x2 = _ffn(x2, ng[2], xm[6], xm[7], xm[8], *w2, L)
    return x2.reshape(B, L, D)
```

```python
import functools
import math

import jax
import jax.numpy as jnp
from jax import lax
from jax.experimental import pallas as pl
from jax.experimental.pallas import tpu as pltpu

F32 = jnp.float32
BF16 = jnp.bfloat16

GRID_W = 64
ROPE_THETA = 10000.0
NORM_EPS = 1e-6
N_MOD = 9
Q_PER_KV = 4
SSM_CHUNK = 16
LANES = 128
VMEM_LIMIT_BYTES = 58 * 1024 * 1024


def _tile(n, pref, mult=8):
    if n <= pref:
        return n
    for t in range(pref, 0, -1):
        if n % t == 0 and t % mult == 0:
            return t
    return n


def _params(*sem):
    return pltpu.CompilerParams(dimension_semantics=sem, vmem_limit_bytes=VMEM_LIMIT_BYTES)


def _norm_mod(x, g, sh, sc):
    ms = jnp.mean(x * x, axis=-1, keepdims=True)
    h = (x * lax.rsqrt(ms + NORM_EPS)) * g
    return (h * (1.0 + sc) + sh).astype(BF16)


def _mod_kernel(c_ref, w_ref, b_ref, o_ref):
    c = c_ref[...]
    s = c * jax.nn.sigmoid(c)
    o_ref[...] = jnp.dot(s.astype(BF16), w_ref[...].astype(BF16),
                         preferred_element_type=F32) + b_ref[...]


def _modulation(cc, w_mod, b_mod):
    D, M = w_mod.shape
    tn = _tile(M, 1024, LANES)
    return pl.pallas_call(
        _mod_kernel,
        grid=(M // tn,),
        in_specs=[pl.BlockSpec((cc.shape[0], D), lambda j: (0, 0)),
                  pl.BlockSpec((D, tn), lambda j: (0, j)),
                  pl.BlockSpec((1, tn), lambda j: (0, j))],
        out_specs=pl.BlockSpec((cc.shape[0], tn), lambda j: (0, j)),
        out_shape=jax.ShapeDtypeStruct((cc.shape[0], M), F32),
        compiler_params=_params("arbitrary"),
        name="modulation",
    )(cc, w_mod, b_mod)


def _ffn_act_kernel(x_ref, ng_ref, sh_ref, sc_ref, wg_ref, wu_ref, a_ref, h_ref, *, sub):
    @pl.when(pl.program_id(1) == 0)
    def _():
        for r in range(0, x_ref.shape[0], sub):
            h_ref[r:r + sub, :] = _norm_mod(x_ref[r:r + sub, :], ng_ref[...], sh_ref[...], sc_ref[...])

    h = h_ref[...]
    half = a_ref.shape[1] // 2
    for c in (0, half):
        g = jnp.dot(h, wg_ref[:, c:c + half], preferred_element_type=F32)
        u = jnp.dot(h, wu_ref[:, c:c + half], preferred_element_type=F32)
        a_ref[:, c:c + half] = ((g * jax.nn.sigmoid(g)) * u).astype(a_ref.dtype)


def _ffn_down_kernel(a_ref, x_ref, gate_ref, wd_ref, o_ref):
    half = o_ref.shape[1] // 2
    for c in (0, half):
        acc = jnp.dot(a_ref[...], wd_ref[:, c:c + half], preferred_element_type=F32)
        o_ref[:, c:c + half] = x_ref[:, c:c + half] + (0.5 * gate_ref[:, c:c + half]) * acc


def _ffn(x, ng, sh, sc, gate, wg, wu, wd, rows_per_mod):
    N, D = x.shape
    F = wg.shape[1]
    tm = _tile(rows_per_mod, 1024)
    tf = _tile(F, 512, 2 * LANES)
    tn = _tile(D, 512, 2 * LANES)
    tpb = rows_per_mod // tm
    vec = pl.BlockSpec((None, 1, D), lambda i, j: (i // tpb, 0, 0))
    a = pl.pallas_call(
        functools.partial(_ffn_act_kernel, sub=_tile(tm, 512)),
        grid=(N // tm, F // tf),
        in_specs=[pl.BlockSpec((tm, D), lambda i, j: (i, 0)),
                  pl.BlockSpec((1, D), lambda i, j: (0, 0)), vec, vec,
                  pl.BlockSpec((D, tf), lambda i, j: (0, j)),
                  pl.BlockSpec((D, tf), lambda i, j: (0, j))],
        out_specs=pl.BlockSpec((tm, tf), lambda i, j: (i, j)),
        out_shape=jax.ShapeDtypeStruct((N, F), BF16),
        scratch_shapes=[pltpu.VMEM((tm, D), BF16)],
        compiler_params=_params("arbitrary", "arbitrary"),
        name="ffn_act",
    )(x, ng, sh, sc, wg, wu)
    return pl.pallas_call(
        _ffn_down_kernel,
        grid=(N // tm, D // tn),
        in_specs=[pl.BlockSpec((tm, F), lambda i, j: (i, 0)),
                  pl.BlockSpec((tm, tn), lambda i, j: (i, j)),
                  pl.BlockSpec((None, 1, tn), lambda i, j: (i // tpb, 0, j)),
                  pl.BlockSpec((F, tn), lambda i, j: (0, j))],
        out_specs=pl.BlockSpec((tm, tn), lambda i, j: (i, j)),
        out_shape=jax.ShapeDtypeStruct((N, D), F32),
        compiler_params=_params("arbitrary", "arbitrary"),
        name="ffn_down",
    )(a, x, gate, wd)


def _head_norm(acc, g, cos, sin, head_dim):
    quarter = head_dim // 4
    out = []
    for hh in range(acc.shape[1] // head_dim):
        xh = acc[:, hh * head_dim:(hh + 1) * head_dim]
        ms = jnp.mean(xh * xh, axis=-1, keepdims=True)
        xh = (xh * lax.rsqrt(ms + NORM_EPS)) * g
        if cos is not None:
            lane = lax.broadcasted_iota(jnp.int32, xh.shape, 1)
            partner = jnp.where((lane % (2 * quarter)) < quarter,
                                pltpu.roll(xh, head_dim - quarter, 1),
                                pltpu.roll(xh, quarter, 1))
            xh = xh * cos + partner * sin
        out.append(xh.astype(BF16))
    return out[0] if len(out) == 1 else jnp.concatenate(out, axis=1)


def _proj_kernel(x_ref, ng_ref, sh_ref, sc_ref, w_ref, cos_ref, sin_ref, kg_ref, qg_ref,
                 k_ref, v_ref, u_ref, *rest, bounds, rope, head_dim, sub):
    h_ref = rest[-1]
    j = pl.program_id(1)
    tm = x_ref.shape[0]
    jv, ju, jq, jg = bounds

    @pl.when(j == 0)
    def _():
        for r in range(0, tm, sub):
            h_ref[r:r + sub, :] = _norm_mod(x_ref[r:r + sub, :], ng_ref[...], sh_ref[...], sc_ref[...])

    acc = jnp.dot(h_ref[...], w_ref[...], preferred_element_type=F32)
    cos = cos_ref[...] if rope else None
    sin = sin_ref[...] if rope else None

    @pl.when(j < jv)
    def _():
        k_ref[...] = _head_norm(acc, kg_ref[...], cos, sin, head_dim)

    @pl.when((j >= jv) & (j < ju))
    def _():
        v_ref[...] = acc.astype(BF16)

    @pl.when((j >= ju) & (j < jq))
    def _():
        u_ref[...] = acc

    if len(rest) == 3:
        q_ref, gate_ref = rest[0], rest[1]

        @pl.when((j >= jq) & (j < jg))
        def _():
            q_ref[...] = _head_norm(acc, qg_ref[...], cos, sin, head_dim)

        @pl.when(j >= jg)
        def _():
            gate_ref[...] = jax.nn.sigmoid(acc).astype(BF16)


def _proj(x, ng, sh, sc, w, cos, sin, kg, qg, rows_per_seq, widths, latent):
    N, D = x.shape
    kv_w, ssm_w, attn_w, gate_w = widths
    head_dim = kg.shape[1]
    tn = kv_w
    assert tn % LANES == 0 and ssm_w % tn == 0 and attn_w % tn == 0 and gate_w % tn == 0
    jv, ju = 1, 2
    jq = ju + ssm_w // tn
    jg = jq + attn_w // tn
    nj = jg + gate_w // tn if latent else jq
    tm = _tile(rows_per_seq, 1024)
    tpb = rows_per_seq // tm
    vec = pl.BlockSpec((None, 1, D), lambda i, j: (i // tpb, 0, 0))
    tab = pl.BlockSpec((tm, head_dim), lambda i, j: (i % tpb, 0))
    hvec = pl.BlockSpec((1, head_dim), lambda i, j: (0, 0))

    def out_spec(j0, nblk):
        return pl.BlockSpec((tm, tn), lambda i, j: (i, jnp.clip(j - j0, 0, nblk - 1)))

    out_specs = [out_spec(0, 1), out_spec(jv, 1), out_spec(ju, jq - ju)]
    out_shape = [jax.ShapeDtypeStruct((N, kv_w), BF16), jax.ShapeDtypeStruct((N, kv_w), BF16),
                 jax.ShapeDtypeStruct((N, ssm_w), F32)]
    if latent:
        out_specs += [out_spec(jq, jg - jq), out_spec(jg, nj - jg)]
        out_shape += [jax.ShapeDtypeStruct((N, attn_w), BF16), jax.ShapeDtypeStruct((N, gate_w), BF16)]
    return pl.pallas_call(
        functools.partial(_proj_kernel, bounds=(jv, ju, jq, jg), rope=latent, head_dim=head_dim,
                          sub=_tile(tm, 512)),
        grid=(N // tm, nj),
        in_specs=[pl.BlockSpec((tm, D), lambda i, j: (i, 0)),
                  pl.BlockSpec((1, D), lambda i, j: (0, 0)), vec, vec,
                  pl.BlockSpec((D, tn), lambda i, j: (0, j)),
                  tab, tab, hvec, hvec],
        out_specs=out_specs,
        out_shape=out_shape,
        scratch_shapes=[pltpu.VMEM((tm, D), BF16)],
        compiler_params=_params("arbitrary", "arbitrary"),
        name="proj_latent" if latent else "proj_context",
    )(x, ng, sh, sc, w, cos, sin, kg, qg)


def _rope_tables(L, head_dim):
    t = jnp.arange(L)
    row = (t // GRID_W).astype(F32)
    col = (t % GRID_W).astype(F32)
    half = head_dim // 4
    inv_freq = ROPE_THETA ** (-jnp.arange(half, dtype=F32) / half)
    ar = row[:, None] * inv_freq
    ac = col[:, None] * inv_freq
    cos = jnp.concatenate([jnp.cos(ar), jnp.cos(ar), jnp.cos(ac), jnp.cos(ac)], axis=-1)
    sin = jnp.concatenate([-jnp.sin(ar), jnp.sin(ar), -jnp.sin(ac), jnp.sin(ac)], axis=-1)
    return cos, sin


def _attn_kernel(q_ref, kc_ref, k_ref, vc_ref, v_ref, o_ref, kall_ref, vext_ref, *, head_dim):
    lc = kc_ref.shape[0]

    @pl.when(pl.program_id(2) == 0)
    def _():
        kall_ref[0:lc, :] = kc_ref[...]
        kall_ref[lc:, :] = k_ref[...]
        vext_ref[0:lc, 0:head_dim] = vc_ref[...]
        vext_ref[lc:, 0:head_dim] = v_ref[...]
        vext_ref[:, head_dim:] = jnp.ones((vext_ref.shape[0], head_dim), BF16)

    k = kall_ref[...]
    vext = vext_ref[...]
    for r in range(q_ref.shape[1] // head_dim):
        sl = slice(r * head_dim, (r + 1) * head_dim)
        s = lax.dot_general(q_ref[:, sl], k, (((1,), (1,)), ((), ())),
                            preferred_element_type=F32)
        m = jnp.max(s, axis=-1, keepdims=True)
        p = jnp.exp2(s - m).astype(BF16)
        oe = jnp.dot(p, vext, preferred_element_type=F32)
        o_ref[:, sl] = (oe[:, :head_dim] / oe[:, head_dim:]).astype(o_ref.dtype)


def _attention(q, kc, k, vc, v, B, head_dim):
    N, W = q.shape
    L, Lc = N // B, kc.shape[0] // B
    kvh = k.shape[1] // head_dim
    gw = W // kvh
    tq = _tile(L, 512)
    tpb = L // tq
    kv_lat = pl.BlockSpec((L, head_dim), lambda b, h, i: (b, h))
    kv_ctx = pl.BlockSpec((Lc, head_dim), lambda b, h, i: (b, h))
    return pl.pallas_call(
        functools.partial(_attn_kernel, head_dim=head_dim),
        grid=(B, kvh, tpb),
        in_specs=[pl.BlockSpec((tq, gw), lambda b, h, i: (b * tpb + i, h)),
                  kv_ctx, kv_lat, kv_ctx, kv_lat],
        out_specs=pl.BlockSpec((tq, gw), lambda b, h, i: (b * tpb + i, h)),
        out_shape=jax.ShapeDtypeStruct((N, W), BF16),
        scratch_shapes=[pltpu.VMEM((L + Lc, head_dim), BF16),
                        pltpu.VMEM((L + Lc, 2 * head_dim), BF16)],
        compiler_params=_params("arbitrary", "arbitrary", "arbitrary"),
        name="attention",
    )(q, kc, k, vc, v)


def _ssm_operators(a_re, a_im, log_dt, b_re, b_im, c_re, c_im):
    T = SSM_CHUNK
    hp = lax.Precision.HIGHEST
    a_re, a_im = a_re.astype(F32), a_im.astype(F32)
    dt = jnp.exp(log_dt.astype(F32))[..., None]
    mag = jnp.exp(a_re * dt)
    lr = mag * jnp.cos(a_im * dt)
    li = mag * jnp.sin(a_im * dt)
    den = a_re * a_re + a_im * a_im
    cr = ((lr - 1.0) * a_re + li * a_im) / den
    ci = (li * a_re - (lr - 1.0) * a_im) / den
    pr, pi = [jnp.ones_like(lr)], [jnp.zeros_like(lr)]
    for _ in range(T):
        pr.append(pr[-1] * lr - pi[-1] * li)
        pi.append(pr[-2] * li + pi[-1] * lr)
    pw_r, pw_i = jnp.stack(pr), jnp.stack(pi)
    _, G, P, E = b_re.shape
    bt_r = jnp.swapaxes(b_re.astype(F32), 2, 3)
    bt_i = jnp.swapaxes(b_im.astype(F32), 2, 3)
    bb_r = cr[:, :, None, :] * bt_r - ci[:, :, None, :] * bt_i
    bb_i = cr[:, :, None, :] * bt_i + ci[:, :, None, :] * bt_r
    c_r, c_i = c_re.astype(F32), c_im.astype(F32)

    def cmul(xr, xi, wr, wi):
        wr = jnp.moveaxis(wr, 0, 2)[:, :, :, None, :]
        wi = jnp.moveaxis(wi, 0, 2)[:, :, :, None, :]
        xr, xi = xr[:, :, None], xi[:, :, None]
        return ((xr * wr - xi * wi).reshape(2, G, T * E, P),
                (xr * wi + xi * wr).reshape(2, G, T * E, P))

    steps = jnp.arange(T)

    def strip(d, lags):
        rr, ri = cmul(c_r, c_i, pw_r[lags], pw_i[lags])
        return (jnp.einsum("gcp,gjp->gcj", bb_r[d], rr[d], precision=hp)
                - jnp.einsum("gcp,gjp->gcj", bb_i[d], ri[d], precision=hp))

    w_strip = jnp.stack([strip(0, steps), strip(1, T - 1 - steps)])
    in_f = cmul(bb_r, bb_i, pw_r[T - 1 - steps], pw_i[T - 1 - steps])
    in_b = cmul(bb_r, bb_i, pw_r[steps], pw_i[steps])
    m_in = jnp.stack([jnp.concatenate([in_f[0][0], in_f[1][0], in_f[1][0], in_f[0][0]], axis=-1),
                      jnp.concatenate([in_b[0][1], in_b[1][1], in_b[1][1], in_b[0][1]], axis=-1)])
    out_f = cmul(c_r, c_i, pw_r[steps + 1], pw_i[steps + 1])
    out_b = cmul(c_r, c_i, pw_r[T - steps], pw_i[T - steps])
    m_out = jnp.stack([jnp.concatenate([out_f[0][0], -out_f[1][0]], axis=-1),
                       jnp.concatenate([out_b[0][1], -out_b[1][1]], axis=-1)])
    ar, ai = pw_r[T], pw_i[T]
    dec = jnp.stack([jnp.concatenate([ar, ar], -1), jnp.concatenate([-ai, ai], -1)], axis=2)
    return w_strip, m_in.astype(BF16), m_out.astype(BF16), dec


def _intra_operator(w_ref, g, e):
    T = SSM_CHUNK
    wf, wb = w_ref[0, g], w_ref[1, g]
    t_of_lane = lax.broadcasted_iota(jnp.int32, wf.shape, 1) // e
    rows = []
    for s in range(T):
        f = wf if s == 0 else jnp.where(t_of_lane >= s, pltpu.roll(wf, s * e, 1), 0.0)
        b = wb if s == T - 1 else jnp.where(t_of_lane <= s, pltpu.roll(wb, (s + 1) * e, 1), 0.0)
        rows.append(f + b)
    return jnp.concatenate(rows, axis=0).astype(BF16)


def _row_pitch(n):
    p = -(-n // 8)
    return 8 * (p if p % 2 else p + 1)


def _ssm_kernel(u_ref, uc_ref, d_ref, w_ref, min_ref, mo_ref, dec_ref, y_ref,
                z_ref, s1_ref, s2_ref, hp_ref, *, batch, e):
    T = SSM_CHUNK
    gpb = w_ref.shape[1]
    rp = z_ref.shape[1]
    n_lat, n_ctx = u_ref.shape[0] // T, uc_ref.shape[0] // T
    ncl, ncc = n_lat // batch, n_ctx // batch
    pl_, pc_ = _row_pitch(ncl), _row_pitch(ncc)
    lat0 = batch * pc_
    r = lat0 + batch * pl_
    lanes = u_ref.shape[1]
    pw = hp_ref.shape[3]

    def padded(rows, n, pitch):
        if pitch == n:
            return [rows]
        out = []
        for b in range(batch):
            out += [rows[b * n:(b + 1) * n, :], jnp.zeros((pitch - n, lanes), F32)]
        return out

    xt = []
    for s in range(T):
        parts = (padded(uc_ref[pl.ds(s, n_ctx, stride=T), :], ncc, pc_)
                 + padded(u_ref[pl.ds(s, n_lat, stride=T), :], ncl, pl_))
        if rp > r:
            parts.append(jnp.zeros((rp - r, lanes), F32))
        xt.append(jnp.concatenate(parts, axis=0).T)
    for g in range(gpb):
        zt = jnp.concatenate([xt[s][g * e:(g + 1) * e, :] for s in range(T)], axis=0)
        z_ref[g] = zt.T.astype(BF16)
    for d in range(2):
        for g in range(gpb):
            st = jnp.dot(z_ref[g], min_ref[d, g], preferred_element_type=F32)
            s1_ref[d, g] = st[:, :pw]
            s2_ref[d, g] = st[:, pw:]

    def sweep(base, count, stride, carry):
        def body(it, carry):
            out = []
            for d in range(2):
                n = it if d == 0 else count - 1 - it
                rows = pl.ds(base + n, batch, stride=stride)
                for g in range(gpb):
                    h1, h2 = carry[2 * (d * gpb + g)], carry[2 * (d * gpb + g) + 1]
                    hp_ref[d, g, rows, :] = h1
                    a1 = dec_ref[d, g, 0:1, :]
                    a2 = dec_ref[d, g, 1:2, :]
                    out.append(a1 * h1 + a2 * h2 + s1_ref[d, g, rows, :])
                    out.append(a1 * h2 - a2 * h1 + s2_ref[d, g, rows, :])
            return tuple(out)
        return lax.fori_loop(0, count, body, carry)

    for d in range(2):
        for g in range(gpb):
            for b in range(batch):
                if pc_ > ncc:
                    hp_ref[d, g, b * pc_ + ncc:(b + 1) * pc_, :] = jnp.zeros((pc_ - ncc, pw), F32)
                if pl_ > ncl:
                    hp_ref[d, g, lat0 + b * pl_ + ncl:lat0 + (b + 1) * pl_, :] = jnp.zeros((pl_ - ncl, pw), F32)
            if rp > r:
                hp_ref[d, g, r:rp, :] = jnp.zeros((rp - r, pw), F32)
    zero = jnp.zeros((batch, pw), F32)
    carry = sweep(0, ncc, pc_, tuple([zero] * (4 * gpb)))
    sweep(lat0, ncl, pl_, carry)
    yt = []
    for g in range(gpb):
        y = jnp.dot(z_ref[g], _intra_operator(w_ref, g, e), preferred_element_type=F32)
        for d in range(2):
            y += lax.dot_general(hp_ref[d, g].astype(BF16), mo_ref[d, g],
                                 (((1,), (1,)), ((), ())), preferred_element_type=F32)
        yt.append(y.T)
    dvec = d_ref[...]
    for t in range(T):
        blk = jnp.concatenate([yt[g][t * e:(t + 1) * e, :] for g in range(gpb)], axis=0).T
        for b in range(batch):
            rows = pl.ds(b * ncl * T + t, ncl, stride=T)
            y_ref[rows, :] = (blk[lat0 + b * pl_:lat0 + b * pl_ + ncl, :] + dvec * u_ref[rows, :])


def _s5(u, uc, d, ops, batch):
    w_strip, m_in, m_out, dec = ops
    N, W = u.shape
    Nc = uc.shape[0]
    G = w_strip.shape[1]
    E = W // G
    gpb = LANES // E
    TE = SSM_CHUNK * E
    P2 = dec.shape[-1]
    assert P2 == LANES and W % LANES == 0
    r = batch * (_row_pitch(N // SSM_CHUNK // batch) + _row_pitch(Nc // SSM_CHUNK // batch))
    rp = -(-r // LANES) * LANES
    return pl.pallas_call(
        functools.partial(_ssm_kernel, batch=batch, e=E),
        grid=(W // LANES,),
        in_specs=[pl.BlockSpec((N, LANES), lambda j: (0, j)),
                  pl.BlockSpec((Nc, LANES), lambda j: (0, j)),
                  pl.BlockSpec((1, LANES), lambda j: (0, j)),
                  pl.BlockSpec((2, gpb, E, TE), lambda j: (0, j, 0, 0)),
                  pl.BlockSpec((2, gpb, TE, 2 * P2), lambda j: (0, j, 0, 0)),
                  pl.BlockSpec((2, gpb, TE, P2), lambda j: (0, j, 0, 0)),
                  pl.BlockSpec((2, gpb, 2, P2), lambda j: (0, j, 0, 0))],
        out_specs=pl.BlockSpec((N, LANES), lambda j: (0, j)),
        out_shape=jax.ShapeDtypeStruct((N, W), F32),
        scratch_shapes=[pltpu.VMEM((gpb, rp, TE), BF16),
                        pltpu.VMEM((2, gpb, rp, P2), F32),
                        pltpu.VMEM((2, gpb, rp, P2), F32),
                        pltpu.VMEM((2, gpb, rp, P2), F32)],
        compiler_params=_params("arbitrary"),
        name="s5",
    )(u, uc, d, w_strip, m_in, m_out, dec)


def _mixer_out_kernel(attn_ref, y_ref, ga_ref, gs_ref, x_ref, g2_ref, wglu_ref, bglu_ref,
                      wa_ref, ws_ref, wo_ref, o_ref, y2_ref, m_ref, *, sub):
    j = pl.program_id(1)
    tm, tn = o_ref.shape
    nd = m_ref.shape[1] // tn
    half = tn // 2

    @pl.when(j == 0)
    def _():
        for r in range(0, tm, sub):
            y = jax.nn.gelu(y_ref[r:r + sub, :])
            z = jnp.dot(y.astype(BF16), wglu_ref[...], preferred_element_type=F32) + bglu_ref[...]
            y2_ref[r:r + sub, :] = (y * jax.nn.sigmoid(z)).astype(BF16)

    @pl.when(j < nd)
    def _():
        for c in (0, half):
            pa = jnp.dot(attn_ref[...], wa_ref[:, c:c + half], preferred_element_type=F32)
            ps = jnp.dot(y2_ref[...], ws_ref[:, c:c + half], preferred_element_type=F32)
            merged = (ga_ref[:, c:c + half].astype(F32) * pa + gs_ref[:, c:c + half].astype(F32) * ps)
            m_ref[:, pl.ds(pl.multiple_of(j * tn + c, half), half)] = merged.astype(BF16)

    @pl.when(j >= nd)
    def _():
        for c in (0, half):
            acc = jnp.dot(m_ref[...], wo_ref[:, c:c + half], preferred_element_type=F32)
            o_ref[:, c:c + half] = x_ref[:, c:c + half] + g2_ref[:, c:c + half] * acc


def _mixer_out(attn, y_ssm, gate, x, g2, w_glu, b_glu, wa, ws, wo, rows_per_mod):
    N, D = x.shape
    WA, WS = attn.shape[1], y_ssm.shape[1]
    tm = _tile(rows_per_mod, 1024)
    tn = _tile(D, 512, 2 * LANES)
    nd = D // tn
    tpb = rows_per_mod // tm
    lo = lambda j: jnp.minimum(j, nd - 1)
    hi = lambda j: jnp.maximum(j - nd, 0)
    return pl.pallas_call(
        functools.partial(_mixer_out_kernel, sub=_tile(tm, 512)),
        grid=(N // tm, 2 * nd),
        in_specs=[pl.BlockSpec((tm, WA), lambda i, j: (i, 0)),
                  pl.BlockSpec((tm, WS), lambda i, j: (i, 0)),
                  pl.BlockSpec((tm, tn), lambda i, j: (i, lo(j))),
                  pl.BlockSpec((tm, tn), lambda i, j: (i, nd + lo(j))),
                  pl.BlockSpec((tm, tn), lambda i, j: (i, hi(j))),
                  pl.BlockSpec((None, 1, tn), lambda i, j: (i // tpb, 0, hi(j))),
                  pl.BlockSpec((WS, WS), lambda i, j: (0, 0)),
                  pl.BlockSpec((1, WS), lambda i, j: (0, 0)),
                  pl.BlockSpec((WA, tn), lambda i, j: (0, lo(j))),
                  pl.BlockSpec((WS, tn), lambda i, j: (0, lo(j))),
                  pl.BlockSpec((D, tn), lambda i, j: (0, hi(j)))],
        out_specs=pl.BlockSpec((tm, tn), lambda i, j: (i, hi(j))),
        out_shape=jax.ShapeDtypeStruct((N, D), F32),
        scratch_shapes=[pltpu.VMEM((tm, WS), BF16), pltpu.VMEM((tm, D), BF16)],
        compiler_params=_params("arbitrary", "arbitrary"),
        name="mixer_out",
    )(attn, y_ssm, gate, gate, x, g2, w_glu, b_glu, wa, ws, wo)


def kernel(x, c, ctx, c_ctx, w_mod, b_mod, norm_g, w_ffn1_gate, w_ffn1_up, w_ffn1_down, w_in, q_norm_g, k_norm_g, ssm_a_re, ssm_a_im, ssm_log_dt, ssm_b_re, ssm_b_im, ssm_c_re, ssm_c_im, ssm_d, w_glu, b_glu, w_br_attn, w_br_ssm, w_out, w_ffn2_gate, w_ffn2_up, w_ffn2_down):
    B, L, D = x.shape
    Lc = ctx.shape[1]
    assert w_mod.shape[0] == 1, "only the single (last) layer configuration is implemented"
    hd = q_norm_g.shape[1]
    ssm_w = w_glu.shape[1]
    attn_w = w_br_attn.shape[1]
    kv_w = attn_w // Q_PER_KV
    widths = (kv_w, ssm_w, attn_w, 2 * D)
    assert L % GRID_W == 0 and L % SSM_CHUNK == 0 and Lc % SSM_CHUNK == 0 and B <= 7
    N, Nc = B * L, B * Lc
    l = 0

    cc = jnp.zeros((8, D), F32).at[:B].set(c).at[B].set(c_ctx)
    mod = _modulation(cc, w_mod[l], b_mod[l][None, :])

    def mvec(k, lo, hi):
        return mod[lo:hi, k * D:(k + 1) * D][:, None, :]

    xm = [mvec(k, 0, B) for k in range(N_MOD)]
    cm = [mvec(k, B, B + 1) for k in range(5)]
    ng = norm_g[l][:, None, :]

    x2 = x.reshape(N, D)
    c2 = ctx.reshape(Nc, D)
    w1 = [w.astype(BF16) for w in (w_ffn1_gate[l], w_ffn1_up[l], w_ffn1_down[l])]
    w2 = [w.astype(BF16) for w in (w_ffn2_gate[l], w_ffn2_up[l], w_ffn2_down[l])]
    w_in_b = w_in[l].astype(BF16)

    x2 = _ffn(x2, ng[0], xm[0], xm[1], xm[2], *w1, L)
    c2 = _ffn(c2, ng[0], cm[0], cm[1], cm[2], *w1, Nc)

    cos, sin = _rope_tables(L, hd)
    kg = k_norm_g[l][None, :]
    qg = q_norm_g[l][None, :] * (hd ** -0.5 * math.log2(math.e))
    k, v, u, q, gate = _proj(x2, ng[1], xm[3], xm[4], w_in_b, cos, sin, kg, qg, L, widths, True)
    kc, vc, uc = _proj(c2, ng[1], cm[3], cm[4], w_in_b, cos, sin, kg, qg, Nc, widths, False)

    attn = _attention(q, kc, k, vc, v, B, hd)
    ops = _ssm_operators(ssm_a_re[l], ssm_a_im[l], ssm_log_dt[l], ssm_b_re[l], ssm_b_im[l],
                         ssm_c_re[l], ssm_c_im[l])
    y_ssm = _s5(u, uc, ssm_d[l][None, :], ops, B)

    x2 = _mixer_out(attn, y_ssm, gate, x2, xm[5], w_glu[l].astype(BF16), b_glu[l][None, :],
                    w_br_attn[l].astype(BF16), w_br_ssm[l].astype(BF16), w_out[l].astype(BF16), L)

    x2 = _ffn(x2, ng[2], xm[6], xm[7], xm[8], *w2, L)
    return x2.reshape(B, L, D)
```

```python
import functools
import math

import jax
import jax.numpy as jnp
from jax import lax
from jax.experimental import pallas as pl
from jax.experimental.pallas import tpu as pltpu

F32 = jnp.float32
BF16 = jnp.bfloat16

GRID_W = 64
ROPE_THETA = 10000.0
NORM_EPS = 1e-6
N_MOD = 9
Q_PER_KV = 4
SSM_CHUNK = 16
LANES = 128
VMEM_LIMIT_BYTES = 58 * 1024 * 1024


def _tile(n, pref, mult=8):
    if n <= pref:
        return n
    for t in range(pref, 0, -1):
        if n % t == 0 and t % mult == 0:
            return t
    return n


def _params(*sem):
    return pltpu.CompilerParams(dimension_semantics=sem, vmem_limit_bytes=VMEM_LIMIT_BYTES)


def _norm_mod(x, g, sh, sc):
    ms = jnp.mean(x * x, axis=-1, keepdims=True)
    h = (x * lax.rsqrt(ms + NORM_EPS)) * g
    return (h * (1.0 + sc) + sh).astype(BF16)


def _mod_kernel(c_ref, w_ref, b_ref, o_ref):
    c = c_ref[...]
    s = c * jax.nn.sigmoid(c)
    o_ref[...] = jnp.dot(s.astype(BF16), w_ref[...].astype(BF16),
                         preferred_element_type=F32) + b_ref[...]


def _modulation(cc, w_mod, b_mod):
    D, M = w_mod.shape
    tn = _tile(M, 1024, LANES)
    return pl.pallas_call(
        _mod_kernel,
        grid=(M // tn,),
        in_specs=[pl.BlockSpec((cc.shape[0], D), lambda j: (0, 0)),
                  pl.BlockSpec((D, tn), lambda j: (0, j)),
                  pl.BlockSpec((1, tn), lambda j: (0, j))],
        out_specs=pl.BlockSpec((cc.shape[0], tn), lambda j: (0, j)),
        out_shape=jax.ShapeDtypeStruct((cc.shape[0], M), F32),
        compiler_params=_params("arbitrary"),
        name="modulation",
    )(cc, w_mod, b_mod)


def _ffn_act_kernel(x_ref, ng_ref, sh_ref, sc_ref, wg_ref, wu_ref, *rest, sub):
    if len(rest) == 4:
        wd_ref, a_ref, wdb_ref, h_ref = rest

        @pl.when(pl.program_id(0) == 0)
        def _():
            wdb_ref[...] = wd_ref[...].astype(BF16)
    else:
        a_ref, h_ref = rest

    @pl.when(pl.program_id(1) == 0)
    def _():
        for r in range(0, x_ref.shape[0], sub):
            h_ref[r:r + sub, :] = _norm_mod(x_ref[r:r + sub, :], ng_ref[...], sh_ref[...], sc_ref[...])

    h = h_ref[...]
    half = a_ref.shape[1] // 2
    for c in (0, half):
        g = jnp.dot(h, wg_ref[:, c:c + half].astype(BF16), preferred_element_type=F32)
        u = jnp.dot(h, wu_ref[:, c:c + half].astype(BF16), preferred_element_type=F32)
        a_ref[:, c:c + half] = ((g * jax.nn.sigmoid(g)) * u).astype(a_ref.dtype)


def _ffn_down_kernel(a_ref, x_ref, gate_ref, wd_ref, o_ref):
    half = o_ref.shape[1] // 2
    for c in (0, half):
        acc = jnp.dot(a_ref[...], wd_ref[:, c:c + half], preferred_element_type=F32)
        o_ref[:, c:c + half] = x_ref[:, c:c + half] + (0.5 * gate_ref[:, c:c + half]) * acc


def _ffn(x, ng, sh, sc, gate, wg, wu, wd, rows_per_mod):
    N, D = x.shape
    F = wg.shape[1]
    tm = _tile(rows_per_mod, 1024)
    tf = _tile(F, 512, 2 * LANES)
    tn = _tile(D, 512, 2 * LANES)
    tpb = rows_per_mod // tm
    nj = F // tf
    vec = pl.BlockSpec((None, 1, D), lambda i, j: (i // tpb, 0, 0))
    in_specs = [pl.BlockSpec((tm, D), lambda i, j: (i, 0)),
                pl.BlockSpec((1, D), lambda i, j: (0, 0)), vec, vec,
                pl.BlockSpec((D, tf), lambda i, j: (0, j)),
                pl.BlockSpec((D, tf), lambda i, j: (0, j))]
    out_specs = [pl.BlockSpec((tm, tf), lambda i, j: (i, j))]
    out_shape = [jax.ShapeDtypeStruct((N, F), BF16)]
    operands = [x, ng, sh, sc, wg, wu]
    if wd.dtype != BF16:
        wd_rows = pl.BlockSpec((tf, D), lambda i, j: (jnp.where(i == 0, j, nj - 1), 0))
        in_specs.append(wd_rows)
        out_specs.append(wd_rows)
        out_shape.append(jax.ShapeDtypeStruct((F, D), BF16))
        operands.append(wd)
    res = pl.pallas_call(
        functools.partial(_ffn_act_kernel, sub=_tile(tm, 512)),
        grid=(N // tm, nj),
        in_specs=in_specs,
        out_specs=out_specs,
        out_shape=out_shape,
        scratch_shapes=[pltpu.VMEM((tm, D), BF16)],
        compiler_params=_params("arbitrary", "arbitrary"),
        name="ffn_act",
    )(*operands)
    a = res[0]
    if wd.dtype != BF16:
        wd = res[1]
    out = pl.pallas_call(
        _ffn_down_kernel,
        grid=(N // tm, D // tn),
        in_specs=[pl.BlockSpec((tm, F), lambda i, j: (i, 0)),
                  pl.BlockSpec((tm, tn), lambda i, j: (i, j)),
                  pl.BlockSpec((None, 1, tn), lambda i, j: (i // tpb, 0, j)),
                  pl.BlockSpec((F, tn), lambda i, j: (0, j))],
        out_specs=pl.BlockSpec((tm, tn), lambda i, j: (i, j)),
        out_shape=jax.ShapeDtypeStruct((N, D), F32),
        compiler_params=_params("arbitrary", "arbitrary"),
        name="ffn_down",
    )(a, x, gate, wd)
    return out, wd


def _head_norm(acc, g, cos, sin, head_dim):
    quarter = head_dim // 4
    out = []
    for hh in range(acc.shape[1] // head_dim):
        xh = acc[:, hh * head_dim:(hh + 1) * head_dim]
        ms = jnp.mean(xh * xh, axis=-1, keepdims=True)
        xh = (xh * lax.rsqrt(ms + NORM_EPS)) * g
        if cos is not None:
            lane = lax.broadcasted_iota(jnp.int32, xh.shape, 1)
            partner = jnp.where((lane % (2 * quarter)) < quarter,
                                pltpu.roll(xh, head_dim - quarter, 1),
                                pltpu.roll(xh, quarter, 1))
            xh = xh * cos + partner * sin
        out.append(xh.astype(BF16))
    return out[0] if len(out) == 1 else jnp.concatenate(out, axis=1)


def _proj_kernel(x_ref, ng_ref, sh_ref, sc_ref, w_ref, cos_ref, sin_ref, kg_ref,
                 k_ref, v_ref, u_ref, *rest, bounds, rope, head_dim, sub):
    h_ref = rest[-1]
    j = pl.program_id(1)
    tm = x_ref.shape[0]
    jv, ju, jq, jg = bounds

    @pl.when(j == 0)
    def _():
        for r in range(0, tm, sub):
            h_ref[r:r + sub, :] = _norm_mod(x_ref[r:r + sub, :], ng_ref[...], sh_ref[...], sc_ref[...])

    acc = jnp.dot(h_ref[...], w_ref[...], preferred_element_type=F32)
    cos = cos_ref[...] if rope else None
    sin = sin_ref[...] if rope else None

    @pl.when(j < jv)
    def _():
        k_ref[...] = _head_norm(acc, kg_ref[...], cos, sin, head_dim)

    @pl.when((j >= jv) & (j < ju))
    def _():
        v_ref[...] = acc.astype(BF16)

    @pl.when((j >= ju) & (j < jq))
    def _():
        u_ref[...] = acc

    if len(rest) == 3:
        q_ref, gate_ref = rest[0], rest[1]

        @pl.when((j >= jq) & (j < jg))
        def _():
            q_ref[...] = acc

        @pl.when(j >= jg)
        def _():
            gate_ref[...] = acc.astype(BF16)


def _proj(x, ng, sh, sc, w, cos, sin, kg, rows_per_seq, widths, latent):
    N, D = x.shape
    kv_w, ssm_w, attn_w, gate_w = widths
    head_dim = kg.shape[1]
    tn = kv_w
    assert tn % LANES == 0 and ssm_w % tn == 0 and attn_w % tn == 0 and gate_w % tn == 0
    jv, ju = 1, 2
    jq = ju + ssm_w // tn
    jg = jq + attn_w // tn
    nj = jg + gate_w // tn if latent else jq
    tm = _tile(rows_per_seq, 1024)
    tpb = rows_per_seq // tm
    vec = pl.BlockSpec((None, 1, D), lambda i, j: (i // tpb, 0, 0))
    tab = pl.BlockSpec((tm, head_dim), lambda i, j: (i % tpb, 0))
    hvec = pl.BlockSpec((1, head_dim), lambda i, j: (0, 0))

    def out_spec(j0, nblk):
        return pl.BlockSpec((tm, tn), lambda i, j: (i, jnp.clip(j - j0, 0, nblk - 1)))

    out_specs = [out_spec(0, 1), out_spec(jv, 1), out_spec(ju, jq - ju)]
    out_shape = [jax.ShapeDtypeStruct((N, kv_w), BF16), jax.ShapeDtypeStruct((N, kv_w), BF16),
                 jax.ShapeDtypeStruct((N, ssm_w), F32)]
    if latent:
        out_specs += [out_spec(jq, jg - jq), out_spec(jg, nj - jg)]
        out_shape += [jax.ShapeDtypeStruct((N, attn_w), F32), jax.ShapeDtypeStruct((N, gate_w), BF16)]
    return pl.pallas_call(
        functools.partial(_proj_kernel, bounds=(jv, ju, jq, jg), rope=latent, head_dim=head_dim,
                          sub=_tile(tm, 512)),
        grid=(N // tm, nj),
        in_specs=[pl.BlockSpec((tm, D), lambda i, j: (i, 0)),
                  pl.BlockSpec((1, D), lambda i, j: (0, 0)), vec, vec,
                  pl.BlockSpec((D, tn), lambda i, j: (0, j)),
                  tab, tab, hvec],
        out_specs=out_specs,
        out_shape=out_shape,
        scratch_shapes=[pltpu.VMEM((tm, D), BF16)],
        compiler_params=_params("arbitrary", "arbitrary"),
        name="proj_latent" if latent else "proj_context",
    )(x, ng, sh, sc, w, cos, sin, kg)


def _rope_tables(L, head_dim):
    t = jnp.arange(L)
    row = (t // GRID_W).astype(F32)
    col = (t % GRID_W).astype(F32)
    half = head_dim // 4
    inv_freq = ROPE_THETA ** (-jnp.arange(half, dtype=F32) / half)
    ar = row[:, None] * inv_freq
    ac = col[:, None] * inv_freq
    cos = jnp.concatenate([jnp.cos(ar), jnp.cos(ar), jnp.cos(ac), jnp.cos(ac)], axis=-1)
    sin = jnp.concatenate([-jnp.sin(ar), jnp.sin(ar), -jnp.sin(ac), jnp.sin(ac)], axis=-1)
    return cos, sin


def _attn_kernel(q_ref, cos_ref, sin_ref, qg_ref, kc_ref, k_ref, vc_ref, v_ref, o_ref,
                 kall_ref, vext_ref, *, head_dim):
    lc = kc_ref.shape[0]

    @pl.when(pl.program_id(2) == 0)
    def _():
        kall_ref[0:lc, :] = kc_ref[...]
        kall_ref[lc:, :] = k_ref[...]
        vext_ref[0:lc, 0:head_dim] = vc_ref[...]
        vext_ref[lc:, 0:head_dim] = v_ref[...]
        vext_ref[:, head_dim:] = jnp.ones((vext_ref.shape[0], head_dim), BF16)

    k = kall_ref[...]
    vext = vext_ref[...]
    for r in range(q_ref.shape[1] // head_dim):
        sl = slice(r * head_dim, (r + 1) * head_dim)
        q = _head_norm(q_ref[:, sl], qg_ref[...], cos_ref[...], sin_ref[...], head_dim)
        s = lax.dot_general(q, k, (((1,), (1,)), ((), ())), preferred_element_type=F32)
        m = jnp.max(s, axis=-1, keepdims=True)
        p = jnp.exp2(s - m).astype(BF16)
        oe = jnp.dot(p, vext, preferred_element_type=F32)
        o_ref[:, sl] = (oe[:, :head_dim] / oe[:, head_dim:]).astype(o_ref.dtype)


def _attention(q, cos, sin, qg, kc, k, vc, v, B, head_dim):
    N, W = q.shape
    L, Lc = N // B, kc.shape[0] // B
    kvh = k.shape[1] // head_dim
    gw = W // kvh
    tq = _tile(L, 512)
    tpb = L // tq
    kv_lat = pl.BlockSpec((L, head_dim), lambda b, h, i: (b, h))
    kv_ctx = pl.BlockSpec((Lc, head_dim), lambda b, h, i: (b, h))
    tab = pl.BlockSpec((tq, head_dim), lambda b, h, i: (i, 0))
    return pl.pallas_call(
        functools.partial(_attn_kernel, head_dim=head_dim),
        grid=(B, kvh, tpb),
        in_specs=[pl.BlockSpec((tq, gw), lambda b, h, i: (b * tpb + i, h)),
                  tab, tab, pl.BlockSpec((1, head_dim), lambda b, h, i: (0, 0)),
                  kv_ctx, kv_lat, kv_ctx, kv_lat],
        out_specs=pl.BlockSpec((tq, gw), lambda b, h, i: (b * tpb + i, h)),
        out_shape=jax.ShapeDtypeStruct((N, W), BF16),
        scratch_shapes=[pltpu.VMEM((L + Lc, head_dim), BF16),
                        pltpu.VMEM((L + Lc, 2 * head_dim), BF16)],
        compiler_params=_params("arbitrary", "arbitrary", "arbitrary"),
        name="attention",
    )(q, cos, sin, qg, kc, k, vc, v)


def _ssm_operators(a_re, a_im, log_dt, b_re, b_im, c_re, c_im):
    T = SSM_CHUNK
    hp = lax.Precision.HIGHEST
    a_re, a_im = a_re.astype(F32), a_im.astype(F32)
    dt = jnp.exp(log_dt.astype(F32))[..., None]
    mag = jnp.exp(a_re * dt)
    lr = mag * jnp.cos(a_im * dt)
    li = mag * jnp.sin(a_im * dt)
    den = a_re * a_re + a_im * a_im
    cr = ((lr - 1.0) * a_re + li * a_im) / den
    ci = (li * a_re - (lr - 1.0) * a_im) / den
    pr, pi = [jnp.ones_like(lr)], [jnp.zeros_like(lr)]
    for _ in range(T):
        pr.append(pr[-1] * lr - pi[-1] * li)
        pi.append(pr[-2] * li + pi[-1] * lr)
    pw_r, pw_i = jnp.stack(pr), jnp.stack(pi)
    _, G, P, E = b_re.shape
    bt_r = jnp.swapaxes(b_re.astype(F32), 2, 3)
    bt_i = jnp.swapaxes(b_im.astype(F32), 2, 3)
    bb_r = cr[:, :, None, :] * bt_r - ci[:, :, None, :] * bt_i
    bb_i = cr[:, :, None, :] * bt_i + ci[:, :, None, :] * bt_r
    c_r, c_i = c_re.astype(F32), c_im.astype(F32)

    def cmul(xr, xi, wr, wi):
        wr = jnp.moveaxis(wr, 0, 2)[:, :, :, None, :]
        wi = jnp.moveaxis(wi, 0, 2)[:, :, :, None, :]
        xr, xi = xr[:, :, None], xi[:, :, None]
        return ((xr * wr - xi * wi).reshape(2, G, T * E, P),
                (xr * wi + xi * wr).reshape(2, G, T * E, P))

    steps = jnp.arange(T)

    def strip(d, lags):
        rr, ri = cmul(c_r, c_i, pw_r[lags], pw_i[lags])
        return (jnp.einsum("gcp,gjp->gcj", bb_r[d], rr[d], precision=hp)
                - jnp.einsum("gcp,gjp->gcj", bb_i[d], ri[d], precision=hp))

    w_strip = jnp.stack([strip(0, steps), strip(1, T - 1 - steps)])
    in_f = cmul(bb_r, bb_i, pw_r[T - 1 - steps], pw_i[T - 1 - steps])
    in_b = cmul(bb_r, bb_i, pw_r[steps], pw_i[steps])
    m_in = jnp.stack([jnp.concatenate([in_f[0][0], in_f[1][0], in_f[1][0], in_f[0][0]], axis=-1),
                      jnp.concatenate([in_b[0][1], in_b[1][1], in_b[1][1], in_b[0][1]], axis=-1)])
    out_f = cmul(c_r, c_i, pw_r[steps + 1], pw_i[steps + 1])
    out_b = cmul(c_r, c_i, pw_r[T - steps], pw_i[T - steps])
    m_out = jnp.stack([jnp.concatenate([out_f[0][0], -out_f[1][0]], axis=-1),
                       jnp.concatenate([out_b[0][1], -out_b[1][1]], axis=-1)])
    ar, ai = pw_r[T], pw_i[T]
    dec = jnp.stack([jnp.concatenate([ar, ar], -1), jnp.concatenate([-ai, ai], -1)], axis=2)
    return w_strip, m_in.astype(BF16), m_out.astype(BF16), dec


def _intra_operator(w_ref, g, e):
    T = SSM_CHUNK
    wf, wb = w_ref[0, g], w_ref[1, g]
    t_of_lane = lax.broadcasted_iota(jnp.int32, wf.shape, 1) // e
    rows = []
    for s in range(T):
        f = wf if s == 0 else jnp.where(t_of_lane >= s, pltpu.roll(wf, s * e, 1), 0.0)
        b = wb if s == T - 1 else jnp.where(t_of_lane <= s, pltpu.roll(wb, (s + 1) * e, 1), 0.0)
        rows.append(f + b)
    return jnp.concatenate(rows, axis=0).astype(BF16)


def _row_pitch(n):
    p = -(-n // 8)
    return 8 * (p if p % 2 else p + 1)


def _ssm_kernel(u_ref, uc_ref, d_ref, w_ref, min_ref, mo_ref, dec_ref, y_ref,
                z_ref, s1_ref, s2_ref, hp_ref, *, batch, e):
    T = SSM_CHUNK
    gpb = w_ref.shape[1]
    rp = z_ref.shape[1]
    n_lat, n_ctx = u_ref.shape[0] // T, uc_ref.shape[0] // T
    ncl, ncc = n_lat // batch, n_ctx // batch
    pl_, pc_ = _row_pitch(ncl), _row_pitch(ncc)
    lat0 = batch * pc_
    r = lat0 + batch * pl_
    lanes = u_ref.shape[1]
    pw = hp_ref.shape[3]

    def padded(rows, n, pitch):
        if pitch == n:
            return [rows]
        out = []
        for b in range(batch):
            out += [rows[b * n:(b + 1) * n, :], jnp.zeros((pitch - n, lanes), F32)]
        return out

    xt = []
    for s in range(T):
        parts = (padded(uc_ref[pl.ds(s, n_ctx, stride=T), :], ncc, pc_)
                 + padded(u_ref[pl.ds(s, n_lat, stride=T), :], ncl, pl_))
        if rp > r:
            parts.append(jnp.zeros((rp - r, lanes), F32))
        xt.append(jnp.concatenate(parts, axis=0).T)
    for g in range(gpb):
        zt = jnp.concatenate([xt[s][g * e:(g + 1) * e, :] for s in range(T)], axis=0)
        z_ref[g] = zt.T.astype(BF16)
    for d in range(2):
        for g in range(gpb):
            st = jnp.dot(z_ref[g], min_ref[d, g], preferred_element_type=F32)
            s1_ref[d, g] = st[:, :pw]
            s2_ref[d, g] = st[:, pw:]

    def sweep(base, count, stride, carry):
        def body(it, carry):
            out = []
            for d in range(2):
                n = it if d == 0 else count - 1 - it
                rows = pl.ds(base + n, batch, stride=stride)
                for g in range(gpb):
                    h1, h2 = carry[2 * (d * gpb + g)], carry[2 * (d * gpb + g) + 1]
                    hp_ref[d, g, rows, :] = h1
                    a1 = dec_ref[d, g, 0:1, :]
                    a2 = dec_ref[d, g, 1:2, :]
                    out.append(a1 * h1 + a2 * h2 + s1_ref[d, g, rows, :])
                    out.append(a1 * h2 - a2 * h1 + s2_ref[d, g, rows, :])
            return tuple(out)
        return lax.fori_loop(0, count, body, carry)

    for d in range(2):
        for g in range(gpb):
            for b in range(batch):
                if pc_ > ncc:
                    hp_ref[d, g, b * pc_ + ncc:(b + 1) * pc_, :] = jnp.zeros((pc_ - ncc, pw), F32)
                if pl_ > ncl:
                    hp_ref[d, g, lat0 + b * pl_ + ncl:lat0 + (b + 1) * pl_, :] = jnp.zeros((pl_ - ncl, pw), F32)
            if rp > r:
                hp_ref[d, g, r:rp, :] = jnp.zeros((rp - r, pw), F32)
    zero = jnp.zeros((batch, pw), F32)
    carry = sweep(0, ncc, pc_, tuple([zero] * (4 * gpb)))
    sweep(lat0, ncl, pl_, carry)
    yt = []
    for g in range(gpb):
        y = jnp.dot(z_ref[g], _intra_operator(w_ref, g, e), preferred_element_type=F32)
        for d in range(2):
            y += lax.dot_general(hp_ref[d, g].astype(BF16), mo_ref[d, g],
                                 (((1,), (1,)), ((), ())), preferred_element_type=F32)
        yt.append(y.T)
    dvec = d_ref[...]
    for t in range(T):
        blk = jnp.concatenate([yt[g][t * e:(t + 1) * e, :] for g in range(gpb)], axis=0).T
        for b in range(batch):
            rows = pl.ds(b * ncl * T + t, ncl, stride=T)
            y_ref[rows, :] = (blk[lat0 + b * pl_:lat0 + b * pl_ + ncl, :] + dvec * u_ref[rows, :])


def _s5(u, uc, d, ops, batch):
    w_strip, m_in, m_out, dec = ops
    N, W = u.shape
    Nc = uc.shape[0]
    G = w_strip.shape[1]
    E = W // G
    gpb = LANES // E
    TE = SSM_CHUNK * E
    P2 = dec.shape[-1]
    assert P2 == LANES and W % LANES == 0
    r = batch * (_row_pitch(N // SSM_CHUNK // batch) + _row_pitch(Nc // SSM_CHUNK // batch))
    rp = -(-r // LANES) * LANES
    return pl.pallas_call(
        functools.partial(_ssm_kernel, batch=batch, e=E),
        grid=(W // LANES,),
        in_specs=[pl.BlockSpec((N, LANES), lambda j: (0, j)),
                  pl.BlockSpec((Nc, LANES), lambda j: (0, j)),
                  pl.BlockSpec((1, LANES), lambda j: (0, j)),
                  pl.BlockSpec((2, gpb, E, TE), lambda j: (0, j, 0, 0)),
                  pl.BlockSpec((2, gpb, TE, 2 * P2), lambda j: (0, j, 0, 0)),
                  pl.BlockSpec((2, gpb, TE, P2), lambda j: (0, j, 0, 0)),
                  pl.BlockSpec((2, gpb, 2, P2), lambda j: (0, j, 0, 0))],
        out_specs=pl.BlockSpec((N, LANES), lambda j: (0, j)),
        out_shape=jax.ShapeDtypeStruct((N, W), F32),
        scratch_shapes=[pltpu.VMEM((gpb, rp, TE), BF16),
                        pltpu.VMEM((2, gpb, rp, P2), F32),
                        pltpu.VMEM((2, gpb, rp, P2), F32),
                        pltpu.VMEM((2, gpb, rp, P2), F32)],
        compiler_params=_params("arbitrary"),
        name="s5",
    )(u, uc, d, w_strip, m_in, m_out, dec)


def _mixer_out_kernel(attn_ref, y_ref, ga_ref, gs_ref, x_ref, g2_ref, wglu_ref, bglu_ref,
                      wa_ref, ws_ref, wo_ref, o_ref, y2_ref, m_ref, *, sub):
    j = pl.program_id(1)
    tm, tn = o_ref.shape
    nd = m_ref.shape[1] // tn
    half = tn // 2

    @pl.when(j == 0)
    def _():
        for r in range(0, tm, sub):
            y = jax.nn.gelu(y_ref[r:r + sub, :])
            z = jnp.dot(y.astype(BF16), wglu_ref[...], preferred_element_type=F32) + bglu_ref[...]
            y2_ref[r:r + sub, :] = (y * jax.nn.sigmoid(z)).astype(BF16)

    @pl.when(j < nd)
    def _():
        for c in (0, half):
            pa = jnp.dot(attn_ref[...], wa_ref[:, c:c + half], preferred_element_type=F32)
            ps = jnp.dot(y2_ref[...], ws_ref[:, c:c + half], preferred_element_type=F32)
            merged = (jax.nn.sigmoid(ga_ref[:, c:c + half].astype(F32)) * pa
                      + jax.nn.sigmoid(gs_ref[:, c:c + half].astype(F32)) * ps)
            m_ref[:, pl.ds(pl.multiple_of(j * tn + c, half), half)] = merged.astype(BF16)

    @pl.when(j >= nd)
    def _():
        for c in (0, half):
            acc = jnp.dot(m_ref[...], wo_ref[:, c:c + half], preferred_element_type=F32)
            o_ref[:, c:c + half] = x_ref[:, c:c + half] + g2_ref[:, c:c + half] * acc


def _mixer_out(attn, y_ssm, gate, x, g2, w_glu, b_glu, wa, ws, wo, rows_per_mod):
    N, D = x.shape
    WA, WS = attn.shape[1], y_ssm.shape[1]
    tm = _tile(rows_per_mod, 1024)
    tn = _tile(D, 512, 2 * LANES)
    nd = D // tn
    tpb = rows_per_mod // tm
    lo = lambda j: jnp.minimum(j, nd - 1)
    hi = lambda j: jnp.maximum(j - nd, 0)
    return pl.pallas_call(
        functools.partial(_mixer_out_kernel, sub=_tile(tm, 512)),
        grid=(N // tm, 2 * nd),
        in_specs=[pl.BlockSpec((tm, WA), lambda i, j: (i, 0)),
                  pl.BlockSpec((tm, WS), lambda i, j: (i, 0)),
                  pl.BlockSpec((tm, tn), lambda i, j: (i, lo(j))),
                  pl.BlockSpec((tm, tn), lambda i, j: (i, nd + lo(j))),
                  pl.BlockSpec((tm, tn), lambda i, j: (i, hi(j))),
                  pl.BlockSpec((None, 1, tn), lambda i, j: (i // tpb, 0, hi(j))),
                  pl.BlockSpec((WS, WS), lambda i, j: (0, 0)),
                  pl.BlockSpec((1, WS), lambda i, j: (0, 0)),
                  pl.BlockSpec((WA, tn), lambda i, j: (0, lo(j))),
                  pl.BlockSpec((WS, tn), lambda i, j: (0, lo(j))),
                  pl.BlockSpec((D, tn), lambda i, j: (0, hi(j)))],
        out_specs=pl.BlockSpec((tm, tn), lambda i, j: (i, hi(j))),
        out_shape=jax.ShapeDtypeStruct((N, D), F32),
        scratch_shapes=[pltpu.VMEM((tm, WS), BF16), pltpu.VMEM((tm, D), BF16)],
        compiler_params=_params("arbitrary", "arbitrary"),
        name="mixer_out",
    )(attn, y_ssm, gate, gate, x, g2, w_glu, b_glu, wa, ws, wo)


def kernel(x, c, ctx, c_ctx, w_mod, b_mod, norm_g, w_ffn1_gate, w_ffn1_up, w_ffn1_down, w_in, q_norm_g, k_norm_g, ssm_a_re, ssm_a_im, ssm_log_dt, ssm_b_re, ssm_b_im, ssm_c_re, ssm_c_im, ssm_d, w_glu, b_glu, w_br_attn, w_br_ssm, w_out, w_ffn2_gate, w_ffn2_up, w_ffn2_down):
    B, L, D = x.shape
    Lc = ctx.shape[1]
    assert w_mod.shape[0] == 1, "only the single (last) layer configuration is implemented"
    hd = q_norm_g.shape[1]
    ssm_w = w_glu.shape[1]
    attn_w = w_br_attn.shape[1]
    kv_w = attn_w // Q_PER_KV
    widths = (kv_w, ssm_w, attn_w, 2 * D)
    assert L % GRID_W == 0 and L % SSM_CHUNK == 0 and Lc % SSM_CHUNK == 0 and B <= 7
    N, Nc = B * L, B * Lc
    l = 0

    cc = jnp.zeros((8, D), F32).at[:B].set(c).at[B].set(c_ctx)
    mod = _modulation(cc, w_mod[l], b_mod[l][None, :])

    def mvec(k, lo, hi):
        return mod[lo:hi, k * D:(k + 1) * D][:, None, :]

    xm = [mvec(k, 0, B) for k in range(N_MOD)]
    cm = [mvec(k, B, B + 1) for k in range(5)]
    ng = norm_g[l][:, None, :]

    x2 = x.reshape(N, D)
    c2 = ctx.reshape(Nc, D)
    w_in_b = w_in[l].astype(BF16)

    x2, wd1 = _ffn(x2, ng[0], xm[0], xm[1], xm[2], w_ffn1_gate[l], w_ffn1_up[l], w_ffn1_down[l], L)
    c2, _ = _ffn(c2, ng[0], cm[0], cm[1], cm[2], w_ffn1_gate[l], w_ffn1_up[l], wd1, Nc)

    cos, sin = _rope_tables(L, hd)
    kg = k_norm_g[l][None, :]
    qg = q_norm_g[l][None, :] * (hd ** -0.5 * math.log2(math.e))
    k, v, u, q, gate = _proj(x2, ng[1], xm[3], xm[4], w_in_b, cos, sin, kg, L, widths, True)
    kc, vc, uc = _proj(c2, ng[1], cm[3], cm[4], w_in_b, cos, sin, kg, Nc, widths, False)

    attn = _attention(q, cos, sin, qg, kc, k, vc, v, B, hd)
    ops = _ssm_operators(ssm_a_re[l], ssm_a_im[l], ssm_log_dt[l], ssm_b_re[l], ssm_b_im[l],
                         ssm_c_re[l], ssm_c_im[l])
    y_ssm = _s5(u, uc, ssm_d[l][None, :], ops, B)

    x2 = _mixer_out(attn, y_ssm, gate, x2, xm[5], w_glu[l].astype(BF16), b_glu[l][None, :],
                    w_br_attn[l].astype(BF16), w_br_ssm[l].astype(BF16), w_out[l].astype(BF16), L)

    x2, _ = _ffn(x2, ng[2], xm[6], xm[7], xm[8], w_ffn2_gate[l], w_ffn2_up[l], w_ffn2_down[l], L)
    return x2.reshape(B, L, D)
```

```python
import functools
import math

import jax
import jax.numpy as jnp
import numpy as np
from jax import lax
from jax.experimental import pallas as pl
from jax.experimental.pallas import tpu as pltpu

F32 = jnp.float32
BF16 = jnp.bfloat16

GRID_W = 64
ROPE_THETA = 10000.0
NORM_EPS = 1e-6
N_MOD = 9
Q_PER_KV = 4
SSM_CHUNK = 16
LANES = 128
VMEM_LIMIT_BYTES = 58 * 1024 * 1024


def _tile(n, pref, mult=8):
    if n <= pref:
        return n
    for t in range(pref, 0, -1):
        if n % t == 0 and t % mult == 0:
            return t
    return n


def _params(*sem):
    return pltpu.CompilerParams(dimension_semantics=sem, vmem_limit_bytes=VMEM_LIMIT_BYTES)


def _norm_mod(x, g, sh, sc):
    ms = jnp.mean(x * x, axis=-1, keepdims=True)
    h = (x * lax.rsqrt(ms + NORM_EPS)) * g
    return (h * (1.0 + sc) + sh).astype(BF16)


def _mod_kernel(c_ref, w_ref, b_ref, o_ref):
    c = c_ref[...]
    s = c * jax.nn.sigmoid(c)
    o_ref[...] = jnp.dot(s.astype(BF16), w_ref[...].astype(BF16),
                         preferred_element_type=F32) + b_ref[...]


def _modulation(cc, w_mod, b_mod):
    D, M = w_mod.shape
    tn = _tile(M, 1024, LANES)
    return pl.pallas_call(
        _mod_kernel,
        grid=(M // tn,),
        in_specs=[pl.BlockSpec((cc.shape[0], D), lambda j: (0, 0)),
                  pl.BlockSpec((D, tn), lambda j: (0, j)),
                  pl.BlockSpec((1, tn), lambda j: (0, j))],
        out_specs=pl.BlockSpec((cc.shape[0], tn), lambda j: (0, j)),
        out_shape=jax.ShapeDtypeStruct((cc.shape[0], M), F32),
        compiler_params=_params("arbitrary"),
        name="modulation",
    )(cc, w_mod, b_mod)


def _ffn_act_kernel(x_ref, ng_ref, sh_ref, sc_ref, wg_ref, wu_ref, *rest, sub):
    if len(rest) == 4:
        wd_ref, a_ref, wdb_ref, h_ref = rest

        @pl.when(pl.program_id(0) == 0)
        def _():
            wdb_ref[...] = wd_ref[...].astype(BF16)
    else:
        a_ref, h_ref = rest

    @pl.when(pl.program_id(1) == 0)
    def _():
        for r in range(0, x_ref.shape[0], sub):
            h_ref[r:r + sub, :] = _norm_mod(x_ref[r:r + sub, :], ng_ref[...], sh_ref[...], sc_ref[...])

    h = h_ref[...]
    half = a_ref.shape[1] // 2
    for c in (0, half):
        g = jnp.dot(h, wg_ref[:, c:c + half].astype(BF16), preferred_element_type=F32)
        u = jnp.dot(h, wu_ref[:, c:c + half].astype(BF16), preferred_element_type=F32)
        a_ref[:, c:c + half] = ((g * jax.nn.sigmoid(g)) * u).astype(a_ref.dtype)


def _ffn_down_kernel(a_ref, x_ref, gate_ref, wd_ref, o_ref):
    half = o_ref.shape[1] // 2
    for c in (0, half):
        acc = jnp.dot(a_ref[...], wd_ref[:, c:c + half], preferred_element_type=F32)
        o_ref[:, c:c + half] = x_ref[:, c:c + half] + (0.5 * gate_ref[:, c:c + half]) * acc


def _ffn(x, ng, sh, sc, gate, wg, wu, wd, rows_per_mod):
    N, D = x.shape
    F = wg.shape[1]
    tm = _tile(rows_per_mod, 1024)
    tf = _tile(F, 512, 2 * LANES)
    tn = _tile(D, 512, 2 * LANES)
    tpb = rows_per_mod // tm
    nj = F // tf
    vec = pl.BlockSpec((None, 1, D), lambda i, j: (i // tpb, 0, 0))
    in_specs = [pl.BlockSpec((tm, D), lambda i, j: (i, 0)),
                pl.BlockSpec((1, D), lambda i, j: (0, 0)), vec, vec,
                pl.BlockSpec((D, tf), lambda i, j: (0, j)),
                pl.BlockSpec((D, tf), lambda i, j: (0, j))]
    out_specs = [pl.BlockSpec((tm, tf), lambda i, j: (i, j))]
    out_shape = [jax.ShapeDtypeStruct((N, F), BF16)]
    operands = [x, ng, sh, sc, wg, wu]
    if wd.dtype != BF16:
        wd_rows = pl.BlockSpec((tf, D), lambda i, j: (jnp.where(i == 0, j, nj - 1), 0))
        in_specs.append(wd_rows)
        out_specs.append(wd_rows)
        out_shape.append(jax.ShapeDtypeStruct((F, D), BF16))
        operands.append(wd)
    res = pl.pallas_call(
        functools.partial(_ffn_act_kernel, sub=_tile(tm, 512)),
        grid=(N // tm, nj),
        in_specs=in_specs,
        out_specs=out_specs,
        out_shape=out_shape,
        scratch_shapes=[pltpu.VMEM((tm, D), BF16)],
        compiler_params=_params("arbitrary", "arbitrary"),
        name="ffn_act",
    )(*operands)
    a = res[0]
    if wd.dtype != BF16:
        wd = res[1]
    out = pl.pallas_call(
        _ffn_down_kernel,
        grid=(N // tm, D // tn),
        in_specs=[pl.BlockSpec((tm, F), lambda i, j: (i, 0)),
                  pl.BlockSpec((tm, tn), lambda i, j: (i, j)),
                  pl.BlockSpec((None, 1, tn), lambda i, j: (i // tpb, 0, j)),
                  pl.BlockSpec((F, tn), lambda i, j: (0, j))],
        out_specs=pl.BlockSpec((tm, tn), lambda i, j: (i, j)),
        out_shape=jax.ShapeDtypeStruct((N, D), F32),
        compiler_params=_params("arbitrary", "arbitrary"),
        name="ffn_down",
    )(a, x, gate, wd)
    return out, wd


def _head_norm(acc, g, cos, sin, head_dim):
    quarter = head_dim // 4
    out = []
    for hh in range(acc.shape[1] // head_dim):
        xh = acc[:, hh * head_dim:(hh + 1) * head_dim]
        ms = jnp.mean(xh * xh, axis=-1, keepdims=True)
        xh = (xh * lax.rsqrt(ms + NORM_EPS)) * g
        if cos is not None:
            lane = lax.broadcasted_iota(jnp.int32, xh.shape, 1)
            partner = jnp.where((lane % (2 * quarter)) < quarter,
                                pltpu.roll(xh, head_dim - quarter, 1),
                                pltpu.roll(xh, quarter, 1))
            xh = xh * cos + partner * sin
        out.append(xh.astype(BF16))
    return out[0] if len(out) == 1 else jnp.concatenate(out, axis=1)


def _proj_kernel(x_ref, ng_ref, sh_ref, sc_ref, w_ref, cos_ref, sin_ref, kg_ref,
                 k_ref, v_ref, u_ref, *rest, bounds, rope, head_dim, sub):
    h_ref = rest[-1]
    j = pl.program_id(1)
    tm = x_ref.shape[0]
    jv, ju, jq, jg = bounds

    @pl.when(j == 0)
    def _():
        for r in range(0, tm, sub):
            h_ref[r:r + sub, :] = _norm_mod(x_ref[r:r + sub, :], ng_ref[...], sh_ref[...], sc_ref[...])

    acc = jnp.dot(h_ref[...], w_ref[...], preferred_element_type=F32)
    cos = cos_ref[...] if rope else None
    sin = sin_ref[...] if rope else None

    @pl.when(j < jv)
    def _():
        k_ref[...] = _head_norm(acc, kg_ref[...], cos, sin, head_dim)

    @pl.when((j >= jv) & (j < ju))
    def _():
        v_ref[...] = acc.astype(BF16)

    @pl.when((j >= ju) & (j < jq))
    def _():
        u_ref[...] = acc

    if len(rest) == 3:
        q_ref, gate_ref = rest[0], rest[1]

        @pl.when((j >= jq) & (j < jg))
        def _():
            q_ref[...] = acc

        @pl.when(j >= jg)
        def _():
            gate_ref[...] = acc.astype(BF16)


def _proj(x, ng, sh, sc, w, cos, sin, kg, rows_per_seq, widths, latent):
    N, D = x.shape
    kv_w, ssm_w, attn_w, gate_w = widths
    head_dim = kg.shape[1]
    tn = kv_w
    assert tn % LANES == 0 and ssm_w % tn == 0 and attn_w % tn == 0 and gate_w % tn == 0
    jv, ju = 1, 2
    jq = ju + ssm_w // tn
    jg = jq + attn_w // tn
    nj = jg + gate_w // tn if latent else jq
    tm = _tile(rows_per_seq, 1024)
    tpb = rows_per_seq // tm
    vec = pl.BlockSpec((None, 1, D), lambda i, j: (i // tpb, 0, 0))
    tab = pl.BlockSpec((tm, head_dim), lambda i, j: (i % tpb, 0))
    hvec = pl.BlockSpec((1, head_dim), lambda i, j: (0, 0))

    def out_spec(j0, nblk):
        return pl.BlockSpec((tm, tn), lambda i, j: (i, jnp.clip(j - j0, 0, nblk - 1)))

    out_specs = [out_spec(0, 1), out_spec(jv, 1), out_spec(ju, jq - ju)]
    out_shape = [jax.ShapeDtypeStruct((N, kv_w), BF16), jax.ShapeDtypeStruct((N, kv_w), BF16),
                 jax.ShapeDtypeStruct((N, ssm_w), F32)]
    if latent:
        out_specs += [out_spec(jq, jg - jq), out_spec(jg, nj - jg)]
        out_shape += [jax.ShapeDtypeStruct((N, attn_w), F32), jax.ShapeDtypeStruct((N, gate_w), BF16)]
    return pl.pallas_call(
        functools.partial(_proj_kernel, bounds=(jv, ju, jq, jg), rope=latent, head_dim=head_dim,
                          sub=_tile(tm, 512)),
        grid=(N // tm, nj),
        in_specs=[pl.BlockSpec((tm, D), lambda i, j: (i, 0)),
                  pl.BlockSpec((1, D), lambda i, j: (0, 0)), vec, vec,
                  pl.BlockSpec((D, tn), lambda i, j: (0, j)),
                  tab, tab, hvec],
        out_specs=out_specs,
        out_shape=out_shape,
        scratch_shapes=[pltpu.VMEM((tm, D), BF16)],
        compiler_params=_params("arbitrary", "arbitrary"),
        name="proj_latent" if latent else "proj_context",
    )(x, ng, sh, sc, w, cos, sin, kg)


def _rope_tables(L, head_dim):
    t = np.arange(L)
    row = (t // GRID_W).astype(np.float64)
    col = (t % GRID_W).astype(np.float64)
    half = head_dim // 4
    inv_freq = ROPE_THETA ** (-np.arange(half, dtype=np.float64) / half)
    ar = row[:, None] * inv_freq
    ac = col[:, None] * inv_freq
    cos = np.concatenate([np.cos(ar), np.cos(ar), np.cos(ac), np.cos(ac)], axis=-1)
    sin = np.concatenate([-np.sin(ar), np.sin(ar), -np.sin(ac), np.sin(ac)], axis=-1)
    return jnp.asarray(cos, F32), jnp.asarray(sin, F32)


def _attn_kernel(q_ref, cos_ref, sin_ref, qn_ref, cosn_ref, sinn_ref, qg_ref,
                 kc_ref, k_ref, vc_ref, v_ref, o_ref, kall_ref, vext_ref, qs_ref, *, head_dim):
    lc = kc_ref.shape[0]
    i = pl.program_id(2)

    @pl.when(i == 0)
    def _():
        kall_ref[0:lc, :] = kc_ref[...]
        kall_ref[lc:, :] = k_ref[...]
        vext_ref[0:lc, 0:head_dim] = vc_ref[...]
        vext_ref[lc:, 0:head_dim] = v_ref[...]
        vext_ref[:, head_dim:] = jnp.ones((vext_ref.shape[0], head_dim), BF16)
        qs_ref[0] = _head_norm(q_ref[...], qg_ref[...], cos_ref[...], sin_ref[...], head_dim)

    k = kall_ref[...]
    vext = vext_ref[...]
    n_rep = q_ref.shape[1] // head_dim
    heads = [slice(r * head_dim, (r + 1) * head_dim) for r in range(n_rep)]

    def scores(sl):
        return lax.dot_general(qs_ref[i % 2, :, sl], k, (((1,), (1,)), ((), ())),
                               preferred_element_type=F32)

    s_next = scores(heads[0])
    for r, sl in enumerate(heads):
        s = s_next
        if r + 1 < n_rep:
            s_next = scores(heads[r + 1])
        qs_ref[(i + 1) % 2, :, sl] = _head_norm(qn_ref[:, sl], qg_ref[...], cosn_ref[...],
                                                sinn_ref[...], head_dim)
        m = jnp.max(s, axis=-1, keepdims=True)
        p = jnp.exp2(s - m).astype(BF16)
        oe = jnp.dot(p, vext, preferred_element_type=F32)
        o_ref[:, sl] = (oe[:, :head_dim] / oe[:, head_dim:]).astype(o_ref.dtype)


def _attention(q, cos, sin, qg, kc, k, vc, v, B, head_dim):
    N, W = q.shape
    L, Lc = N // B, kc.shape[0] // B
    kvh = k.shape[1] // head_dim
    gw = W // kvh
    tq = _tile(L, 512)
    tpb = L // tq
    kv_lat = pl.BlockSpec((L, head_dim), lambda b, h, i: (b, h))
    kv_ctx = pl.BlockSpec((Lc, head_dim), lambda b, h, i: (b, h))
    nxt = lambda i: jnp.minimum(i + 1, tpb - 1)
    tab = pl.BlockSpec((tq, head_dim), lambda b, h, i: (i, 0))
    tab_next = pl.BlockSpec((tq, head_dim), lambda b, h, i: (nxt(i), 0))
    return pl.pallas_call(
        functools.partial(_attn_kernel, head_dim=head_dim),
        grid=(B, kvh, tpb),
        in_specs=[pl.BlockSpec((tq, gw), lambda b, h, i: (b * tpb + i, h)), tab, tab,
                  pl.BlockSpec((tq, gw), lambda b, h, i: (b * tpb + nxt(i), h)), tab_next, tab_next,
                  pl.BlockSpec((1, head_dim), lambda b, h, i: (0, 0)),
                  kv_ctx, kv_lat, kv_ctx, kv_lat],
        out_specs=pl.BlockSpec((tq, gw), lambda b, h, i: (b * tpb + i, h)),
        out_shape=jax.ShapeDtypeStruct((N, W), BF16),
        scratch_shapes=[pltpu.VMEM((L + Lc, head_dim), BF16),
                        pltpu.VMEM((L + Lc, 2 * head_dim), BF16),
                        pltpu.VMEM((2, tq, gw), BF16)],
        compiler_params=_params("arbitrary", "arbitrary", "arbitrary"),
        name="attention",
    )(q, cos, sin, q, cos, sin, qg, kc, k, vc, v)


def _ssm_operators(a_re, a_im, log_dt, b_re, b_im, c_re, c_im):
    T = SSM_CHUNK
    hp = lax.Precision.HIGHEST
    a_re, a_im = a_re.astype(F32), a_im.astype(F32)
    dt = jnp.exp(log_dt.astype(F32))[..., None]
    mag = jnp.exp(a_re * dt)
    lr = mag * jnp.cos(a_im * dt)
    li = mag * jnp.sin(a_im * dt)
    den = a_re * a_re + a_im * a_im
    cr = ((lr - 1.0) * a_re + li * a_im) / den
    ci = (li * a_re - (lr - 1.0) * a_im) / den
    pr, pi = [jnp.ones_like(lr)], [jnp.zeros_like(lr)]
    for _ in range(T):
        pr.append(pr[-1] * lr - pi[-1] * li)
        pi.append(pr[-2] * li + pi[-1] * lr)
    pw_r, pw_i = jnp.stack(pr), jnp.stack(pi)
    _, G, P, E = b_re.shape
    bt_r = jnp.swapaxes(b_re.astype(F32), 2, 3)
    bt_i = jnp.swapaxes(b_im.astype(F32), 2, 3)
    bb_r = cr[:, :, None, :] * bt_r - ci[:, :, None, :] * bt_i
    bb_i = cr[:, :, None, :] * bt_i + ci[:, :, None, :] * bt_r
    c_r, c_i = c_re.astype(F32), c_im.astype(F32)

    def cmul(xr, xi, wr, wi):
        wr = jnp.moveaxis(wr, 0, 2)[:, :, :, None, :]
        wi = jnp.moveaxis(wi, 0, 2)[:, :, :, None, :]
        xr, xi = xr[:, :, None], xi[:, :, None]
        return ((xr * wr - xi * wi).reshape(2, G, T * E, P),
                (xr * wi + xi * wr).reshape(2, G, T * E, P))

    steps = jnp.arange(T)

    def strip(d, lags):
        rr, ri = cmul(c_r, c_i, pw_r[lags], pw_i[lags])
        return (jnp.einsum("gcp,gjp->gcj", bb_r[d], rr[d], precision=hp)
                - jnp.einsum("gcp,gjp->gcj", bb_i[d], ri[d], precision=hp))

    w_strip = jnp.stack([strip(0, steps), strip(1, T - 1 - steps)])
    in_f = cmul(bb_r, bb_i, pw_r[T - 1 - steps], pw_i[T - 1 - steps])
    in_b = cmul(bb_r, bb_i, pw_r[steps], pw_i[steps])
    m_in = jnp.stack([jnp.concatenate([in_f[0][0], in_f[1][0], in_f[1][0], in_f[0][0]], axis=-1),
                      jnp.concatenate([in_b[0][1], in_b[1][1], in_b[1][1], in_b[0][1]], axis=-1)])
    out_f = cmul(c_r, c_i, pw_r[steps + 1], pw_i[steps + 1])
    out_b = cmul(c_r, c_i, pw_r[T - steps], pw_i[T - steps])
    m_out = jnp.stack([jnp.concatenate([out_f[0][0], -out_f[1][0]], axis=-1),
                       jnp.concatenate([out_b[0][1], -out_b[1][1]], axis=-1)])
    ar, ai = pw_r[T], pw_i[T]
    dec = jnp.stack([jnp.concatenate([ar, ar], -1), jnp.concatenate([-ai, ai], -1)], axis=2)
    return w_strip, m_in.astype(BF16), m_out.astype(BF16), dec


def _intra_operator(w_ref, g, e):
    T = SSM_CHUNK
    wf, wb = w_ref[0, g], w_ref[1, g]
    t_of_lane = lax.broadcasted_iota(jnp.int32, wf.shape, 1) // e
    rows = []
    for s in range(T):
        f = wf if s == 0 else jnp.where(t_of_lane >= s, pltpu.roll(wf, s * e, 1), 0.0)
        b = wb if s == T - 1 else jnp.where(t_of_lane <= s, pltpu.roll(wb, (s + 1) * e, 1), 0.0)
        rows.append(f + b)
    return jnp.concatenate(rows, axis=0).astype(BF16)


def _row_pitch(n):
    p = -(-n // 8)
    return 8 * (p if p % 2 else p + 1)


def _ssm_kernel(u_ref, uc_ref, d_ref, w_ref, min_ref, mo_ref, dec_ref, y_ref,
                z_ref, s1_ref, s2_ref, hp_ref, *, batch, e):
    T = SSM_CHUNK
    gpb = w_ref.shape[1]
    rp = z_ref.shape[1]
    n_lat, n_ctx = u_ref.shape[0] // T, uc_ref.shape[0] // T
    ncl, ncc = n_lat // batch, n_ctx // batch
    pl_, pc_ = _row_pitch(ncl), _row_pitch(ncc)
    lat0 = batch * pc_
    r = lat0 + batch * pl_
    lanes = u_ref.shape[1]
    pw = hp_ref.shape[3]

    def padded(rows, n, pitch):
        if pitch == n:
            return [rows]
        out = []
        for b in range(batch):
            out += [rows[b * n:(b + 1) * n, :], jnp.zeros((pitch - n, lanes), F32)]
        return out

    xt = []
    for s in range(T):
        parts = (padded(uc_ref[pl.ds(s, n_ctx, stride=T), :], ncc, pc_)
                 + padded(u_ref[pl.ds(s, n_lat, stride=T), :], ncl, pl_))
        if rp > r:
            parts.append(jnp.zeros((rp - r, lanes), F32))
        xt.append(jnp.concatenate(parts, axis=0).T)
    for g in range(gpb):
        zt = jnp.concatenate([xt[s][g * e:(g + 1) * e, :] for s in range(T)], axis=0)
        z_ref[g] = zt.T.astype(BF16)
    for d in range(2):
        for g in range(gpb):
            st = jnp.dot(z_ref[g], min_ref[d, g], preferred_element_type=F32)
            s1_ref[d, g] = st[:, :pw]
            s2_ref[d, g] = st[:, pw:]

    def sweep(base, count, stride, carry):
        def body(it, carry):
            out = []
            for d in range(2):
                n = it if d == 0 else count - 1 - it
                rows = pl.ds(base + n, batch, stride=stride)
                for g in range(gpb):
                    h1, h2 = carry[2 * (d * gpb + g)], carry[2 * (d * gpb + g) + 1]
                    hp_ref[d, g, rows, :] = h1
                    a1 = dec_ref[d, g, 0:1, :]
                    a2 = dec_ref[d, g, 1:2, :]
                    out.append(a1 * h1 + a2 * h2 + s1_ref[d, g, rows, :])
                    out.append(a1 * h2 - a2 * h1 + s2_ref[d, g, rows, :])
            return tuple(out)
        return lax.fori_loop(0, count, body, carry)

    for d in range(2):
        for g in range(gpb):
            for b in range(batch):
                if pc_ > ncc:
                    hp_ref[d, g, b * pc_ + ncc:(b + 1) * pc_, :] = jnp.zeros((pc_ - ncc, pw), F32)
                if pl_ > ncl:
                    hp_ref[d, g, lat0 + b * pl_ + ncl:lat0 + (b + 1) * pl_, :] = jnp.zeros((pl_ - ncl, pw), F32)
            if rp > r:
                hp_ref[d, g, r:rp, :] = jnp.zeros((rp - r, pw), F32)
    zero = jnp.zeros((batch, pw), F32)
    carry = sweep(0, ncc, pc_, tuple([zero] * (4 * gpb)))
    sweep(lat0, ncl, pl_, carry)
    yt = []
    for g in range(gpb):
        y = jnp.dot(z_ref[g], _intra_operator(w_ref, g, e), preferred_element_type=F32)
        for d in range(2):
            y += lax.dot_general(hp_ref[d, g].astype(BF16), mo_ref[d, g],
                                 (((1,), (1,)), ((), ())), preferred_element_type=F32)
        yt.append(y.T)
    dvec = d_ref[...]
    for t in range(T):
        blk = jnp.concatenate([yt[g][t * e:(t + 1) * e, :] for g in range(gpb)], axis=0).T
        for b in range(batch):
            rows = pl.ds(b * ncl * T + t, ncl, stride=T)
            y_ref[rows, :] = (blk[lat0 + b * pl_:lat0 + b * pl_ + ncl, :] + dvec * u_ref[rows, :])


def _s5(u, uc, d, ops, batch):
    w_strip, m_in, m_out, dec = ops
    N, W = u.shape
    Nc = uc.shape[0]
    G = w_strip.shape[1]
    E = W // G
    gpb = LANES // E
    TE = SSM_CHUNK * E
    P2 = dec.shape[-1]
    assert P2 == LANES and W % LANES == 0
    r = batch * (_row_pitch(N // SSM_CHUNK // batch) + _row_pitch(Nc // SSM_CHUNK // batch))
    rp = -(-r // LANES) * LANES
    return pl.pallas_call(
        functools.partial(_ssm_kernel, batch=batch, e=E),
        grid=(W // LANES,),
        in_specs=[pl.BlockSpec((N, LANES), lambda j: (0, j)),
                  pl.BlockSpec((Nc, LANES), lambda j: (0, j)),
                  pl.BlockSpec((1, LANES), lambda j: (0, j)),
                  pl.BlockSpec((2, gpb, E, TE), lambda j: (0, j, 0, 0)),
                  pl.BlockSpec((2, gpb, TE, 2 * P2), lambda j: (0, j, 0, 0)),
                  pl.BlockSpec((2, gpb, TE, P2), lambda j: (0, j, 0, 0)),
                  pl.BlockSpec((2, gpb, 2, P2), lambda j: (0, j, 0, 0))],
        out_specs=pl.BlockSpec((N, LANES), lambda j: (0, j)),
        out_shape=jax.ShapeDtypeStruct((N, W), F32),
        scratch_shapes=[pltpu.VMEM((gpb, rp, TE), BF16),
                        pltpu.VMEM((2, gpb, rp, P2), F32),
                        pltpu.VMEM((2, gpb, rp, P2), F32),
                        pltpu.VMEM((2, gpb, rp, P2), F32)],
        compiler_params=_params("arbitrary"),
        name="s5",
    )(u, uc, d, w_strip, m_in, m_out, dec)


def _mixer_out_kernel(attn_ref, y_ref, ga_ref, gs_ref, x_ref, g2_ref, wglu_ref, bglu_ref,
                      wa_ref, ws_ref, wo_ref, o_ref, y2_ref, m_ref, *, sub):
    j = pl.program_id(1)
    tm, tn = o_ref.shape
    nd = m_ref.shape[1] // tn
    half = tn // 2

    @pl.when(j == 0)
    def _():
        for r in range(0, tm, sub):
            y = jax.nn.gelu(y_ref[r:r + sub, :])
            z = jnp.dot(y.astype(BF16), wglu_ref[...], preferred_element_type=F32) + bglu_ref[...]
            y2_ref[r:r + sub, :] = (y * jax.nn.sigmoid(z)).astype(BF16)

    @pl.when(j < nd)
    def _():
        for c in (0, half):
            pa = jnp.dot(attn_ref[...], wa_ref[:, c:c + half], preferred_element_type=F32)
            ps = jnp.dot(y2_ref[...], ws_ref[:, c:c + half], preferred_element_type=F32)
            merged = (jax.nn.sigmoid(ga_ref[:, c:c + half].astype(F32)) * pa
                      + jax.nn.sigmoid(gs_ref[:, c:c + half].astype(F32)) * ps)
            m_ref[:, pl.ds(pl.multiple_of(j * tn + c, half), half)] = merged.astype(BF16)

    @pl.when(j >= nd)
    def _():
        for c in (0, half):
            acc = jnp.dot(m_ref[...], wo_ref[:, c:c + half], preferred_element_type=F32)
            o_ref[:, c:c + half] = x_ref[:, c:c + half] + g2_ref[:, c:c + half] * acc


def _mixer_out(attn, y_ssm, gate, x, g2, w_glu, b_glu, wa, ws, wo, rows_per_mod):
    N, D = x.shape
    WA, WS = attn.shape[1], y_ssm.shape[1]
    tm = _tile(rows_per_mod, 1024)
    tn = _tile(D, 512, 2 * LANES)
    nd = D // tn
    tpb = rows_per_mod // tm
    lo = lambda j: jnp.minimum(j, nd - 1)
    hi = lambda j: jnp.maximum(j - nd, 0)
    return pl.pallas_call(
        functools.partial(_mixer_out_kernel, sub=_tile(tm, 512)),
        grid=(N // tm, 2 * nd),
        in_specs=[pl.BlockSpec((tm, WA), lambda i, j: (i, 0)),
                  pl.BlockSpec((tm, WS), lambda i, j: (i, 0)),
                  pl.BlockSpec((tm, tn), lambda i, j: (i, lo(j))),
                  pl.BlockSpec((tm, tn), lambda i, j: (i, nd + lo(j))),
                  pl.BlockSpec((tm, tn), lambda i, j: (i, hi(j))),
                  pl.BlockSpec((None, 1, tn), lambda i, j: (i // tpb, 0, hi(j))),
                  pl.BlockSpec((WS, WS), lambda i, j: (0, 0)),
                  pl.BlockSpec((1, WS), lambda i, j: (0, 0)),
                  pl.BlockSpec((WA, tn), lambda i, j: (0, lo(j))),
                  pl.BlockSpec((WS, tn), lambda i, j: (0, lo(j))),
                  pl.BlockSpec((D, tn), lambda i, j: (0, hi(j)))],
        out_specs=pl.BlockSpec((tm, tn), lambda i, j: (i, hi(j))),
        out_shape=jax.ShapeDtypeStruct((N, D), F32),
        scratch_shapes=[pltpu.VMEM((tm, WS), BF16), pltpu.VMEM((tm, D), BF16)],
        compiler_params=_params("arbitrary", "arbitrary"),
        name="mixer_out",
    )(attn, y_ssm, gate, gate, x, g2, w_glu, b_glu, wa, ws, wo)


def kernel(x, c, ctx, c_ctx, w_mod, b_mod, norm_g, w_ffn1_gate, w_ffn1_up, w_ffn1_down, w_in, q_norm_g, k_norm_g, ssm_a_re, ssm_a_im, ssm_log_dt, ssm_b_re, ssm_b_im, ssm_c_re, ssm_c_im, ssm_d, w_glu, b_glu, w_br_attn, w_br_ssm, w_out, w_ffn2_gate, w_ffn2_up, w_ffn2_down):
    B, L, D = x.shape
    Lc = ctx.shape[1]
    assert w_mod.shape[0] == 1, "only the single (last) layer configuration is implemented"
    hd = q_norm_g.shape[1]
    ssm_w = w_glu.shape[1]
    attn_w = w_br_attn.shape[1]
    kv_w = attn_w // Q_PER_KV
    widths = (kv_w, ssm_w, attn_w, 2 * D)
    assert L % GRID_W == 0 and L % SSM_CHUNK == 0 and Lc % SSM_CHUNK == 0 and B <= 7
    N, Nc = B * L, B * Lc
    l = 0

    cc = jnp.zeros((8, D), F32).at[:B].set(c).at[B].set(c_ctx)
    mod = _modulation(cc, w_mod[l], b_mod[l][None, :])

    def mvec(k, lo, hi):
        return mod[lo:hi, k * D:(k + 1) * D][:, None, :]

    xm = [mvec(k, 0, B) for k in range(N_MOD)]
    cm = [mvec(k, B, B + 1) for k in range(5)]
    ng = norm_g[l][:, None, :]

    x2 = x.reshape(N, D)
    c2 = ctx.reshape(Nc, D)
    w_in_b = w_in[l].astype(BF16)

    x2, wd1 = _ffn(x2, ng[0], xm[0], xm[1], xm[2], w_ffn1_gate[l], w_ffn1_up[l], w_ffn1_down[l], L)
    c2, _ = _ffn(c2, ng[0], cm[0], cm[1], cm[2], w_ffn1_gate[l], w_ffn1_up[l], wd1, Nc)

    cos, sin = _rope_tables(L, hd)
    kg = k_norm_g[l][None, :]
    qg = q_norm_g[l][None, :] * (hd ** -0.5 * math.log2(math.e))
    k, v, u, q, gate = _proj(x2, ng[1], xm[3], xm[4], w_in_b, cos, sin, kg, L, widths, True)
    kc, vc, uc = _proj(c2, ng[1], cm[3], cm[4], w_in_b, cos, sin, kg, Nc, widths, False)

    attn = _attention(q, cos, sin, qg, kc, k, vc, v, B, hd)
    ops = _ssm_operators(ssm_a_re[l], ssm_a_im[l], ssm_log_dt[l], ssm_b_re[l], ssm_b_im[l],
                         ssm_c_re[l], ssm_c_im[l])
    y_ssm = _s5(u, uc, ssm_d[l][None, :], ops, B)

    x2 = _mixer_out(attn, y_ssm, gate, x2, xm[5], w_glu[l].astype(BF16), b_glu[l][None, :],
                    w_br_attn[l].astype(BF16), w_br_ssm[l].astype(BF16), w_out[l].astype(BF16), L)

    x2, _ = _ffn(x2, ng[2], xm[6], xm[7], xm[8], w_ffn2_gate[l], w_ffn2_up[l], w_ffn2_down[l], L)
    return x2.reshape(B, L, D)
```

```python
import functools
import math

import jax
import jax.numpy as jnp
import numpy as np
from jax import lax
from jax.experimental import pallas as pl
from jax.experimental.pallas import tpu as pltpu

F32 = jnp.float32
BF16 = jnp.bfloat16

GRID_W = 64
ROPE_THETA = 10000.0
NORM_EPS = 1e-6
N_MOD = 9
Q_PER_KV = 4
SSM_CHUNK = 16
LANES = 128
VMEM_LIMIT_BYTES = 58 * 1024 * 1024


def _tile(n, pref, mult=8):
    if n <= pref:
        return n
    for t in range(pref, 0, -1):
        if n % t == 0 and t % mult == 0:
            return t
    return n


def _params(*sem):
    return pltpu.CompilerParams(dimension_semantics=sem, vmem_limit_bytes=VMEM_LIMIT_BYTES)


def _norm_mod(x, g, sh, sc):
    ms = jnp.mean(x * x, axis=-1, keepdims=True)
    h = (x * lax.rsqrt(ms + NORM_EPS)) * g
    return (h * (1.0 + sc) + sh).astype(BF16)


def _mod_kernel(c_ref, w_ref, b_ref, o_ref):
    c = c_ref[...]
    s = c * jax.nn.sigmoid(c)
    o_ref[...] = jnp.dot(s.astype(BF16), w_ref[...].astype(BF16),
                         preferred_element_type=F32) + b_ref[...]


def _modulation(cc, w_mod, b_mod):
    D, M = w_mod.shape
    tn = _tile(M, 1024, LANES)
    return pl.pallas_call(
        _mod_kernel,
        grid=(M // tn,),
        in_specs=[pl.BlockSpec((cc.shape[0], D), lambda j: (0, 0)),
                  pl.BlockSpec((D, tn), lambda j: (0, j)),
                  pl.BlockSpec((1, tn), lambda j: (0, j))],
        out_specs=pl.BlockSpec((cc.shape[0], tn), lambda j: (0, j)),
        out_shape=jax.ShapeDtypeStruct((cc.shape[0], M), F32),
        compiler_params=_params("arbitrary"),
        name="modulation",
    )(cc, w_mod, b_mod)


def _ffn_act_kernel(x_ref, ng_ref, sh_ref, sc_ref, wg_ref, wu_ref, *rest, sub):
    if len(rest) == 4:
        wd_ref, a_ref, wdb_ref, h_ref = rest

        @pl.when(pl.program_id(0) == 0)
        def _():
            wdb_ref[...] = wd_ref[...].astype(BF16)
    else:
        a_ref, h_ref = rest

    @pl.when(pl.program_id(1) == 0)
    def _():
        for r in range(0, x_ref.shape[0], sub):
            h_ref[r:r + sub, :] = _norm_mod(x_ref[r:r + sub, :], ng_ref[...], sh_ref[...], sc_ref[...])

    h = h_ref[...]
    half = a_ref.shape[1] // 2
    for c in (0, half):
        g = jnp.dot(h, wg_ref[:, c:c + half].astype(BF16), preferred_element_type=F32)
        u = jnp.dot(h, wu_ref[:, c:c + half].astype(BF16), preferred_element_type=F32)
        a_ref[:, c:c + half] = ((g * jax.nn.sigmoid(g)) * u).astype(a_ref.dtype)


def _ffn_down_kernel(a_ref, x_ref, gate_ref, wd_ref, o_ref):
    half = o_ref.shape[1] // 2
    for c in (0, half):
        acc = jnp.dot(a_ref[...], wd_ref[:, c:c + half], preferred_element_type=F32)
        o_ref[:, c:c + half] = x_ref[:, c:c + half] + (0.5 * gate_ref[:, c:c + half]) * acc


def _ffn(x, ng, sh, sc, gate, wg, wu, wd, rows_per_mod):
    N, D = x.shape
    F = wg.shape[1]
    tm = _tile(rows_per_mod, 1024)
    tf = _tile(F, 512, 2 * LANES)
    tn = _tile(D, 512, 2 * LANES)
    tpb = rows_per_mod // tm
    nj = F // tf
    vec = pl.BlockSpec((None, 1, D), lambda i, j: (i // tpb, 0, 0))
    in_specs = [pl.BlockSpec((tm, D), lambda i, j: (i, 0)),
                pl.BlockSpec((1, D), lambda i, j: (0, 0)), vec, vec,
                pl.BlockSpec((D, tf), lambda i, j: (0, j)),
                pl.BlockSpec((D, tf), lambda i, j: (0, j))]
    out_specs = [pl.BlockSpec((tm, tf), lambda i, j: (i, j))]
    out_shape = [jax.ShapeDtypeStruct((N, F), BF16)]
    operands = [x, ng, sh, sc, wg, wu]
    if wd.dtype != BF16:
        wd_rows = pl.BlockSpec((tf, D), lambda i, j: (jnp.where(i == 0, j, nj - 1), 0))
        in_specs.append(wd_rows)
        out_specs.append(wd_rows)
        out_shape.append(jax.ShapeDtypeStruct((F, D), BF16))
        operands.append(wd)
    res = pl.pallas_call(
        functools.partial(_ffn_act_kernel, sub=_tile(tm, 512)),
        grid=(N // tm, nj),
        in_specs=in_specs,
        out_specs=out_specs,
        out_shape=out_shape,
        scratch_shapes=[pltpu.VMEM((tm, D), BF16)],
        compiler_params=_params("arbitrary", "arbitrary"),
        name="ffn_act",
    )(*operands)
    a = res[0]
    if wd.dtype != BF16:
        wd = res[1]
    out = pl.pallas_call(
        _ffn_down_kernel,
        grid=(N // tm, D // tn),
        in_specs=[pl.BlockSpec((tm, F), lambda i, j: (i, 0)),
                  pl.BlockSpec((tm, tn), lambda i, j: (i, j)),
                  pl.BlockSpec((None, 1, tn), lambda i, j: (i // tpb, 0, j)),
                  pl.BlockSpec((F, tn), lambda i, j: (0, j))],
        out_specs=pl.BlockSpec((tm, tn), lambda i, j: (i, j)),
        out_shape=jax.ShapeDtypeStruct((N, D), F32),
        compiler_params=_params("arbitrary", "arbitrary"),
        name="ffn_down",
    )(a, x, gate, wd)
    return out, wd


def _head_norm(acc, g, cos, sin, head_dim):
    quarter = head_dim // 4
    out = []
    for hh in range(acc.shape[1] // head_dim):
        xh = acc[:, hh * head_dim:(hh + 1) * head_dim]
        ms = jnp.mean(xh * xh, axis=-1, keepdims=True)
        xh = (xh * lax.rsqrt(ms + NORM_EPS)) * g
        if cos is not None:
            lane = lax.broadcasted_iota(jnp.int32, xh.shape, 1)
            partner = jnp.where((lane % (2 * quarter)) < quarter,
                                pltpu.roll(xh, head_dim - quarter, 1),
                                pltpu.roll(xh, quarter, 1))
            xh = xh * cos + partner * sin
        out.append(xh.astype(BF16))
    return out[0] if len(out) == 1 else jnp.concatenate(out, axis=1)


def _proj_kernel(x_ref, ng_ref, sh_ref, sc_ref, w_ref, cos_ref, sin_ref, kg_ref,
                 k_ref, v_ref, u_ref, *rest, bounds, rope, head_dim, sub):
    h_ref = rest[-1]
    j = pl.program_id(1)
    tm = x_ref.shape[0]
    jv, ju, jq, jg = bounds

    @pl.when(j == 0)
    def _():
        for r in range(0, tm, sub):
            h_ref[r:r + sub, :] = _norm_mod(x_ref[r:r + sub, :], ng_ref[...], sh_ref[...], sc_ref[...])

    acc = jnp.dot(h_ref[...], w_ref[...], preferred_element_type=F32)
    cos = cos_ref[...] if rope else None
    sin = sin_ref[...] if rope else None

    @pl.when(j < jv)
    def _():
        k_ref[...] = _head_norm(acc, kg_ref[...], cos, sin, head_dim)

    @pl.when((j >= jv) & (j < ju))
    def _():
        v_ref[...] = acc.astype(BF16)

    @pl.when((j >= ju) & (j < jq))
    def _():
        u_ref[...] = acc

    if len(rest) == 3:
        q_ref, gate_ref = rest[0], rest[1]

        @pl.when((j >= jq) & (j < jg))
        def _():
            q_ref[...] = acc

        @pl.when(j >= jg)
        def _():
            gate_ref[...] = acc.astype(BF16)


def _proj(x, ng, sh, sc, w, cos, sin, kg, rows_per_seq, widths, latent):
    N, D = x.shape
    kv_w, ssm_w, attn_w, gate_w = widths
    head_dim = kg.shape[1]
    tn = kv_w
    assert tn % LANES == 0 and ssm_w % tn == 0 and attn_w % tn == 0 and gate_w % tn == 0
    jv, ju = 1, 2
    jq = ju + ssm_w // tn
    jg = jq + attn_w // tn
    nj = jg + gate_w // tn if latent else jq
    tm = _tile(rows_per_seq, 1024)
    tpb = rows_per_seq // tm
    vec = pl.BlockSpec((None, 1, D), lambda i, j: (i // tpb, 0, 0))
    tab = pl.BlockSpec((tm, head_dim), lambda i, j: (i % tpb, 0))
    hvec = pl.BlockSpec((1, head_dim), lambda i, j: (0, 0))

    def out_spec(j0, nblk):
        return pl.BlockSpec((tm, tn), lambda i, j: (i, jnp.clip(j - j0, 0, nblk - 1)))

    out_specs = [out_spec(0, 1), out_spec(jv, 1), out_spec(ju, jq - ju)]
    out_shape = [jax.ShapeDtypeStruct((N, kv_w), BF16), jax.ShapeDtypeStruct((N, kv_w), BF16),
                 jax.ShapeDtypeStruct((N, ssm_w), F32)]
    if latent:
        out_specs += [out_spec(jq, jg - jq), out_spec(jg, nj - jg)]
        out_shape += [jax.ShapeDtypeStruct((N, attn_w), F32), jax.ShapeDtypeStruct((N, gate_w), BF16)]
    return pl.pallas_call(
        functools.partial(_proj_kernel, bounds=(jv, ju, jq, jg), rope=latent, head_dim=head_dim,
                          sub=_tile(tm, 512)),
        grid=(N // tm, nj),
        in_specs=[pl.BlockSpec((tm, D), lambda i, j: (i, 0)),
                  pl.BlockSpec((1, D), lambda i, j: (0, 0)), vec, vec,
                  pl.BlockSpec((D, tn), lambda i, j: (0, j)),
                  tab, tab, hvec],
        out_specs=out_specs,
        out_shape=out_shape,
        scratch_shapes=[pltpu.VMEM((tm, D), BF16)],
        compiler_params=_params("arbitrary", "arbitrary"),
        name="proj_latent" if latent else "proj_context",
    )(x, ng, sh, sc, w, cos, sin, kg)


def _rope_tables(L, head_dim):
    t = np.arange(L)
    row = (t // GRID_W).astype(np.float64)
    col = (t % GRID_W).astype(np.float64)
    half = head_dim // 4
    inv_freq = ROPE_THETA ** (-np.arange(half, dtype=np.float64) / half)
    ar = row[:, None] * inv_freq
    ac = col[:, None] * inv_freq
    cos = np.concatenate([np.cos(ar), np.cos(ar), np.cos(ac), np.cos(ac)], axis=-1)
    sin = np.concatenate([-np.sin(ar), np.sin(ar), -np.sin(ac), np.sin(ac)], axis=-1)
    return jnp.asarray(cos, F32), jnp.asarray(sin, F32)


def _attn_kernel(q_ref, cos_ref, sin_ref, qn_ref, cosn_ref, sinn_ref, qg_ref,
                 kc_ref, k_ref, vc_ref, v_ref, o_ref, kall_ref, vext_ref, qs_ref, *, head_dim):
    lc = kc_ref.shape[0]
    i = pl.program_id(2)

    @pl.when(i == 0)
    def _():
        kall_ref[0:lc, :] = kc_ref[...]
        kall_ref[lc:, :] = k_ref[...]
        vext_ref[0:lc, 0:head_dim] = vc_ref[...]
        vext_ref[lc:, 0:head_dim] = v_ref[...]
        vext_ref[:, head_dim:] = jnp.ones((vext_ref.shape[0], head_dim), BF16)
        qs_ref[0] = _head_norm(q_ref[...], qg_ref[...], cos_ref[...], sin_ref[...], head_dim)

    k = kall_ref[...]
    vext = vext_ref[...]
    n_rep = q_ref.shape[1] // head_dim
    heads = [slice(r * head_dim, (r + 1) * head_dim) for r in range(n_rep)]

    def scores(sl):
        return lax.dot_general(qs_ref[i % 2, :, sl], k, (((1,), (1,)), ((), ())),
                               preferred_element_type=F32)

    s_next = scores(heads[0])
    for r, sl in enumerate(heads):
        s = s_next
        if r + 1 < n_rep:
            s_next = scores(heads[r + 1])
        qs_ref[(i + 1) % 2, :, sl] = _head_norm(qn_ref[:, sl], qg_ref[...], cosn_ref[...],
                                                sinn_ref[...], head_dim)
        m = jnp.max(s, axis=-1, keepdims=True)
        p = jnp.exp2(s - m).astype(BF16)
        oe = jnp.dot(p, vext, preferred_element_type=F32)
        o_ref[:, sl] = (oe[:, :head_dim] / oe[:, head_dim:]).astype(o_ref.dtype)


def _attention(q, cos, sin, qg, kc, k, vc, v, B, head_dim):
    N, W = q.shape
    L, Lc = N // B, kc.shape[0] // B
    kvh = k.shape[1] // head_dim
    gw = W // kvh
    tq = _tile(L, 512)
    tpb = L // tq
    kv_lat = pl.BlockSpec((L, head_dim), lambda b, h, i: (b, h))
    kv_ctx = pl.BlockSpec((Lc, head_dim), lambda b, h, i: (b, h))
    nxt = lambda i: jnp.minimum(i + 1, tpb - 1)
    tab = pl.BlockSpec((tq, head_dim), lambda b, h, i: (i, 0))
    tab_next = pl.BlockSpec((tq, head_dim), lambda b, h, i: (nxt(i), 0))
    return pl.pallas_call(
        functools.partial(_attn_kernel, head_dim=head_dim),
        grid=(B, kvh, tpb),
        in_specs=[pl.BlockSpec((tq, gw), lambda b, h, i: (b * tpb + i, h)), tab, tab,
                  pl.BlockSpec((tq, gw), lambda b, h, i: (b * tpb + nxt(i), h)), tab_next, tab_next,
                  pl.BlockSpec((1, head_dim), lambda b, h, i: (0, 0)),
                  kv_ctx, kv_lat, kv_ctx, kv_lat],
        out_specs=pl.BlockSpec((tq, gw), lambda b, h, i: (b * tpb + i, h)),
        out_shape=jax.ShapeDtypeStruct((N, W), BF16),
        scratch_shapes=[pltpu.VMEM((L + Lc, head_dim), BF16),
                        pltpu.VMEM((L + Lc, 2 * head_dim), BF16),
                        pltpu.VMEM((2, tq, gw), BF16)],
        compiler_params=_params("arbitrary", "arbitrary", "arbitrary"),
        name="attention",
    )(q, cos, sin, q, cos, sin, qg, kc, k, vc, v)


def _ssm_operators(a_re, a_im, log_dt, b_re, b_im, c_re, c_im):
    T = SSM_CHUNK
    a_re, a_im = a_re.astype(F32), a_im.astype(F32)
    dt = jnp.exp(log_dt.astype(F32))[..., None]
    mag = jnp.exp(a_re * dt)
    lr = mag * jnp.cos(a_im * dt)
    li = mag * jnp.sin(a_im * dt)
    den = a_re * a_re + a_im * a_im
    cr = ((lr - 1.0) * a_re + li * a_im) / den
    ci = (li * a_re - (lr - 1.0) * a_im) / den
    pr, pi = [jnp.ones_like(lr)], [jnp.zeros_like(lr)]
    for _ in range(T):
        pr.append(pr[-1] * lr - pi[-1] * li)
        pi.append(pr[-2] * li + pi[-1] * lr)
    pw_r, pw_i = jnp.stack(pr), jnp.stack(pi)
    _, G, P, E = b_re.shape
    bt_r = jnp.swapaxes(b_re.astype(F32), 2, 3)
    bt_i = jnp.swapaxes(b_im.astype(F32), 2, 3)
    bb_r = cr[:, :, None, :] * bt_r - ci[:, :, None, :] * bt_i
    bb_i = cr[:, :, None, :] * bt_i + ci[:, :, None, :] * bt_r
    c_r, c_i = c_re.astype(F32), c_im.astype(F32)

    def lag_table(pw, fwd_lags, bwd_lags, reps):
        tab = jnp.stack([jnp.moveaxis(pw[fwd_lags, 0], 0, 1), jnp.moveaxis(pw[bwd_lags, 1], 0, 1)])
        return jnp.concatenate([tab] * reps, axis=-1)

    steps = np.arange(T)
    lags = np.arange(T + 1)
    p1 = lag_table(pw_r, T - 1 - steps, steps, 4)
    p2 = lag_table(pw_i, T - 1 - steps, steps, 4)
    q1 = lag_table(pw_r, lags, T - lags, 2)
    q2 = lag_table(pw_i, lags, T - lags, 2)
    bb1 = jnp.concatenate([bb_r, bb_i, bb_i, bb_r], axis=-1)
    bb2 = jnp.concatenate([-bb_i, bb_r, bb_r, -bb_i], axis=-1)
    cc1 = jnp.concatenate([c_r, -c_i], axis=-1)
    cc2 = jnp.concatenate([-c_i, -c_r], axis=-1)
    ar, ai = pw_r[T], pw_i[T]
    dec = jnp.stack([jnp.concatenate([ar, ar], -1), jnp.concatenate([-ai, ai], -1)], axis=2)
    return p1, p2, bb1, bb2, q1, q2, cc1, cc2, dec


def _state_in_operator(p1_ref, p2_ref, bb1_ref, bb2_ref, d, g):
    bb1, bb2 = bb1_ref[d, g], bb2_ref[d, g]
    rows = [p1_ref[d, g, s:s + 1, :] * bb1 + p2_ref[d, g, s:s + 1, :] * bb2
            for s in range(SSM_CHUNK)]
    return jnp.concatenate(rows, axis=0).astype(BF16)


def _readout_table(q1_ref, q2_ref, cc1_ref, cc2_ref, d, g):
    cc1, cc2 = cc1_ref[d, g], cc2_ref[d, g]
    rows = [q1_ref[d, g, j:j + 1, :] * cc1 + q2_ref[d, g, j:j + 1, :] * cc2
            for j in range(SSM_CHUNK + 1)]
    return jnp.concatenate(rows, axis=0)


def _intra_operator(wf, wb, e):
    T = SSM_CHUNK
    t_of_lane = lax.broadcasted_iota(jnp.int32, wf.shape, 1) // e
    rows = []
    for s in range(T):
        f = wf if s == 0 else jnp.where(t_of_lane >= s, pltpu.roll(wf, s * e, 1), 0.0)
        b = wb if s == T - 1 else jnp.where(t_of_lane <= s, pltpu.roll(wb, (s + 1) * e, 1), 0.0)
        rows.append(f + b)
    return jnp.concatenate(rows, axis=0).astype(BF16)


def _row_pitch(n):
    p = -(-n // 8)
    return 8 * (p if p % 2 else p + 1)


def _ssm_kernel(u_ref, uc_ref, d_ref, p1_ref, p2_ref, bb1_ref, bb2_ref, q1_ref, q2_ref,
                cc1_ref, cc2_ref, dec_ref, y_ref, z_ref, s1_ref, s2_ref, hp_ref, *, batch, e):
    T = SSM_CHUNK
    gpb = dec_ref.shape[1]
    rp = z_ref.shape[1]
    n_lat, n_ctx = u_ref.shape[0] // T, uc_ref.shape[0] // T
    ncl, ncc = n_lat // batch, n_ctx // batch
    pl_, pc_ = _row_pitch(ncl), _row_pitch(ncc)
    lat0 = batch * pc_
    r = lat0 + batch * pl_
    lanes = u_ref.shape[1]
    pw = hp_ref.shape[3]

    def padded(rows, n, pitch):
        if pitch == n:
            return [rows]
        out = []
        for b in range(batch):
            out += [rows[b * n:(b + 1) * n, :], jnp.zeros((pitch - n, lanes), F32)]
        return out

    xt = []
    for s in range(T):
        parts = (padded(uc_ref[pl.ds(s, n_ctx, stride=T), :], ncc, pc_)
                 + padded(u_ref[pl.ds(s, n_lat, stride=T), :], ncl, pl_))
        if rp > r:
            parts.append(jnp.zeros((rp - r, lanes), F32))
        xt.append(jnp.concatenate(parts, axis=0).T)
    for g in range(gpb):
        zt = jnp.concatenate([xt[s][g * e:(g + 1) * e, :] for s in range(T)], axis=0)
        z_ref[g] = zt.T.astype(BF16)
    for d in range(2):
        for g in range(gpb):
            st = jnp.dot(z_ref[g], _state_in_operator(p1_ref, p2_ref, bb1_ref, bb2_ref, d, g),
                         preferred_element_type=F32)
            s1_ref[d, g] = st[:, :pw]
            s2_ref[d, g] = st[:, pw:]

    def sweep(base, count, stride, carry):
        def body(it, carry):
            out = []
            for d in range(2):
                n = it if d == 0 else count - 1 - it
                rows = pl.ds(base + n, batch, stride=stride)
                for g in range(gpb):
                    h1, h2 = carry[2 * (d * gpb + g)], carry[2 * (d * gpb + g) + 1]
                    hp_ref[d, g, rows, :] = h1
                    a1 = dec_ref[d, g, 0:1, :]
                    a2 = dec_ref[d, g, 1:2, :]
                    out.append(a1 * h1 + a2 * h2 + s1_ref[d, g, rows, :])
                    out.append(a1 * h2 - a2 * h1 + s2_ref[d, g, rows, :])
            return tuple(out)
        return lax.fori_loop(0, count, body, carry)

    for d in range(2):
        for g in range(gpb):
            for b in range(batch):
                if pc_ > ncc:
                    hp_ref[d, g, b * pc_ + ncc:(b + 1) * pc_, :] = jnp.zeros((pc_ - ncc, pw), F32)
                if pl_ > ncl:
                    hp_ref[d, g, lat0 + b * pl_ + ncl:lat0 + (b + 1) * pl_, :] = jnp.zeros((pl_ - ncl, pw), F32)
            if rp > r:
                hp_ref[d, g, r:rp, :] = jnp.zeros((rp - r, pw), F32)
    zero = jnp.zeros((batch, pw), F32)
    carry = sweep(0, ncc, pc_, tuple([zero] * (4 * gpb)))
    sweep(lat0, ncl, pl_, carry)
    nt = (((1,), (1,)), ((), ()))
    te = T * e
    yt = []
    for g in range(gpb):
        rf = _readout_table(q1_ref, q2_ref, cc1_ref, cc2_ref, 0, g)
        rb = _readout_table(q1_ref, q2_ref, cc1_ref, cc2_ref, 1, g)
        wf = lax.dot_general(bb1_ref[0, g][:, :pw], rf[:te], nt, precision=lax.Precision.HIGHEST,
                             preferred_element_type=F32)
        wb = lax.dot_general(bb1_ref[1, g][:, :pw], rb[e:], nt, precision=lax.Precision.HIGHEST,
                             preferred_element_type=F32)
        y = jnp.dot(z_ref[g], _intra_operator(wf, wb, e), preferred_element_type=F32)
        y += lax.dot_general(hp_ref[0, g].astype(BF16), rf[e:].astype(BF16), nt,
                             preferred_element_type=F32)
        y += lax.dot_general(hp_ref[1, g].astype(BF16), rb[:te].astype(BF16), nt,
                             preferred_element_type=F32)
        yt.append(y.T)
    dvec = d_ref[...]
    for t in range(T):
        blk = jnp.concatenate([yt[g][t * e:(t + 1) * e, :] for g in range(gpb)], axis=0).T
        for b in range(batch):
            rows = pl.ds(b * ncl * T + t, ncl, stride=T)
            y_ref[rows, :] = (blk[lat0 + b * pl_:lat0 + b * pl_ + ncl, :] + dvec * u_ref[rows, :])


def _s5(u, uc, d, ops, batch):
    dec = ops[-1]
    N, W = u.shape
    Nc = uc.shape[0]
    G = dec.shape[1]
    E = W // G
    gpb = LANES // E
    TE = SSM_CHUNK * E
    P2 = dec.shape[-1]
    assert P2 == LANES and W % LANES == 0

    def table(a):
        return pl.BlockSpec((2, gpb) + a.shape[2:], lambda j: (0, j, 0, 0))
    r = batch * (_row_pitch(N // SSM_CHUNK // batch) + _row_pitch(Nc // SSM_CHUNK // batch))
    rp = -(-r // LANES) * LANES
    return pl.pallas_call(
        functools.partial(_ssm_kernel, batch=batch, e=E),
        grid=(W // LANES,),
        in_specs=[pl.BlockSpec((N, LANES), lambda j: (0, j)),
                  pl.BlockSpec((Nc, LANES), lambda j: (0, j)),
                  pl.BlockSpec((1, LANES), lambda j: (0, j))] + [table(a) for a in ops],
        out_specs=pl.BlockSpec((N, LANES), lambda j: (0, j)),
        out_shape=jax.ShapeDtypeStruct((N, W), F32),
        scratch_shapes=[pltpu.VMEM((gpb, rp, TE), BF16),
                        pltpu.VMEM((2, gpb, rp, P2), F32),
                        pltpu.VMEM((2, gpb, rp, P2), F32),
                        pltpu.VMEM((2, gpb, rp, P2), F32)],
        compiler_params=_params("arbitrary"),
        name="s5",
    )(u, uc, d, *ops)


def _mixer_out_kernel(attn_ref, y_ref, ga_ref, gs_ref, x_ref, g2_ref, wglu_ref, bglu_ref,
                      wa_ref, ws_ref, wo_ref, o_ref, y2_ref, m_ref, *, sub):
    j = pl.program_id(1)
    tm, tn = o_ref.shape
    nd = m_ref.shape[1] // tn
    half = tn // 2

    @pl.when(j == 0)
    def _():
        for r in range(0, tm, sub):
            y = jax.nn.gelu(y_ref[r:r + sub, :])
            z = jnp.dot(y.astype(BF16), wglu_ref[...], preferred_element_type=F32) + bglu_ref[...]
            y2_ref[r:r + sub, :] = (y * jax.nn.sigmoid(z)).astype(BF16)

    @pl.when(j < nd)
    def _():
        for c in (0, half):
            pa = jnp.dot(attn_ref[...], wa_ref[:, c:c + half], preferred_element_type=F32)
            ps = jnp.dot(y2_ref[...], ws_ref[:, c:c + half], preferred_element_type=F32)
            merged = (jax.nn.sigmoid(ga_ref[:, c:c + half].astype(F32)) * pa
                      + jax.nn.sigmoid(gs_ref[:, c:c + half].astype(F32)) * ps)
            m_ref[:, pl.ds(pl.multiple_of(j * tn + c, half), half)] = merged.astype(BF16)

    @pl.when(j >= nd)
    def _():
        for c in (0, half):
            acc = jnp.dot(m_ref[...], wo_ref[:, c:c + half], preferred_element_type=F32)
            o_ref[:, c:c + half] = x_ref[:, c:c + half] + g2_ref[:, c:c + half] * acc


def _mixer_out(attn, y_ssm, gate, x, g2, w_glu, b_glu, wa, ws, wo, rows_per_mod):
    N, D = x.shape
    WA, WS = attn.shape[1], y_ssm.shape[1]
    tm = _tile(rows_per_mod, 1024)
    tn = _tile(D, 512, 2 * LANES)
    nd = D // tn
    tpb = rows_per_mod // tm
    lo = lambda j: jnp.minimum(j, nd - 1)
    hi = lambda j: jnp.maximum(j - nd, 0)
    return pl.pallas_call(
        functools.partial(_mixer_out_kernel, sub=_tile(tm, 512)),
        grid=(N // tm, 2 * nd),
        in_specs=[pl.BlockSpec((tm, WA), lambda i, j: (i, 0)),
                  pl.BlockSpec((tm, WS), lambda i, j: (i, 0)),
                  pl.BlockSpec((tm, tn), lambda i, j: (i, lo(j))),
                  pl.BlockSpec((tm, tn), lambda i, j: (i, nd + lo(j))),
                  pl.BlockSpec((tm, tn), lambda i, j: (i, hi(j))),
                  pl.BlockSpec((None, 1, tn), lambda i, j: (i // tpb, 0, hi(j))),
                  pl.BlockSpec((WS, WS), lambda i, j: (0, 0)),
                  pl.BlockSpec((1, WS), lambda i, j: (0, 0)),
                  pl.BlockSpec((WA, tn), lambda i, j: (0, lo(j))),
                  pl.BlockSpec((WS, tn), lambda i, j: (0, lo(j))),
                  pl.BlockSpec((D, tn), lambda i, j: (0, hi(j)))],
        out_specs=pl.BlockSpec((tm, tn), lambda i, j: (i, hi(j))),
        out_shape=jax.ShapeDtypeStruct((N, D), F32),
        scratch_shapes=[pltpu.VMEM((tm, WS), BF16), pltpu.VMEM((tm, D), BF16)],
        compiler_params=_params("arbitrary", "arbitrary"),
        name="mixer_out",
    )(attn, y_ssm, gate, gate, x, g2, w_glu, b_glu, wa, ws, wo)


def kernel(x, c, ctx, c_ctx, w_mod, b_mod, norm_g, w_ffn1_gate, w_ffn1_up, w_ffn1_down, w_in, q_norm_g, k_norm_g, ssm_a_re, ssm_a_im, ssm_log_dt, ssm_b_re, ssm_b_im, ssm_c_re, ssm_c_im, ssm_d, w_glu, b_glu, w_br_attn, w_br_ssm, w_out, w_ffn2_gate, w_ffn2_up, w_ffn2_down):
    B, L, D = x.shape
    Lc = ctx.shape[1]
    assert w_mod.shape[0] == 1, "only the single (last) layer configuration is implemented"
    hd = q_norm_g.shape[1]
    ssm_w = w_glu.shape[1]
    attn_w = w_br_attn.shape[1]
    kv_w = attn_w // Q_PER_KV
    widths = (kv_w, ssm_w, attn_w, 2 * D)
    assert L % GRID_W == 0 and L % SSM_CHUNK == 0 and Lc % SSM_CHUNK == 0 and B <= 7
    N, Nc = B * L, B * Lc
    l = 0

    cc = jnp.zeros((8, D), F32).at[:B].set(c).at[B].set(c_ctx)
    mod = _modulation(cc, w_mod[l], b_mod[l][None, :])

    def mvec(k, lo, hi):
        return mod[lo:hi, k * D:(k + 1) * D][:, None, :]

    xm = [mvec(k, 0, B) for k in range(N_MOD)]
    cm = [mvec(k, B, B + 1) for k in range(5)]
    ng = norm_g[l][:, None, :]

    x2 = x.reshape(N, D)
    c2 = ctx.reshape(Nc, D)
    w_in_b = w_in[l].astype(BF16)

    x2, wd1 = _ffn(x2, ng[0], xm[0], xm[1], xm[2], w_ffn1_gate[l], w_ffn1_up[l], w_ffn1_down[l], L)
    c2, _ = _ffn(c2, ng[0], cm[0], cm[1], cm[2], w_ffn1_gate[l], w_ffn1_up[l], wd1, Nc)

    cos, sin = _rope_tables(L, hd)
    kg = k_norm_g[l][None, :]
    qg = q_norm_g[l][None, :] * (hd ** -0.5 * math.log2(math.e))
    k, v, u, q, gate = _proj(x2, ng[1], xm[3], xm[4], w_in_b, cos, sin, kg, L, widths, True)
    kc, vc, uc = _proj(c2, ng[1], cm[3], cm[4], w_in_b, cos, sin, kg, Nc, widths, False)

    attn = _attention(q, cos, sin, qg, kc, k, vc, v, B, hd)
    ops = _ssm_operators(ssm_a_re[l], ssm_a_im[l], ssm_log_dt[l], ssm_b_re[l], ssm_b_im[l],
                         ssm_c_re[l], ssm_c_im[l])
    y_ssm = _s5(u, uc, ssm_d[l][None, :], ops, B)

    x2 = _mixer_out(attn, y_ssm, gate, x2, xm[5], w_glu[l].astype(BF16), b_glu[l][None, :],
                    w_br_attn[l].astype(BF16), w_br_ssm[l].astype(BF16), w_out[l].astype(BF16), L)

    x2, _ = _ffn(x2, ng[2], xm[6], xm[7], xm[8], w_ffn2_gate[l], w_ffn2_up[l], w_ffn2_down[l], L)
    return x2.reshape(B, L, D)
```

```python
import functools
import math

import jax
import jax.numpy as jnp
import numpy as np
from jax import lax
from jax.experimental import pallas as pl
from jax.experimental.pallas import tpu as pltpu

F32 = jnp.float32
BF16 = jnp.bfloat16

GRID_W = 64
ROPE_THETA = 10000.0
NORM_EPS = 1e-6
N_MOD = 9
Q_PER_KV = 4
SSM_CHUNK = 16
LANES = 128
VMEM_LIMIT_BYTES = 58 * 1024 * 1024


def _tile(n, pref, mult=8):
    if n <= pref:
        return n
    for t in range(pref, 0, -1):
        if n % t == 0 and t % mult == 0:
            return t
    return n


def _params(*sem):
    return pltpu.CompilerParams(dimension_semantics=sem, vmem_limit_bytes=VMEM_LIMIT_BYTES)


def _as_bf16(w):
    return w if w.dtype == BF16 else w.astype(BF16)


def _norm_mod(x, g, sh, sc):
    ms = jnp.mean(x * x, axis=-1, keepdims=True)
    h = (x * lax.rsqrt(ms + NORM_EPS)) * g
    return (h * (1.0 + sc) + sh).astype(BF16)


def _mod_kernel(c_ref, w_ref, b_ref, o_ref):
    c = c_ref[...]
    s = c * jax.nn.sigmoid(c)
    o_ref[...] = jnp.dot(s.astype(BF16), w_ref[...].astype(BF16),
                         preferred_element_type=F32) + b_ref[...]


def _modulation(cc, w_mod, b_mod):
    D, M = w_mod.shape
    tn = _tile(M, 1024, LANES)
    return pl.pallas_call(
        _mod_kernel,
        grid=(M // tn,),
        in_specs=[pl.BlockSpec((cc.shape[0], D), lambda j: (0, 0)),
                  pl.BlockSpec((D, tn), lambda j: (0, j)),
                  pl.BlockSpec((1, tn), lambda j: (0, j))],
        out_specs=pl.BlockSpec((cc.shape[0], tn), lambda j: (0, j)),
        out_shape=jax.ShapeDtypeStruct((cc.shape[0], M), F32),
        compiler_params=_params("arbitrary"),
        name="modulation",
    )(cc, w_mod, b_mod)


def _ffn_act_kernel(x_ref, ng_ref, sh_ref, sc_ref, wg_ref, wu_ref, *rest, sub):
    if len(rest) == 4:
        wd_ref, a_ref, wdb_ref, h_ref = rest

        @pl.when(pl.program_id(0) == 0)
        def _():
            wdb_ref[...] = wd_ref[...].astype(BF16)
    else:
        a_ref, h_ref = rest

    @pl.when(pl.program_id(1) == 0)
    def _():
        for r in range(0, x_ref.shape[0], sub):
            h_ref[r:r + sub, :] = _norm_mod(x_ref[r:r + sub, :], ng_ref[...], sh_ref[...], sc_ref[...])

    h = h_ref[...]
    half = a_ref.shape[1] // 2
    for c in (0, half):
        g = jnp.dot(h, _as_bf16(wg_ref[:, c:c + half]), preferred_element_type=F32)
        u = jnp.dot(h, _as_bf16(wu_ref[:, c:c + half]), preferred_element_type=F32)
        a_ref[:, c:c + half] = ((g * jax.nn.sigmoid(g)) * u).astype(a_ref.dtype)


def _side_cast_specs(arrays, nsteps, step_of):
    specs, shapes = [], []
    for a in arrays:
        rows = a.shape[0] // nsteps
        assert a.shape[0] % nsteps == 0 and rows % 16 == 0, (a.shape, nsteps)
        specs.append(pl.BlockSpec((rows, a.shape[1]), lambda *ids: (step_of(*ids), 0)))
        shapes.append(jax.ShapeDtypeStruct(a.shape, BF16))
    return specs, shapes


def _side_cast(src_refs, dst_refs):
    for src, dst in zip(src_refs, dst_refs):
        dst[...] = src[...].astype(BF16)


def _ffn_down_kernel(a_ref, x_ref, gate_ref, wd_ref, *rest):
    n_cast = len(rest) // 2
    o_ref = rest[n_cast]
    _side_cast(rest[:n_cast], rest[n_cast + 1:])
    half = o_ref.shape[1] // 2
    for c in (0, half):
        acc = jnp.dot(a_ref[...], wd_ref[:, c:c + half], preferred_element_type=F32)
        o_ref[:, c:c + half] = x_ref[:, c:c + half] + (0.5 * gate_ref[:, c:c + half]) * acc


def _ffn(x, ng, sh, sc, gate, wg, wu, wd, rows_per_mod, side_cast=()):
    N, D = x.shape
    F = wg.shape[1]
    tm = _tile(rows_per_mod, 1024)
    tf = _tile(F, 512, 2 * LANES)
    tn = _tile(D, 512, 2 * LANES)
    tpb = rows_per_mod // tm
    nj = F // tf
    vec = pl.BlockSpec((None, 1, D), lambda i, j: (i // tpb, 0, 0))
    in_specs = [pl.BlockSpec((tm, D), lambda i, j: (i, 0)),
                pl.BlockSpec((1, D), lambda i, j: (0, 0)), vec, vec,
                pl.BlockSpec((D, tf), lambda i, j: (0, j)),
                pl.BlockSpec((D, tf), lambda i, j: (0, j))]
    out_specs = [pl.BlockSpec((tm, tf), lambda i, j: (i, j))]
    out_shape = [jax.ShapeDtypeStruct((N, F), BF16)]
    operands = [x, ng, sh, sc, wg, wu]
    if wd.dtype != BF16:
        wd_rows = pl.BlockSpec((tf, D), lambda i, j: (jnp.where(i == 0, j, nj - 1), 0))
        in_specs.append(wd_rows)
        out_specs.append(wd_rows)
        out_shape.append(jax.ShapeDtypeStruct((F, D), BF16))
        operands.append(wd)
    res = pl.pallas_call(
        functools.partial(_ffn_act_kernel, sub=_tile(tm, 512)),
        grid=(N // tm, nj),
        in_specs=in_specs,
        out_specs=out_specs,
        out_shape=out_shape,
        scratch_shapes=[pltpu.VMEM((tm, D), BF16)],
        compiler_params=_params("arbitrary", "arbitrary"),
        name="ffn_act",
    )(*operands)
    a = res[0]
    if wd.dtype != BF16:
        wd = res[1]
    nd = D // tn
    cast_specs, cast_shapes = _side_cast_specs(side_cast, (N // tm) * nd, lambda i, j: i * nd + j)
    res = pl.pallas_call(
        _ffn_down_kernel,
        grid=(N // tm, nd),
        in_specs=[pl.BlockSpec((tm, F), lambda i, j: (i, 0)),
                  pl.BlockSpec((tm, tn), lambda i, j: (i, j)),
                  pl.BlockSpec((None, 1, tn), lambda i, j: (i // tpb, 0, j)),
                  pl.BlockSpec((F, tn), lambda i, j: (0, j))] + cast_specs,
        out_specs=[pl.BlockSpec((tm, tn), lambda i, j: (i, j))] + cast_specs,
        out_shape=[jax.ShapeDtypeStruct((N, D), F32)] + cast_shapes,
        compiler_params=_params("arbitrary", "arbitrary"),
        name="ffn_down",
    )(a, x, gate, wd, *side_cast)
    return res[0], wd, res[1:]


def _head_norm(acc, g, cos, sin, head_dim):
    quarter = head_dim // 4
    out = []
    for hh in range(acc.shape[1] // head_dim):
        xh = acc[:, hh * head_dim:(hh + 1) * head_dim]
        ms = jnp.mean(xh * xh, axis=-1, keepdims=True)
        xh = (xh * lax.rsqrt(ms + NORM_EPS)) * g
        if cos is not None:
            lane = lax.broadcasted_iota(jnp.int32, xh.shape, 1)
            partner = jnp.where((lane % (2 * quarter)) < quarter,
                                pltpu.roll(xh, head_dim - quarter, 1),
                                pltpu.roll(xh, quarter, 1))
            xh = xh * cos + partner * sin
        out.append(xh.astype(BF16))
    return out[0] if len(out) == 1 else jnp.concatenate(out, axis=1)


def _proj_kernel(x_ref, ng_ref, sh_ref, sc_ref, w_ref, cos_ref, sin_ref, kg_ref,
                 k_ref, v_ref, u_ref, *rest, bounds, rope, head_dim, sub):
    h_ref = rest[-1]
    j = pl.program_id(1)
    tm = x_ref.shape[0]
    jv, ju, jq, jg = bounds

    @pl.when(j == 0)
    def _():
        for r in range(0, tm, sub):
            h_ref[r:r + sub, :] = _norm_mod(x_ref[r:r + sub, :], ng_ref[...], sh_ref[...], sc_ref[...])

    acc = jnp.dot(h_ref[...], w_ref[...], preferred_element_type=F32)
    cos = cos_ref[...] if rope else None
    sin = sin_ref[...] if rope else None

    @pl.when(j < jv)
    def _():
        k_ref[...] = _head_norm(acc, kg_ref[...], cos, sin, head_dim)

    @pl.when((j >= jv) & (j < ju))
    def _():
        v_ref[...] = acc.astype(BF16)

    @pl.when((j >= ju) & (j < jq))
    def _():
        u_ref[...] = acc

    if len(rest) == 3:
        q_ref, gate_ref = rest[0], rest[1]

        @pl.when((j >= jq) & (j < jg))
        def _():
            q_ref[...] = acc

        @pl.when(j >= jg)
        def _():
            gate_ref[...] = acc.astype(BF16)


def _proj(x, ng, sh, sc, w, cos, sin, kg, rows_per_seq, widths, latent):
    N, D = x.shape
    kv_w, ssm_w, attn_w, gate_w = widths
    head_dim = kg.shape[1]
    tn = kv_w
    assert tn % LANES == 0 and ssm_w % tn == 0 and attn_w % tn == 0 and gate_w % tn == 0
    jv, ju = 1, 2
    jq = ju + ssm_w // tn
    jg = jq + attn_w // tn
    nj = jg + gate_w // tn if latent else jq
    tm = _tile(rows_per_seq, 1024)
    tpb = rows_per_seq // tm
    vec = pl.BlockSpec((None, 1, D), lambda i, j: (i // tpb, 0, 0))
    tab = pl.BlockSpec((tm, head_dim), lambda i, j: (i % tpb, 0))
    hvec = pl.BlockSpec((1, head_dim), lambda i, j: (0, 0))

    def out_spec(j0, nblk):
        return pl.BlockSpec((tm, tn), lambda i, j: (i, jnp.clip(j - j0, 0, nblk - 1)))

    out_specs = [out_spec(0, 1), out_spec(jv, 1), out_spec(ju, jq - ju)]
    out_shape = [jax.ShapeDtypeStruct((N, kv_w), BF16), jax.ShapeDtypeStruct((N, kv_w), BF16),
                 jax.ShapeDtypeStruct((N, ssm_w), F32)]
    if latent:
        out_specs += [out_spec(jq, jg - jq), out_spec(jg, nj - jg)]
        out_shape += [jax.ShapeDtypeStruct((N, attn_w), F32), jax.ShapeDtypeStruct((N, gate_w), BF16)]
    return pl.pallas_call(
        functools.partial(_proj_kernel, bounds=(jv, ju, jq, jg), rope=latent, head_dim=head_dim,
                          sub=_tile(tm, 512)),
        grid=(N // tm, nj),
        in_specs=[pl.BlockSpec((tm, D), lambda i, j: (i, 0)),
                  pl.BlockSpec((1, D), lambda i, j: (0, 0)), vec, vec,
                  pl.BlockSpec((D, tn), lambda i, j: (0, j)),
                  tab, tab, hvec],
        out_specs=out_specs,
        out_shape=out_shape,
        scratch_shapes=[pltpu.VMEM((tm, D), BF16)],
        compiler_params=_params("arbitrary", "arbitrary"),
        name="proj_latent" if latent else "proj_context",
    )(x, ng, sh, sc, w, cos, sin, kg)


def _rope_tables(L, head_dim):
    t = np.arange(L)
    row = (t // GRID_W).astype(np.float64)
    col = (t % GRID_W).astype(np.float64)
    half = head_dim // 4
    inv_freq = ROPE_THETA ** (-np.arange(half, dtype=np.float64) / half)
    ar = row[:, None] * inv_freq
    ac = col[:, None] * inv_freq
    cos = np.concatenate([np.cos(ar), np.cos(ar), np.cos(ac), np.cos(ac)], axis=-1)
    sin = np.concatenate([-np.sin(ar), np.sin(ar), -np.sin(ac), np.sin(ac)], axis=-1)
    return jnp.asarray(cos, F32), jnp.asarray(sin, F32)


def _attn_kernel(q_ref, cos_ref, sin_ref, qn_ref, cosn_ref, sinn_ref, qg_ref,
                 kc_ref, k_ref, vc_ref, v_ref, *rest, head_dim):
    n_cast = (len(rest) - 4) // 2
    o_ref = rest[n_cast]
    kall_ref, vext_ref, qs_ref = rest[-3:]
    _side_cast(rest[:n_cast], rest[n_cast + 1:2 * n_cast + 1])
    lc = kc_ref.shape[0]
    i = pl.program_id(2)

    @pl.when(i == 0)
    def _():
        kall_ref[0:lc, :] = kc_ref[...]
        kall_ref[lc:, :] = k_ref[...]
        vext_ref[0:lc, 0:head_dim] = vc_ref[...]
        vext_ref[lc:, 0:head_dim] = v_ref[...]
        vext_ref[:, head_dim:] = jnp.ones((vext_ref.shape[0], head_dim), BF16)
        qs_ref[0] = _head_norm(q_ref[...], qg_ref[...], cos_ref[...], sin_ref[...], head_dim)

    k = kall_ref[...]
    vext = vext_ref[...]
    n_rep = q_ref.shape[1] // head_dim
    heads = [slice(r * head_dim, (r + 1) * head_dim) for r in range(n_rep)]

    def scores(sl):
        return lax.dot_general(qs_ref[i % 2, :, sl], k, (((1,), (1,)), ((), ())),
                               preferred_element_type=F32)

    s_next = scores(heads[0])
    for r, sl in enumerate(heads):
        s = s_next
        if r + 1 < n_rep:
            s_next = scores(heads[r + 1])
        qs_ref[(i + 1) % 2, :, sl] = _head_norm(qn_ref[:, sl], qg_ref[...], cosn_ref[...],
                                                sinn_ref[...], head_dim)
        m = jnp.max(s, axis=-1, keepdims=True)
        p = jnp.exp2(s - m).astype(BF16)
        oe = jnp.dot(p, vext, preferred_element_type=F32)
        o_ref[:, sl] = (oe[:, :head_dim] / oe[:, head_dim:]).astype(o_ref.dtype)


def _attention(q, cos, sin, qg, kc, k, vc, v, B, head_dim, side_cast=()):
    N, W = q.shape
    L, Lc = N // B, kc.shape[0] // B
    kvh = k.shape[1] // head_dim
    gw = W // kvh
    tq = _tile(L, 512)
    tpb = L // tq
    kv_lat = pl.BlockSpec((L, head_dim), lambda b, h, i: (b, h))
    kv_ctx = pl.BlockSpec((Lc, head_dim), lambda b, h, i: (b, h))
    nxt = lambda i: jnp.minimum(i + 1, tpb - 1)
    tab = pl.BlockSpec((tq, head_dim), lambda b, h, i: (i, 0))
    tab_next = pl.BlockSpec((tq, head_dim), lambda b, h, i: (nxt(i), 0))
    cast_specs, cast_shapes = _side_cast_specs(side_cast, B * kvh * tpb,
                                               lambda b, h, i: (b * kvh + h) * tpb + i)
    res = pl.pallas_call(
        functools.partial(_attn_kernel, head_dim=head_dim),
        grid=(B, kvh, tpb),
        in_specs=[pl.BlockSpec((tq, gw), lambda b, h, i: (b * tpb + i, h)), tab, tab,
                  pl.BlockSpec((tq, gw), lambda b, h, i: (b * tpb + nxt(i), h)), tab_next, tab_next,
                  pl.BlockSpec((1, head_dim), lambda b, h, i: (0, 0)),
                  kv_ctx, kv_lat, kv_ctx, kv_lat] + cast_specs,
        out_specs=[pl.BlockSpec((tq, gw), lambda b, h, i: (b * tpb + i, h))] + cast_specs,
        out_shape=[jax.ShapeDtypeStruct((N, W), BF16)] + cast_shapes,
        scratch_shapes=[pltpu.VMEM((L + Lc, head_dim), BF16),
                        pltpu.VMEM((L + Lc, 2 * head_dim), BF16),
                        pltpu.VMEM((2, tq, gw), BF16)],
        compiler_params=_params("arbitrary", "arbitrary", "arbitrary"),
        name="attention",
    )(q, cos, sin, q, cos, sin, qg, kc, k, vc, v, *side_cast)
    return res[0], res[1:]


def _ssm_operators(a_re, a_im, log_dt, b_re, b_im, c_re, c_im):
    T = SSM_CHUNK
    a_re, a_im = a_re.astype(F32), a_im.astype(F32)
    dt = jnp.exp(log_dt.astype(F32))[..., None]
    mag = jnp.exp(a_re * dt)
    lr = mag * jnp.cos(a_im * dt)
    li = mag * jnp.sin(a_im * dt)
    den = a_re * a_re + a_im * a_im
    cr = ((lr - 1.0) * a_re + li * a_im) / den
    ci = (li * a_re - (lr - 1.0) * a_im) / den
    pr, pi = [jnp.ones_like(lr)], [jnp.zeros_like(lr)]
    for _ in range(T):
        pr.append(pr[-1] * lr - pi[-1] * li)
        pi.append(pr[-2] * li + pi[-1] * lr)
    pw_r, pw_i = jnp.stack(pr), jnp.stack(pi)
    _, G, P, E = b_re.shape
    bt_r = jnp.swapaxes(b_re.astype(F32), 2, 3)
    bt_i = jnp.swapaxes(b_im.astype(F32), 2, 3)
    bb_r = cr[:, :, None, :] * bt_r - ci[:, :, None, :] * bt_i
    bb_i = cr[:, :, None, :] * bt_i + ci[:, :, None, :] * bt_r
    c_r, c_i = c_re.astype(F32), c_im.astype(F32)

    def lag_table(pw, fwd_lags, bwd_lags, reps):
        tab = jnp.stack([jnp.moveaxis(pw[fwd_lags, 0], 0, 1), jnp.moveaxis(pw[bwd_lags, 1], 0, 1)])
        return jnp.concatenate([tab] * reps, axis=-1)

    steps = np.arange(T)
    lags = np.arange(T + 1)
    p1 = lag_table(pw_r, T - 1 - steps, steps, 4)
    p2 = lag_table(pw_i, T - 1 - steps, steps, 4)
    q1 = lag_table(pw_r, lags, T - lags, 2)
    q2 = lag_table(pw_i, lags, T - lags, 2)
    bb1 = jnp.concatenate([bb_r, bb_i, bb_i, bb_r], axis=-1)
    bb2 = jnp.concatenate([-bb_i, bb_r, bb_r, -bb_i], axis=-1)
    cc1 = jnp.concatenate([c_r, -c_i], axis=-1)
    cc2 = jnp.concatenate([-c_i, -c_r], axis=-1)
    ar, ai = pw_r[T], pw_i[T]
    dec = jnp.stack([jnp.concatenate([ar, ar], -1), jnp.concatenate([-ai, ai], -1)], axis=2)
    return p1, p2, bb1, bb2, q1, q2, cc1, cc2, dec


def _state_in_operator(p1_ref, p2_ref, bb1_ref, bb2_ref, d, g):
    bb1, bb2 = bb1_ref[d, g], bb2_ref[d, g]
    rows = [p1_ref[d, g, s:s + 1, :] * bb1 + p2_ref[d, g, s:s + 1, :] * bb2
            for s in range(SSM_CHUNK)]
    return jnp.concatenate(rows, axis=0).astype(BF16)


def _readout_table(q1_ref, q2_ref, cc1_ref, cc2_ref, d, g):
    cc1, cc2 = cc1_ref[d, g], cc2_ref[d, g]
    rows = [q1_ref[d, g, j:j + 1, :] * cc1 + q2_ref[d, g, j:j + 1, :] * cc2
            for j in range(SSM_CHUNK + 1)]
    return jnp.concatenate(rows, axis=0)


def _intra_operator(wf, wb, e):
    T = SSM_CHUNK
    t_of_lane = lax.broadcasted_iota(jnp.int32, wf.shape, 1) // e
    rows = []
    for s in range(T):
        f = wf if s == 0 else jnp.where(t_of_lane >= s, pltpu.roll(wf, s * e, 1), 0.0)
        b = wb if s == T - 1 else jnp.where(t_of_lane <= s, pltpu.roll(wb, (s + 1) * e, 1), 0.0)
        rows.append(f + b)
    return jnp.concatenate(rows, axis=0).astype(BF16)


def _row_pitch(n):
    p = -(-n // 8)
    return 8 * (p if p % 2 else p + 1)


def _ssm_kernel(u_ref, uc_ref, d_ref, p1_ref, p2_ref, bb1_ref, bb2_ref, q1_ref, q2_ref,
                cc1_ref, cc2_ref, dec_ref, y_ref, z_ref, s1_ref, s2_ref, hp_ref, *, batch, e):
    T = SSM_CHUNK
    gpb = dec_ref.shape[1]
    rp = z_ref.shape[1]
    n_lat, n_ctx = u_ref.shape[0] // T, uc_ref.shape[0] // T
    ncl, ncc = n_lat // batch, n_ctx // batch
    pl_, pc_ = _row_pitch(ncl), _row_pitch(ncc)
    lat0 = batch * pc_
    r = lat0 + batch * pl_
    lanes = u_ref.shape[1]
    pw = hp_ref.shape[3]

    def padded(rows, n, pitch):
        if pitch == n:
            return [rows]
        out = []
        for b in range(batch):
            out += [rows[b * n:(b + 1) * n, :], jnp.zeros((pitch - n, lanes), F32)]
        return out

    xt = []
    for s in range(T):
        parts = (padded(uc_ref[pl.ds(s, n_ctx, stride=T), :], ncc, pc_)
                 + padded(u_ref[pl.ds(s, n_lat, stride=T), :], ncl, pl_))
        if rp > r:
            parts.append(jnp.zeros((rp - r, lanes), F32))
        xt.append(jnp.concatenate(parts, axis=0).T)
    for g in range(gpb):
        zt = jnp.concatenate([xt[s][g * e:(g + 1) * e, :] for s in range(T)], axis=0)
        z_ref[g] = zt.T.astype(BF16)
    for d in range(2):
        for g in range(gpb):
            st = jnp.dot(z_ref[g], _state_in_operator(p1_ref, p2_ref, bb1_ref, bb2_ref, d, g),
                         preferred_element_type=F32)
            s1_ref[d, g] = st[:, :pw]
            s2_ref[d, g] = st[:, pw:]

    def sweep(base, count, stride, carry):
        def body(it, carry):
            out = []
            for d in range(2):
                n = it if d == 0 else count - 1 - it
                rows = pl.ds(base + n, batch, stride=stride)
                for g in range(gpb):
                    h1, h2 = carry[2 * (d * gpb + g)], carry[2 * (d * gpb + g) + 1]
                    hp_ref[d, g, rows, :] = h1
                    a1 = dec_ref[d, g, 0:1, :]
                    a2 = dec_ref[d, g, 1:2, :]
                    out.append(a1 * h1 + a2 * h2 + s1_ref[d, g, rows, :])
                    out.append(a1 * h2 - a2 * h1 + s2_ref[d, g, rows, :])
            return tuple(out)
        return lax.fori_loop(0, count, body, carry)

    for d in range(2):
        for g in range(gpb):
            for b in range(batch):
                if pc_ > ncc:
                    hp_ref[d, g, b * pc_ + ncc:(b + 1) * pc_, :] = jnp.zeros((pc_ - ncc, pw), F32)
                if pl_ > ncl:
                    hp_ref[d, g, lat0 + b * pl_ + ncl:lat0 + (b + 1) * pl_, :] = jnp.zeros((pl_ - ncl, pw), F32)
            if rp > r:
                hp_ref[d, g, r:rp, :] = jnp.zeros((rp - r, pw), F32)
    zero = jnp.zeros((batch, pw), F32)
    carry = sweep(0, ncc, pc_, tuple([zero] * (4 * gpb)))
    sweep(lat0, ncl, pl_, carry)
    nt = (((1,), (1,)), ((), ()))
    te = T * e
    yt = []
    for g in range(gpb):
        rf = _readout_table(q1_ref, q2_ref, cc1_ref, cc2_ref, 0, g)
        rb = _readout_table(q1_ref, q2_ref, cc1_ref, cc2_ref, 1, g)
        wf = lax.dot_general(bb1_ref[0, g][:, :pw], rf[:te], nt, precision=lax.Precision.HIGHEST,
                             preferred_element_type=F32)
        wb = lax.dot_general(bb1_ref[1, g][:, :pw], rb[e:], nt, precision=lax.Precision.HIGHEST,
                             preferred_element_type=F32)
        y = jnp.dot(z_ref[g], _intra_operator(wf, wb, e), preferred_element_type=F32)
        y += lax.dot_general(hp_ref[0, g].astype(BF16), rf[e:].astype(BF16), nt,
                             preferred_element_type=F32)
        y += lax.dot_general(hp_ref[1, g].astype(BF16), rb[:te].astype(BF16), nt,
                             preferred_element_type=F32)
        yt.append(y.T)
    dvec = d_ref[...]
    for t in range(T):
        blk = jnp.concatenate([yt[g][t * e:(t + 1) * e, :] for g in range(gpb)], axis=0).T
        for b in range(batch):
            rows = pl.ds(b * ncl * T + t, ncl, stride=T)
            y_ref[rows, :] = (blk[lat0 + b * pl_:lat0 + b * pl_ + ncl, :] + dvec * u_ref[rows, :])


def _s5(u, uc, d, ops, batch):
    dec = ops[-1]
    N, W = u.shape
    Nc = uc.shape[0]
    G = dec.shape[1]
    E = W // G
    gpb = LANES // E
    TE = SSM_CHUNK * E
    P2 = dec.shape[-1]
    assert P2 == LANES and W % LANES == 0

    def table(a):
        return pl.BlockSpec((2, gpb) + a.shape[2:], lambda j: (0, j, 0, 0))
    r = batch * (_row_pitch(N // SSM_CHUNK // batch) + _row_pitch(Nc // SSM_CHUNK // batch))
    rp = -(-r // LANES) * LANES
    return pl.pallas_call(
        functools.partial(_ssm_kernel, batch=batch, e=E),
        grid=(W // LANES,),
        in_specs=[pl.BlockSpec((N, LANES), lambda j: (0, j)),
                  pl.BlockSpec((Nc, LANES), lambda j: (0, j)),
                  pl.BlockSpec((1, LANES), lambda j: (0, j))] + [table(a) for a in ops],
        out_specs=pl.BlockSpec((N, LANES), lambda j: (0, j)),
        out_shape=jax.ShapeDtypeStruct((N, W), F32),
        scratch_shapes=[pltpu.VMEM((gpb, rp, TE), BF16),
                        pltpu.VMEM((2, gpb, rp, P2), F32),
                        pltpu.VMEM((2, gpb, rp, P2), F32),
                        pltpu.VMEM((2, gpb, rp, P2), F32)],
        compiler_params=_params("arbitrary"),
        name="s5",
    )(u, uc, d, *ops)


def _mixer_out_kernel(attn_ref, y_ref, ga_ref, gs_ref, x_ref, g2_ref, wglu_ref, bglu_ref,
                      wa_ref, ws_ref, wo_ref, o_ref, y2_ref, m_ref, *, sub):
    j = pl.program_id(1)
    tm, tn = o_ref.shape
    nd = m_ref.shape[1] // tn
    half = tn // 2

    @pl.when(j == 0)
    def _():
        for r in range(0, tm, sub):
            y = jax.nn.gelu(y_ref[r:r + sub, :])
            z = jnp.dot(y.astype(BF16), wglu_ref[...], preferred_element_type=F32) + bglu_ref[...]
            y2_ref[r:r + sub, :] = (y * jax.nn.sigmoid(z)).astype(BF16)

    @pl.when(j < nd)
    def _():
        for c in (0, half):
            pa = jnp.dot(attn_ref[...], wa_ref[:, c:c + half], preferred_element_type=F32)
            ps = jnp.dot(y2_ref[...], ws_ref[:, c:c + half], preferred_element_type=F32)
            merged = (jax.nn.sigmoid(ga_ref[:, c:c + half].astype(F32)) * pa
                      + jax.nn.sigmoid(gs_ref[:, c:c + half].astype(F32)) * ps)
            m_ref[:, pl.ds(pl.multiple_of(j * tn + c, half), half)] = merged.astype(BF16)

    @pl.when(j >= nd)
    def _():
        for c in (0, half):
            acc = jnp.dot(m_ref[...], wo_ref[:, c:c + half], preferred_element_type=F32)
            o_ref[:, c:c + half] = x_ref[:, c:c + half] + g2_ref[:, c:c + half] * acc


def _mixer_out(attn, y_ssm, gate, x, g2, w_glu, b_glu, wa, ws, wo, rows_per_mod):
    N, D = x.shape
    WA, WS = attn.shape[1], y_ssm.shape[1]
    tm = _tile(rows_per_mod, 1024)
    tn = _tile(D, 512, 2 * LANES)
    nd = D // tn
    tpb = rows_per_mod // tm
    lo = lambda j: jnp.minimum(j, nd - 1)
    hi = lambda j: jnp.maximum(j - nd, 0)
    return pl.pallas_call(
        functools.partial(_mixer_out_kernel, sub=_tile(tm, 512)),
        grid=(N // tm, 2 * nd),
        in_specs=[pl.BlockSpec((tm, WA), lambda i, j: (i, 0)),
                  pl.BlockSpec((tm, WS), lambda i, j: (i, 0)),
                  pl.BlockSpec((tm, tn), lambda i, j: (i, lo(j))),
                  pl.BlockSpec((tm, tn), lambda i, j: (i, nd + lo(j))),
                  pl.BlockSpec((tm, tn), lambda i, j: (i, hi(j))),
                  pl.BlockSpec((None, 1, tn), lambda i, j: (i // tpb, 0, hi(j))),
                  pl.BlockSpec((WS, WS), lambda i, j: (0, 0)),
                  pl.BlockSpec((1, WS), lambda i, j: (0, 0)),
                  pl.BlockSpec((WA, tn), lambda i, j: (0, lo(j))),
                  pl.BlockSpec((WS, tn), lambda i, j: (0, lo(j))),
                  pl.BlockSpec((D, tn), lambda i, j: (0, hi(j)))],
        out_specs=pl.BlockSpec((tm, tn), lambda i, j: (i, hi(j))),
        out_shape=jax.ShapeDtypeStruct((N, D), F32),
        scratch_shapes=[pltpu.VMEM((tm, WS), BF16), pltpu.VMEM((tm, D), BF16)],
        compiler_params=_params("arbitrary", "arbitrary"),
        name="mixer_out",
    )(attn, y_ssm, gate, gate, x, g2, w_glu, b_glu, wa, ws, wo)


def kernel(x, c, ctx, c_ctx, w_mod, b_mod, norm_g, w_ffn1_gate, w_ffn1_up, w_ffn1_down, w_in, q_norm_g, k_norm_g, ssm_a_re, ssm_a_im, ssm_log_dt, ssm_b_re, ssm_b_im, ssm_c_re, ssm_c_im, ssm_d, w_glu, b_glu, w_br_attn, w_br_ssm, w_out, w_ffn2_gate, w_ffn2_up, w_ffn2_down):
    B, L, D = x.shape
    Lc = ctx.shape[1]
    assert w_mod.shape[0] == 1, "only the single (last) layer configuration is implemented"
    hd = q_norm_g.shape[1]
    ssm_w = w_glu.shape[1]
    attn_w = w_br_attn.shape[1]
    kv_w = attn_w // Q_PER_KV
    widths = (kv_w, ssm_w, attn_w, 2 * D)
    assert L % GRID_W == 0 and L % SSM_CHUNK == 0 and Lc % SSM_CHUNK == 0 and B <= 7
    N, Nc = B * L, B * Lc
    l = 0

    cc = jnp.zeros((8, D), F32).at[:B].set(c).at[B].set(c_ctx)
    mod = _modulation(cc, w_mod[l], b_mod[l][None, :])

    def mvec(k, lo, hi):
        return mod[lo:hi, k * D:(k + 1) * D][:, None, :]

    xm = [mvec(k, 0, B) for k in range(N_MOD)]
    cm = [mvec(k, B, B + 1) for k in range(5)]
    ng = norm_g[l][:, None, :]

    x2 = x.reshape(N, D)
    c2 = ctx.reshape(Nc, D)

    x2, wd1, (w_in_b,) = _ffn(x2, ng[0], xm[0], xm[1], xm[2], w_ffn1_gate[l], w_ffn1_up[l],
                              w_ffn1_down[l], L, side_cast=(w_in[l],))
    c2, _, _ = _ffn(c2, ng[0], cm[0], cm[1], cm[2], w_ffn1_gate[l], w_ffn1_up[l], wd1, Nc)

    cos, sin = _rope_tables(L, hd)
    kg = k_norm_g[l][None, :]
    qg = q_norm_g[l][None, :] * (hd ** -0.5 * math.log2(math.e))
    k, v, u, q, gate = _proj(x2, ng[1], xm[3], xm[4], w_in_b, cos, sin, kg, L, widths, True)
    kc, vc, uc = _proj(c2, ng[1], cm[3], cm[4], w_in_b, cos, sin, kg, Nc, widths, False)

    attn, _ = _attention(q, cos, sin, qg, kc, k, vc, v, B, hd)
    ops = _ssm_operators(ssm_a_re[l], ssm_a_im[l], ssm_log_dt[l], ssm_b_re[l], ssm_b_im[l],
                         ssm_c_re[l], ssm_c_im[l])
    y_ssm = _s5(u, uc, ssm_d[l][None, :], ops, B)

    x2 = _mixer_out(attn, y_ssm, gate, x2, xm[5], w_glu[l].astype(BF16), b_glu[l][None, :],
                    w_br_attn[l].astype(BF16), w_br_ssm[l].astype(BF16), w_out[l].astype(BF16), L)

    x2, _, _ = _ffn(x2, ng[2], xm[6], xm[7], xm[8], w_ffn2_gate[l], w_ffn2_up[l],
                    w_ffn2_down[l], L)
    return x2.reshape(B, L, D)
```

```python
import functools
import math

import jax
import jax.numpy as jnp
import numpy as np
from jax import lax
from jax.experimental import pallas as pl
from jax.experimental.pallas import tpu as pltpu

F32 = jnp.float32
BF16 = jnp.bfloat16

GRID_W = 64
ROPE_THETA = 10000.0
NORM_EPS = 1e-6
N_MOD = 9
Q_PER_KV = 4
SSM_CHUNK = 16
LANES = 128
VMEM_LIMIT_BYTES = 58 * 1024 * 1024


def _tile(n, pref, mult=8):
    if n <= pref:
        return n
    for t in range(pref, 0, -1):
        if n % t == 0 and t % mult == 0:
            return t
    return n


def _params(*sem):
    return pltpu.CompilerParams(dimension_semantics=sem, vmem_limit_bytes=VMEM_LIMIT_BYTES)


def _as_bf16(w):
    return w if w.dtype == BF16 else w.astype(BF16)


def _norm_mod(x, g, sh, sc):
    ms = jnp.mean(x * x, axis=-1, keepdims=True)
    h = (x * lax.rsqrt(ms + NORM_EPS)) * g
    return (h * (1.0 + sc) + sh).astype(BF16)


def _mod_kernel(c_ref, w_ref, b_ref, o_ref):
    c = c_ref[...]
    s = c * jax.nn.sigmoid(c)
    o_ref[...] = jnp.dot(s.astype(BF16), w_ref[...].astype(BF16),
                         preferred_element_type=F32) + b_ref[...]


def _modulation(cc, w_mod, b_mod):
    D, M = w_mod.shape
    tn = _tile(M, 1024, LANES)
    return pl.pallas_call(
        _mod_kernel,
        grid=(M // tn,),
        in_specs=[pl.BlockSpec((cc.shape[0], D), lambda j: (0, 0)),
                  pl.BlockSpec((D, tn), lambda j: (0, j)),
                  pl.BlockSpec((1, tn), lambda j: (0, j))],
        out_specs=pl.BlockSpec((cc.shape[0], tn), lambda j: (0, j)),
        out_shape=jax.ShapeDtypeStruct((cc.shape[0], M), F32),
        compiler_params=_params("arbitrary"),
        name="modulation",
    )(cc, w_mod, b_mod)


def _ffn_act_kernel(x_ref, ng_ref, sh_ref, sc_ref, wg_ref, wu_ref, *rest, sub):
    if len(rest) == 4:
        wd_ref, a_ref, wdb_ref, h_ref = rest

        @pl.when(pl.program_id(0) == 0)
        def _():
            wdb_ref[...] = wd_ref[...].astype(BF16)
    else:
        a_ref, h_ref = rest

    half = a_ref.shape[1] // 2
    j = pl.program_id(1)

    def emit(row_blocks):
        halves = [(c, _as_bf16(wg_ref[:, c:c + half]), _as_bf16(wu_ref[:, c:c + half]))
                  for c in (0, half)]
        for rows, make_h in row_blocks:
            h = make_h()
            for c, wg, wu in halves:
                g = jnp.dot(h, wg, preferred_element_type=F32)
                u = jnp.dot(h, wu, preferred_element_type=F32)
                a_ref[rows, c:c + half] = ((g * jax.nn.sigmoid(g)) * u).astype(a_ref.dtype)

    def normed(rows):
        def make_h():
            h = _norm_mod(x_ref[rows, :], ng_ref[...], sh_ref[...], sc_ref[...])
            h_ref[rows, :] = h
            return h
        return make_h

    @pl.when(j == 0)
    def _():
        emit([(slice(r, r + sub), normed(slice(r, r + sub))) for r in range(0, x_ref.shape[0], sub)])

    @pl.when(j > 0)
    def _():
        emit([(slice(None), lambda: h_ref[...])])


def _side_cast_specs(arrays, nsteps, step_of):
    specs, shapes = [], []
    for a in arrays:
        rows = a.shape[0] // nsteps
        assert a.shape[0] % nsteps == 0 and rows % 16 == 0, (a.shape, nsteps)
        specs.append(pl.BlockSpec((rows, a.shape[1]), lambda *ids: (step_of(*ids), 0)))
        shapes.append(jax.ShapeDtypeStruct(a.shape, BF16))
    return specs, shapes


def _side_cast(src_refs, dst_refs):
    for src, dst in zip(src_refs, dst_refs):
        dst[...] = src[...].astype(BF16)


def _ffn_down_kernel(a_ref, x_ref, gate_ref, wd_ref, *rest):
    n_cast = len(rest) // 2
    o_ref = rest[n_cast]
    _side_cast(rest[:n_cast], rest[n_cast + 1:])
    half = o_ref.shape[1] // 2
    for c in (0, half):
        acc = jnp.dot(a_ref[...], wd_ref[:, c:c + half], preferred_element_type=F32)
        o_ref[:, c:c + half] = x_ref[:, c:c + half] + (0.5 * gate_ref[:, c:c + half]) * acc


def _ffn(x, ng, sh, sc, gate, wg, wu, wd, rows_per_mod, side_cast=()):
    N, D = x.shape
    F = wg.shape[1]
    tm = _tile(rows_per_mod, 1024)
    tf = _tile(F, 512, 2 * LANES)
    tn = _tile(D, 512, 2 * LANES)
    tpb = rows_per_mod // tm
    nj = F // tf
    vec = pl.BlockSpec((None, 1, D), lambda i, j: (i // tpb, 0, 0))
    in_specs = [pl.BlockSpec((tm, D), lambda i, j: (i, 0)),
                pl.BlockSpec((1, D), lambda i, j: (0, 0)), vec, vec,
                pl.BlockSpec((D, tf), lambda i, j: (0, j)),
                pl.BlockSpec((D, tf), lambda i, j: (0, j))]
    out_specs = [pl.BlockSpec((tm, tf), lambda i, j: (i, j))]
    out_shape = [jax.ShapeDtypeStruct((N, F), BF16)]
    operands = [x, ng, sh, sc, wg, wu]
    if wd.dtype != BF16:
        wd_rows = pl.BlockSpec((tf, D), lambda i, j: (jnp.where(i == 0, j, nj - 1), 0))
        in_specs.append(wd_rows)
        out_specs.append(wd_rows)
        out_shape.append(jax.ShapeDtypeStruct((F, D), BF16))
        operands.append(wd)
    res = pl.pallas_call(
        functools.partial(_ffn_act_kernel, sub=_tile(tm, 512)),
        grid=(N // tm, nj),
        in_specs=in_specs,
        out_specs=out_specs,
        out_shape=out_shape,
        scratch_shapes=[pltpu.VMEM((tm, D), BF16)],
        compiler_params=_params("arbitrary", "arbitrary"),
        name="ffn_act",
    )(*operands)
    a = res[0]
    if wd.dtype != BF16:
        wd = res[1]
    nd = D // tn
    cast_specs, cast_shapes = _side_cast_specs(side_cast, (N // tm) * nd, lambda i, j: i * nd + j)
    res = pl.pallas_call(
        _ffn_down_kernel,
        grid=(N // tm, nd),
        in_specs=[pl.BlockSpec((tm, F), lambda i, j: (i, 0)),
                  pl.BlockSpec((tm, tn), lambda i, j: (i, j)),
                  pl.BlockSpec((None, 1, tn), lambda i, j: (i // tpb, 0, j)),
                  pl.BlockSpec((F, tn), lambda i, j: (0, j))] + cast_specs,
        out_specs=[pl.BlockSpec((tm, tn), lambda i, j: (i, j))] + cast_specs,
        out_shape=[jax.ShapeDtypeStruct((N, D), F32)] + cast_shapes,
        compiler_params=_params("arbitrary", "arbitrary"),
        name="ffn_down",
    )(a, x, gate, wd, *side_cast)
    return res[0], wd, res[1:]


def _head_norm(acc, g, cos, sin, head_dim):
    quarter = head_dim // 4
    out = []
    for hh in range(acc.shape[1] // head_dim):
        xh = acc[:, hh * head_dim:(hh + 1) * head_dim]
        ms = jnp.mean(xh * xh, axis=-1, keepdims=True)
        xh = (xh * lax.rsqrt(ms + NORM_EPS)) * g
        if cos is not None:
            lane = lax.broadcasted_iota(jnp.int32, xh.shape, 1)
            partner = jnp.where((lane % (2 * quarter)) < quarter,
                                pltpu.roll(xh, head_dim - quarter, 1),
                                pltpu.roll(xh, quarter, 1))
            xh = xh * cos + partner * sin
        out.append(xh.astype(BF16))
    return out[0] if len(out) == 1 else jnp.concatenate(out, axis=1)


def _proj_kernel(x_ref, ng_ref, sh_ref, sc_ref, w_ref, cos_ref, sin_ref, kg_ref,
                 k_ref, v_ref, u_ref, *rest, bounds, rope, head_dim, sub):
    h_ref = rest[-1]
    j = pl.program_id(1)
    tm = x_ref.shape[0]
    jv, ju, jq, jg = bounds
    assert jv == 1

    @pl.when(j == 0)
    def _():
        for r in range(0, tm, sub):
            rows = slice(r, r + sub)
            h = _norm_mod(x_ref[rows, :], ng_ref[...], sh_ref[...], sc_ref[...])
            h_ref[rows, :] = h
            acc = jnp.dot(h, w_ref[...], preferred_element_type=F32)
            cos = cos_ref[rows, :] if rope else None
            sin = sin_ref[rows, :] if rope else None
            k_ref[rows, :] = _head_norm(acc, kg_ref[...], cos, sin, head_dim)

    @pl.when(j > 0)
    def _():
        acc = jnp.dot(h_ref[...], w_ref[...], preferred_element_type=F32)

        @pl.when(j < ju)
        def _():
            v_ref[...] = acc.astype(BF16)

        @pl.when((j >= ju) & (j < jq))
        def _():
            u_ref[...] = acc

        if len(rest) == 3:
            q_ref, gate_ref = rest[0], rest[1]

            @pl.when((j >= jq) & (j < jg))
            def _():
                q_ref[...] = acc

            @pl.when(j >= jg)
            def _():
                gate_ref[...] = acc.astype(BF16)


def _proj(x, ng, sh, sc, w, cos, sin, kg, rows_per_seq, widths, latent):
    N, D = x.shape
    kv_w, ssm_w, attn_w, gate_w = widths
    head_dim = kg.shape[1]
    tn = kv_w
    assert tn % LANES == 0 and ssm_w % tn == 0 and attn_w % tn == 0 and gate_w % tn == 0
    jv, ju = 1, 2
    jq = ju + ssm_w // tn
    jg = jq + attn_w // tn
    nj = jg + gate_w // tn if latent else jq
    tm = _tile(rows_per_seq, 1024)
    tpb = rows_per_seq // tm
    vec = pl.BlockSpec((None, 1, D), lambda i, j: (i // tpb, 0, 0))
    tab = pl.BlockSpec((tm, head_dim), lambda i, j: (i % tpb, 0))
    hvec = pl.BlockSpec((1, head_dim), lambda i, j: (0, 0))

    def out_spec(j0, nblk):
        return pl.BlockSpec((tm, tn), lambda i, j: (i, jnp.clip(j - j0, 0, nblk - 1)))

    out_specs = [out_spec(0, 1), out_spec(jv, 1), out_spec(ju, jq - ju)]
    out_shape = [jax.ShapeDtypeStruct((N, kv_w), BF16), jax.ShapeDtypeStruct((N, kv_w), BF16),
                 jax.ShapeDtypeStruct((N, ssm_w), F32)]
    if latent:
        out_specs += [out_spec(jq, jg - jq), out_spec(jg, nj - jg)]
        out_shape += [jax.ShapeDtypeStruct((N, attn_w), F32), jax.ShapeDtypeStruct((N, gate_w), BF16)]
    return pl.pallas_call(
        functools.partial(_proj_kernel, bounds=(jv, ju, jq, jg), rope=latent, head_dim=head_dim,
                          sub=_tile(tm, 512)),
        grid=(N // tm, nj),
        in_specs=[pl.BlockSpec((tm, D), lambda i, j: (i, 0)),
                  pl.BlockSpec((1, D), lambda i, j: (0, 0)), vec, vec,
                  pl.BlockSpec((D, tn), lambda i, j: (0, j)),
                  tab, tab, hvec],
        out_specs=out_specs,
        out_shape=out_shape,
        scratch_shapes=[pltpu.VMEM((tm, D), BF16)],
        compiler_params=_params("arbitrary", "arbitrary"),
        name="proj_latent" if latent else "proj_context",
    )(x, ng, sh, sc, w, cos, sin, kg)


def _rope_tables(L, head_dim):
    t = np.arange(L)
    row = (t // GRID_W).astype(np.float64)
    col = (t % GRID_W).astype(np.float64)
    half = head_dim // 4
    inv_freq = ROPE_THETA ** (-np.arange(half, dtype=np.float64) / half)
    ar = row[:, None] * inv_freq
    ac = col[:, None] * inv_freq
    cos = np.concatenate([np.cos(ar), np.cos(ar), np.cos(ac), np.cos(ac)], axis=-1)
    sin = np.concatenate([-np.sin(ar), np.sin(ar), -np.sin(ac), np.sin(ac)], axis=-1)
    return jnp.asarray(cos, F32), jnp.asarray(sin, F32)


def _attn_kernel(q_ref, cos_ref, sin_ref, qn_ref, cosn_ref, sinn_ref, qg_ref,
                 kc_ref, k_ref, vc_ref, v_ref, *rest, head_dim):
    n_cast = (len(rest) - 4) // 2
    o_ref = rest[n_cast]
    kall_ref, vext_ref, qs_ref = rest[-3:]
    _side_cast(rest[:n_cast], rest[n_cast + 1:2 * n_cast + 1])
    lc = kc_ref.shape[0]
    i = pl.program_id(2)

    @pl.when(i == 0)
    def _():
        kall_ref[0:lc, :] = kc_ref[...]
        kall_ref[lc:, :] = k_ref[...]
        vext_ref[0:lc, 0:head_dim] = vc_ref[...]
        vext_ref[lc:, 0:head_dim] = v_ref[...]
        vext_ref[:, head_dim:] = jnp.ones((vext_ref.shape[0], head_dim), BF16)
        qs_ref[0] = _head_norm(q_ref[...], qg_ref[...], cos_ref[...], sin_ref[...], head_dim)

    k = kall_ref[...]
    vext = vext_ref[...]
    n_rep = q_ref.shape[1] // head_dim
    heads = [slice(r * head_dim, (r + 1) * head_dim) for r in range(n_rep)]

    def scores(sl):
        return lax.dot_general(qs_ref[i % 2, :, sl], k, (((1,), (1,)), ((), ())),
                               preferred_element_type=F32)

    s_next = scores(heads[0])
    for r, sl in enumerate(heads):
        s = s_next
        if r + 1 < n_rep:
            s_next = scores(heads[r + 1])
        qs_ref[(i + 1) % 2, :, sl] = _head_norm(qn_ref[:, sl], qg_ref[...], cosn_ref[...],
                                                sinn_ref[...], head_dim)
        m = jnp.max(s, axis=-1, keepdims=True)
        p = jnp.exp2(s - m).astype(BF16)
        oe = jnp.dot(p, vext, preferred_element_type=F32)
        o_ref[:, sl] = (oe[:, :head_dim] / oe[:, head_dim:]).astype(o_ref.dtype)


def _attention(q, cos, sin, qg, kc, k, vc, v, B, head_dim, side_cast=()):
    N, W = q.shape
    L, Lc = N // B, kc.shape[0] // B
    kvh = k.shape[1] // head_dim
    gw = W // kvh
    tq = _tile(L, 512)
    tpb = L // tq
    kv_lat = pl.BlockSpec((L, head_dim), lambda b, h, i: (b, h))
    kv_ctx = pl.BlockSpec((Lc, head_dim), lambda b, h, i: (b, h))
    nxt = lambda i: jnp.minimum(i + 1, tpb - 1)
    tab = pl.BlockSpec((tq, head_dim), lambda b, h, i: (i, 0))
    tab_next = pl.BlockSpec((tq, head_dim), lambda b, h, i: (nxt(i), 0))
    cast_specs, cast_shapes = _side_cast_specs(side_cast, B * kvh * tpb,
                                               lambda b, h, i: (b * kvh + h) * tpb + i)
    res = pl.pallas_call(
        functools.partial(_attn_kernel, head_dim=head_dim),
        grid=(B, kvh, tpb),
        in_specs=[pl.BlockSpec((tq, gw), lambda b, h, i: (b * tpb + i, h)), tab, tab,
                  pl.BlockSpec((tq, gw), lambda b, h, i: (b * tpb + nxt(i), h)), tab_next, tab_next,
                  pl.BlockSpec((1, head_dim), lambda b, h, i: (0, 0)),
                  kv_ctx, kv_lat, kv_ctx, kv_lat] + cast_specs,
        out_specs=[pl.BlockSpec((tq, gw), lambda b, h, i: (b * tpb + i, h))] + cast_specs,
        out_shape=[jax.ShapeDtypeStruct((N, W), BF16)] + cast_shapes,
        scratch_shapes=[pltpu.VMEM((L + Lc, head_dim), BF16),
                        pltpu.VMEM((L + Lc, 2 * head_dim), BF16),
                        pltpu.VMEM((2, tq, gw), BF16)],
        compiler_params=_params("arbitrary", "arbitrary", "arbitrary"),
        name="attention",
    )(q, cos, sin, q, cos, sin, qg, kc, k, vc, v, *side_cast)
    return res[0], res[1:]


def _ssm_operators(a_re, a_im, log_dt, b_re, b_im, c_re, c_im):
    T = SSM_CHUNK
    a_re, a_im = a_re.astype(F32), a_im.astype(F32)
    dt = jnp.exp(log_dt.astype(F32))[..., None]
    mag = jnp.exp(a_re * dt)
    lr = mag * jnp.cos(a_im * dt)
    li = mag * jnp.sin(a_im * dt)
    den = a_re * a_re + a_im * a_im
    cr = ((lr - 1.0) * a_re + li * a_im) / den
    ci = (li * a_re - (lr - 1.0) * a_im) / den
    pr, pi = [jnp.ones_like(lr)], [jnp.zeros_like(lr)]
    for _ in range(T):
        pr.append(pr[-1] * lr - pi[-1] * li)
        pi.append(pr[-2] * li + pi[-1] * lr)
    pw_r, pw_i = jnp.stack(pr), jnp.stack(pi)
    _, G, P, E = b_re.shape
    bt_r = jnp.swapaxes(b_re.astype(F32), 2, 3)
    bt_i = jnp.swapaxes(b_im.astype(F32), 2, 3)
    bb_r = cr[:, :, None, :] * bt_r - ci[:, :, None, :] * bt_i
    bb_i = cr[:, :, None, :] * bt_i + ci[:, :, None, :] * bt_r
    c_r, c_i = c_re.astype(F32), c_im.astype(F32)

    def lag_table(pw, fwd_lags, bwd_lags, reps):
        tab = jnp.stack([jnp.moveaxis(pw[fwd_lags, 0], 0, 1), jnp.moveaxis(pw[bwd_lags, 1], 0, 1)])
        return jnp.concatenate([tab] * reps, axis=-1)

    steps = np.arange(T)
    lags = np.arange(T + 1)
    p1 = lag_table(pw_r, T - 1 - steps, steps, 4)
    p2 = lag_table(pw_i, T - 1 - steps, steps, 4)
    q1 = lag_table(pw_r, lags, T - lags, 2)
    q2 = lag_table(pw_i, lags, T - lags, 2)
    bb1 = jnp.concatenate([bb_r, bb_i, bb_i, bb_r], axis=-1)
    bb2 = jnp.concatenate([-bb_i, bb_r, bb_r, -bb_i], axis=-1)
    cc1 = jnp.concatenate([c_r, -c_i], axis=-1)
    cc2 = jnp.concatenate([-c_i, -c_r], axis=-1)
    ar, ai = pw_r[T], pw_i[T]
    dec = jnp.stack([jnp.concatenate([ar, ar], -1), jnp.concatenate([-ai, ai], -1)], axis=2)
    return p1, p2, bb1, bb2, q1, q2, cc1, cc2, dec


def _state_in_operator(p1_ref, p2_ref, bb1_ref, bb2_ref, d, g):
    bb1, bb2 = bb1_ref[d, g], bb2_ref[d, g]
    rows = [p1_ref[d, g, s:s + 1, :] * bb1 + p2_ref[d, g, s:s + 1, :] * bb2
            for s in range(SSM_CHUNK)]
    return jnp.concatenate(rows, axis=0).astype(BF16)


def _readout_table(q1_ref, q2_ref, cc1_ref, cc2_ref, d, g):
    cc1, cc2 = cc1_ref[d, g], cc2_ref[d, g]
    rows = [q1_ref[d, g, j:j + 1, :] * cc1 + q2_ref[d, g, j:j + 1, :] * cc2
            for j in range(SSM_CHUNK + 1)]
    return jnp.concatenate(rows, axis=0)


def _intra_operator(wf, wb, e):
    T = SSM_CHUNK
    t_of_lane = lax.broadcasted_iota(jnp.int32, wf.shape, 1) // e
    rows = []
    for s in range(T):
        f = wf if s == 0 else jnp.where(t_of_lane >= s, pltpu.roll(wf, s * e, 1), 0.0)
        b = wb if s == T - 1 else jnp.where(t_of_lane <= s, pltpu.roll(wb, (s + 1) * e, 1), 0.0)
        rows.append(f + b)
    return jnp.concatenate(rows, axis=0).astype(BF16)


def _row_pitch(n):
    p = -(-n // 8)
    return 8 * (p if p % 2 else p + 1)


def _ssm_kernel(u_ref, uc_ref, d_ref, p1_ref, p2_ref, bb1_ref, bb2_ref, q1_ref, q2_ref,
                cc1_ref, cc2_ref, dec_ref, y_ref, z_ref, s1_ref, s2_ref, hp_ref, *, batch, e):
    T = SSM_CHUNK
    gpb = dec_ref.shape[1]
    rp = z_ref.shape[1]
    n_lat, n_ctx = u_ref.shape[0] // T, uc_ref.shape[0] // T
    ncl, ncc = n_lat // batch, n_ctx // batch
    pl_, pc_ = _row_pitch(ncl), _row_pitch(ncc)
    lat0 = batch * pc_
    r = lat0 + batch * pl_
    lanes = u_ref.shape[1]
    pw = hp_ref.shape[3]

    def padded(rows, n, pitch):
        if pitch == n:
            return [rows]
        out = []
        for b in range(batch):
            out += [rows[b * n:(b + 1) * n, :], jnp.zeros((pitch - n, lanes), F32)]
        return out

    xt = []
    for s in range(T):
        parts = (padded(uc_ref[pl.ds(s, n_ctx, stride=T), :], ncc, pc_)
                 + padded(u_ref[pl.ds(s, n_lat, stride=T), :], ncl, pl_))
        if rp > r:
            parts.append(jnp.zeros((rp - r, lanes), F32))
        xt.append(jnp.concatenate(parts, axis=0).T)
    for g in range(gpb):
        zt = jnp.concatenate([xt[s][g * e:(g + 1) * e, :] for s in range(T)], axis=0)
        z_ref[g] = zt.T.astype(BF16)
    for d in range(2):
        for g in range(gpb):
            st = jnp.dot(z_ref[g], _state_in_operator(p1_ref, p2_ref, bb1_ref, bb2_ref, d, g),
                         preferred_element_type=F32)
            s1_ref[d, g] = st[:, :pw]
            s2_ref[d, g] = st[:, pw:]

    def sweep(base, count, stride, carry):
        def body(it, carry):
            out = []
            for d in range(2):
                n = it if d == 0 else count - 1 - it
                rows = pl.ds(base + n, batch, stride=stride)
                for g in range(gpb):
                    h1, h2 = carry[2 * (d * gpb + g)], carry[2 * (d * gpb + g) + 1]
                    hp_ref[d, g, rows, :] = h1
                    a1 = dec_ref[d, g, 0:1, :]
                    a2 = dec_ref[d, g, 1:2, :]
                    out.append(a1 * h1 + a2 * h2 + s1_ref[d, g, rows, :])
                    out.append(a1 * h2 - a2 * h1 + s2_ref[d, g, rows, :])
            return tuple(out)
        return lax.fori_loop(0, count, body, carry)

    for d in range(2):
        for g in range(gpb):
            for b in range(batch):
                if pc_ > ncc:
                    hp_ref[d, g, b * pc_ + ncc:(b + 1) * pc_, :] = jnp.zeros((pc_ - ncc, pw), F32)
                if pl_ > ncl:
                    hp_ref[d, g, lat0 + b * pl_ + ncl:lat0 + (b + 1) * pl_, :] = jnp.zeros((pl_ - ncl, pw), F32)
            if rp > r:
                hp_ref[d, g, r:rp, :] = jnp.zeros((rp - r, pw), F32)
    zero = jnp.zeros((batch, pw), F32)
    carry = sweep(0, ncc, pc_, tuple([zero] * (4 * gpb)))
    sweep(lat0, ncl, pl_, carry)
    nt = (((1,), (1,)), ((), ()))
    te = T * e
    yt = []
    for g in range(gpb):
        rf = _readout_table(q1_ref, q2_ref, cc1_ref, cc2_ref, 0, g)
        rb = _readout_table(q1_ref, q2_ref, cc1_ref, cc2_ref, 1, g)
        wf = lax.dot_general(bb1_ref[0, g][:, :pw], rf[:te], nt, precision=lax.Precision.HIGHEST,
                             preferred_element_type=F32)
        wb = lax.dot_general(bb1_ref[1, g][:, :pw], rb[e:], nt, precision=lax.Precision.HIGHEST,
                             preferred_element_type=F32)
        y = jnp.dot(z_ref[g], _intra_operator(wf, wb, e), preferred_element_type=F32)
        y += lax.dot_general(hp_ref[0, g].astype(BF16), rf[e:].astype(BF16), nt,
                             preferred_element_type=F32)
        y += lax.dot_general(hp_ref[1, g].astype(BF16), rb[:te].astype(BF16), nt,
                             preferred_element_type=F32)
        yt.append(y.T)
    dvec = d_ref[...]
    for t in range(T):
        blk = jnp.concatenate([yt[g][t * e:(t + 1) * e, :] for g in range(gpb)], axis=0).T
        for b in range(batch):
            rows = pl.ds(b * ncl * T + t, ncl, stride=T)
            y_ref[rows, :] = (blk[lat0 + b * pl_:lat0 + b * pl_ + ncl, :] + dvec * u_ref[rows, :])


def _s5(u, uc, d, ops, batch):
    dec = ops[-1]
    N, W = u.shape
    Nc = uc.shape[0]
    G = dec.shape[1]
    E = W // G
    gpb = LANES // E
    TE = SSM_CHUNK * E
    P2 = dec.shape[-1]
    assert P2 == LANES and W % LANES == 0

    def table(a):
        return pl.BlockSpec((2, gpb) + a.shape[2:], lambda j: (0, j, 0, 0))
    r = batch * (_row_pitch(N // SSM_CHUNK // batch) + _row_pitch(Nc // SSM_CHUNK // batch))
    rp = -(-r // LANES) * LANES
    return pl.pallas_call(
        functools.partial(_ssm_kernel, batch=batch, e=E),
        grid=(W // LANES,),
        in_specs=[pl.BlockSpec((N, LANES), lambda j: (0, j)),
                  pl.BlockSpec((Nc, LANES), lambda j: (0, j)),
                  pl.BlockSpec((1, LANES), lambda j: (0, j))] + [table(a) for a in ops],
        out_specs=pl.BlockSpec((N, LANES), lambda j: (0, j)),
        out_shape=jax.ShapeDtypeStruct((N, W), F32),
        scratch_shapes=[pltpu.VMEM((gpb, rp, TE), BF16),
                        pltpu.VMEM((2, gpb, rp, P2), F32),
                        pltpu.VMEM((2, gpb, rp, P2), F32),
                        pltpu.VMEM((2, gpb, rp, P2), F32)],
        compiler_params=_params("arbitrary"),
        name="s5",
    )(u, uc, d, *ops)


def _mixer_out_kernel(attn_ref, y_ref, ga_ref, gs_ref, x_ref, g2_ref, wglu_ref, bglu_ref,
                      wa_ref, ws_ref, wo_ref, o_ref, y2_ref, m_ref, *, sub):
    j = pl.program_id(1)
    tm, tn = o_ref.shape
    nd = m_ref.shape[1] // tn
    half = tn // 2

    def merge(rows, attn, y2):
        for c in (0, half):
            pa = jnp.dot(attn, wa_ref[:, c:c + half], preferred_element_type=F32)
            ps = jnp.dot(y2, ws_ref[:, c:c + half], preferred_element_type=F32)
            merged = (jax.nn.sigmoid(ga_ref[rows, c:c + half].astype(F32)) * pa
                      + jax.nn.sigmoid(gs_ref[rows, c:c + half].astype(F32)) * ps)
            m_ref[rows, pl.ds(pl.multiple_of(j * tn + c, half), half)] = merged.astype(BF16)

    @pl.when(j == 0)
    def _():
        for r in range(0, tm, sub):
            rows = slice(r, r + sub)
            y = jax.nn.gelu(y_ref[rows, :])
            z = jnp.dot(y.astype(BF16), wglu_ref[...], preferred_element_type=F32) + bglu_ref[...]
            y2 = (y * jax.nn.sigmoid(z)).astype(BF16)
            y2_ref[rows, :] = y2
            merge(rows, attn_ref[rows, :], y2)

    @pl.when((j > 0) & (j < nd))
    def _():
        merge(slice(None), attn_ref[...], y2_ref[...])

    @pl.when(j >= nd)
    def _():
        for c in (0, half):
            acc = jnp.dot(m_ref[...], wo_ref[:, c:c + half], preferred_element_type=F32)
            o_ref[:, c:c + half] = x_ref[:, c:c + half] + g2_ref[:, c:c + half] * acc


def _mixer_out(attn, y_ssm, gate, x, g2, w_glu, b_glu, wa, ws, wo, rows_per_mod):
    N, D = x.shape
    WA, WS = attn.shape[1], y_ssm.shape[1]
    tm = _tile(rows_per_mod, 1024)
    tn = _tile(D, 512, 2 * LANES)
    nd = D // tn
    tpb = rows_per_mod // tm
    lo = lambda j: jnp.minimum(j, nd - 1)
    hi = lambda j: jnp.maximum(j - nd, 0)
    return pl.pallas_call(
        functools.partial(_mixer_out_kernel, sub=_tile(tm, 512)),
        grid=(N // tm, 2 * nd),
        in_specs=[pl.BlockSpec((tm, WA), lambda i, j: (i, 0)),
                  pl.BlockSpec((tm, WS), lambda i, j: (i, 0)),
                  pl.BlockSpec((tm, tn), lambda i, j: (i, lo(j))),
                  pl.BlockSpec((tm, tn), lambda i, j: (i, nd + lo(j))),
                  pl.BlockSpec((tm, tn), lambda i, j: (i, hi(j))),
                  pl.BlockSpec((None, 1, tn), lambda i, j: (i // tpb, 0, hi(j))),
                  pl.BlockSpec((WS, WS), lambda i, j: (0, 0)),
                  pl.BlockSpec((1, WS), lambda i, j: (0, 0)),
                  pl.BlockSpec((WA, tn), lambda i, j: (0, lo(j))),
                  pl.BlockSpec((WS, tn), lambda i, j: (0, lo(j))),
                  pl.BlockSpec((D, tn), lambda i, j: (0, hi(j)))],
        out_specs=pl.BlockSpec((tm, tn), lambda i, j: (i, hi(j))),
        out_shape=jax.ShapeDtypeStruct((N, D), F32),
        scratch_shapes=[pltpu.VMEM((tm, WS), BF16), pltpu.VMEM((tm, D), BF16)],
        compiler_params=_params("arbitrary", "arbitrary"),
        name="mixer_out",
    )(attn, y_ssm, gate, gate, x, g2, w_glu, b_glu, wa, ws, wo)


def kernel(x, c, ctx, c_ctx, w_mod, b_mod, norm_g, w_ffn1_gate, w_ffn1_up, w_ffn1_down, w_in, q_norm_g, k_norm_g, ssm_a_re, ssm_a_im, ssm_log_dt, ssm_b_re, ssm_b_im, ssm_c_re, ssm_c_im, ssm_d, w_glu, b_glu, w_br_attn, w_br_ssm, w_out, w_ffn2_gate, w_ffn2_up, w_ffn2_down):
    B, L, D = x.shape
    Lc = ctx.shape[1]
    assert w_mod.shape[0] == 1, "only the single (last) layer configuration is implemented"
    hd = q_norm_g.shape[1]
    ssm_w = w_glu.shape[1]
    attn_w = w_br_attn.shape[1]
    kv_w = attn_w // Q_PER_KV
    widths = (kv_w, ssm_w, attn_w, 2 * D)
    assert L % GRID_W == 0 and L % SSM_CHUNK == 0 and Lc % SSM_CHUNK == 0 and B <= 7
    N, Nc = B * L, B * Lc
    l = 0

    cc = jnp.zeros((8, D), F32).at[:B].set(c).at[B].set(c_ctx)
    mod = _modulation(cc, w_mod[l], b_mod[l][None, :])

    def mvec(k, lo, hi):
        return mod[lo:hi, k * D:(k + 1) * D][:, None, :]

    xm = [mvec(k, 0, B) for k in range(N_MOD)]
    cm = [mvec(k, B, B + 1) for k in range(5)]
    ng = norm_g[l][:, None, :]

    x2 = x.reshape(N, D)
    c2 = ctx.reshape(Nc, D)

    x2, wd1, (w_in_b,) = _ffn(x2, ng[0], xm[0], xm[1], xm[2], w_ffn1_gate[l], w_ffn1_up[l],
                              w_ffn1_down[l], L, side_cast=(w_in[l],))
    c2, _, _ = _ffn(c2, ng[0], cm[0], cm[1], cm[2], w_ffn1_gate[l], w_ffn1_up[l], wd1, Nc)

    cos, sin = _rope_tables(L, hd)
    kg = k_norm_g[l][None, :]
    qg = q_norm_g[l][None, :] * (hd ** -0.5 * math.log2(math.e))
    k, v, u, q, gate = _proj(x2, ng[1], xm[3], xm[4], w_in_b, cos, sin, kg, L, widths, True)
    kc, vc, uc = _proj(c2, ng[1], cm[3], cm[4], w_in_b, cos, sin, kg, Nc, widths, False)

    attn, _ = _attention(q, cos, sin, qg, kc, k, vc, v, B, hd)
    ops = _ssm_operators(ssm_a_re[l], ssm_a_im[l], ssm_log_dt[l], ssm_b_re[l], ssm_b_im[l],
                         ssm_c_re[l], ssm_c_im[l])
    y_ssm = _s5(u, uc, ssm_d[l][None, :], ops, B)

    x2 = _mixer_out(attn, y_ssm, gate, x2, xm[5], w_glu[l].astype(BF16), b_glu[l][None, :],
                    w_br_attn[l].astype(BF16), w_br_ssm[l].astype(BF16), w_out[l].astype(BF16), L)

    x2, _, _ = _ffn(x2, ng[2], xm[6], xm[7], xm[8], w_ffn2_gate[l], w_ffn2_up[l],
                    w_ffn2_down[l], L)
    return x2.reshape(B, L, D)
```

```python
import functools
import math

import jax
import jax.numpy as jnp
import numpy as np
from jax import lax
from jax.experimental import pallas as pl
from jax.experimental.pallas import tpu as pltpu

F32 = jnp.float32
BF16 = jnp.bfloat16

GRID_W = 64
ROPE_THETA = 10000.0
NORM_EPS = 1e-6
N_MOD = 9
Q_PER_KV = 4
SSM_CHUNK = 16
LANES = 128
VMEM_LIMIT_BYTES = 58 * 1024 * 1024


def _tile(n, pref, mult=8):
    if n <= pref:
        return n
    for t in range(pref, 0, -1):
        if n % t == 0 and t % mult == 0:
            return t
    return n


def _params(*sem):
    return pltpu.CompilerParams(dimension_semantics=sem, vmem_limit_bytes=VMEM_LIMIT_BYTES)


def _as_bf16(w):
    return w if w.dtype == BF16 else w.astype(BF16)


def _norm_mod(x, g, sh, sc):
    ms = jnp.mean(x * x, axis=-1, keepdims=True)
    h = (x * lax.rsqrt(ms + NORM_EPS)) * g
    return (h * (1.0 + sc) + sh).astype(BF16)


def _mod_kernel(c_ref, w_ref, b_ref, o_ref):
    c = c_ref[...]
    s = c * jax.nn.sigmoid(c)
    o_ref[...] = jnp.dot(s.astype(BF16), w_ref[...].astype(BF16),
                         preferred_element_type=F32) + b_ref[...]


def _modulation(cc, w_mod, b_mod):
    D, M = w_mod.shape
    tn = _tile(M, 1024, LANES)
    return pl.pallas_call(
        _mod_kernel,
        grid=(M // tn,),
        in_specs=[pl.BlockSpec((cc.shape[0], D), lambda j: (0, 0)),
                  pl.BlockSpec((D, tn), lambda j: (0, j)),
                  pl.BlockSpec((1, tn), lambda j: (0, j))],
        out_specs=pl.BlockSpec((cc.shape[0], tn), lambda j: (0, j)),
        out_shape=jax.ShapeDtypeStruct((cc.shape[0], M), F32),
        compiler_params=_params("arbitrary"),
        name="modulation",
    )(cc, w_mod, b_mod)


def _ffn_act_kernel(x_ref, ng_ref, sh_ref, sc_ref, wg_ref, wu_ref, *rest, sub):
    if len(rest) == 4:
        wd_ref, a_ref, wdb_ref, h_ref = rest

        @pl.when(pl.program_id(0) == 0)
        def _():
            wdb_ref[...] = wd_ref[...].astype(BF16)
    else:
        a_ref, h_ref = rest

    half = a_ref.shape[1] // 2
    j = pl.program_id(1)

    def emit(row_blocks):
        halves = [(c, _as_bf16(wg_ref[:, c:c + half]), _as_bf16(wu_ref[:, c:c + half]))
                  for c in (0, half)]
        for rows, make_h in row_blocks:
            h = make_h()
            for c, wg, wu in halves:
                g = jnp.dot(h, wg, preferred_element_type=F32)
                u = jnp.dot(h, wu, preferred_element_type=F32)
                a_ref[rows, c:c + half] = ((g * jax.nn.sigmoid(g)) * u).astype(a_ref.dtype)

    def normed(rows):
        def make_h():
            h = _norm_mod(x_ref[rows, :], ng_ref[...], sh_ref[...], sc_ref[...])
            h_ref[rows, :] = h
            return h
        return make_h

    @pl.when(j == 0)
    def _():
        emit([(slice(r, r + sub), normed(slice(r, r + sub))) for r in range(0, x_ref.shape[0], sub)])

    @pl.when(j > 0)
    def _():
        emit([(slice(None), lambda: h_ref[...])])


def _side_cast_specs(arrays, nsteps, step_of):
    specs, shapes = [], []
    for a in arrays:
        rows = a.shape[0] // nsteps
        assert a.shape[0] % nsteps == 0 and rows % 16 == 0, (a.shape, nsteps)
        specs.append(pl.BlockSpec((rows, a.shape[1]), lambda *ids: (step_of(*ids), 0)))
        shapes.append(jax.ShapeDtypeStruct(a.shape, BF16))
    return specs, shapes


def _side_cast(src_refs, dst_refs):
    for src, dst in zip(src_refs, dst_refs):
        dst[...] = src[...].astype(BF16)


def _ffn_down_kernel(a_ref, x_ref, gate_ref, wd_ref, *rest):
    n_cast = len(rest) // 2
    o_ref = rest[n_cast]
    _side_cast(rest[:n_cast], rest[n_cast + 1:])
    half = o_ref.shape[1] // 2
    for c in (0, half):
        acc = jnp.dot(a_ref[...], wd_ref[:, c:c + half], preferred_element_type=F32)
        o_ref[:, c:c + half] = x_ref[:, c:c + half] + (0.5 * gate_ref[:, c:c + half]) * acc


def _ffn(x, ng, sh, sc, gate, wg, wu, wd, rows_per_mod, side_cast=()):
    N, D = x.shape
    F = wg.shape[1]
    tm = _tile(rows_per_mod, 1024)
    tf = _tile(F, 512, 2 * LANES)
    tn = _tile(D, 512, 2 * LANES)
    tpb = rows_per_mod // tm
    nj = F // tf
    vec = pl.BlockSpec((None, 1, D), lambda i, j: (i // tpb, 0, 0))
    in_specs = [pl.BlockSpec((tm, D), lambda i, j: (i, 0)),
                pl.BlockSpec((1, D), lambda i, j: (0, 0)), vec, vec,
                pl.BlockSpec((D, tf), lambda i, j: (0, j)),
                pl.BlockSpec((D, tf), lambda i, j: (0, j))]
    out_specs = [pl.BlockSpec((tm, tf), lambda i, j: (i, j))]
    out_shape = [jax.ShapeDtypeStruct((N, F), BF16)]
    operands = [x, ng, sh, sc, wg, wu]
    if wd.dtype != BF16:
        wd_rows = pl.BlockSpec((tf, D), lambda i, j: (jnp.where(i == 0, j, nj - 1), 0))
        in_specs.append(wd_rows)
        out_specs.append(wd_rows)
        out_shape.append(jax.ShapeDtypeStruct((F, D), BF16))
        operands.append(wd)
    res = pl.pallas_call(
        functools.partial(_ffn_act_kernel, sub=_tile(tm, 512)),
        grid=(N // tm, nj),
        in_specs=in_specs,
        out_specs=out_specs,
        out_shape=out_shape,
        scratch_shapes=[pltpu.VMEM((tm, D), BF16)],
        compiler_params=_params("arbitrary", "arbitrary"),
        name="ffn_act",
    )(*operands)
    a = res[0]
    if wd.dtype != BF16:
        wd = res[1]
    nd = D // tn
    cast_specs, cast_shapes = _side_cast_specs(side_cast, (N // tm) * nd, lambda i, j: i * nd + j)
    res = pl.pallas_call(
        _ffn_down_kernel,
        grid=(N // tm, nd),
        in_specs=[pl.BlockSpec((tm, F), lambda i, j: (i, 0)),
                  pl.BlockSpec((tm, tn), lambda i, j: (i, j)),
                  pl.BlockSpec((None, 1, tn), lambda i, j: (i // tpb, 0, j)),
                  pl.BlockSpec((F, tn), lambda i, j: (0, j))] + cast_specs,
        out_specs=[pl.BlockSpec((tm, tn), lambda i, j: (i, j))] + cast_specs,
        out_shape=[jax.ShapeDtypeStruct((N, D), F32)] + cast_shapes,
        compiler_params=_params("arbitrary", "arbitrary"),
        name="ffn_down",
    )(a, x, gate, wd, *side_cast)
    return res[0], wd, res[1:]


def _head_norm(acc, g, cos, sin, head_dim):
    quarter = head_dim // 4
    out = []
    for hh in range(acc.shape[1] // head_dim):
        xh = acc[:, hh * head_dim:(hh + 1) * head_dim]
        ms = jnp.mean(xh * xh, axis=-1, keepdims=True)
        xh = (xh * lax.rsqrt(ms + NORM_EPS)) * g
        if cos is not None:
            lane = lax.broadcasted_iota(jnp.int32, xh.shape, 1)
            partner = jnp.where((lane % (2 * quarter)) < quarter,
                                pltpu.roll(xh, head_dim - quarter, 1),
                                pltpu.roll(xh, quarter, 1))
            xh = xh * cos + partner * sin
        out.append(xh.astype(BF16))
    return out[0] if len(out) == 1 else jnp.concatenate(out, axis=1)


def _proj_kernel(x_ref, ng_ref, sh_ref, sc_ref, w_ref, cos_ref, sin_ref, kg_ref,
                 k_ref, v_ref, u_ref, *rest, bounds, rope, head_dim, sub):
    h_ref = rest[-1]
    j = pl.program_id(1)
    tm = x_ref.shape[0]
    jv, ju, jq, jg = bounds
    assert jv == 1

    @pl.when(j == 0)
    def _():
        for r in range(0, tm, sub):
            rows = slice(r, r + sub)
            h = _norm_mod(x_ref[rows, :], ng_ref[...], sh_ref[...], sc_ref[...])
            h_ref[rows, :] = h
            acc = jnp.dot(h, w_ref[...], preferred_element_type=F32)
            cos = cos_ref[rows, :] if rope else None
            sin = sin_ref[rows, :] if rope else None
            k_ref[rows, :] = _head_norm(acc, kg_ref[...], cos, sin, head_dim)

    @pl.when(j > 0)
    def _():
        acc = jnp.dot(h_ref[...], w_ref[...], preferred_element_type=F32)

        @pl.when(j < ju)
        def _():
            v_ref[...] = acc.astype(BF16)

        @pl.when((j >= ju) & (j < jq))
        def _():
            u_ref[...] = acc

        if len(rest) == 3:
            q_ref, gate_ref = rest[0], rest[1]

            @pl.when((j >= jq) & (j < jg))
            def _():
                q_ref[...] = acc

            @pl.when(j >= jg)
            def _():
                gate_ref[...] = acc.astype(BF16)


def _proj(x, ng, sh, sc, w, cos, sin, kg, rows_per_seq, widths, latent):
    N, D = x.shape
    kv_w, ssm_w, attn_w, gate_w = widths
    head_dim = kg.shape[1]
    tn = kv_w
    assert tn % LANES == 0 and ssm_w % tn == 0 and attn_w % tn == 0 and gate_w % tn == 0
    jv, ju = 1, 2
    jq = ju + ssm_w // tn
    jg = jq + attn_w // tn
    nj = jg + gate_w // tn if latent else jq
    tm = _tile(rows_per_seq, 1024)
    tpb = rows_per_seq // tm
    vec = pl.BlockSpec((None, 1, D), lambda i, j: (i // tpb, 0, 0))
    tab = pl.BlockSpec((tm, head_dim), lambda i, j: (i % tpb, 0))
    hvec = pl.BlockSpec((1, head_dim), lambda i, j: (0, 0))

    def out_spec(j0, nblk):
        return pl.BlockSpec((tm, tn), lambda i, j: (i, jnp.clip(j - j0, 0, nblk - 1)))

    out_specs = [out_spec(0, 1), out_spec(jv, 1), out_spec(ju, jq - ju)]
    out_shape = [jax.ShapeDtypeStruct((N, kv_w), BF16), jax.ShapeDtypeStruct((N, kv_w), BF16),
                 jax.ShapeDtypeStruct((N, ssm_w), F32)]
    if latent:
        out_specs += [out_spec(jq, jg - jq), out_spec(jg, nj - jg)]
        out_shape += [jax.ShapeDtypeStruct((N, attn_w), F32), jax.ShapeDtypeStruct((N, gate_w), BF16)]
    return pl.pallas_call(
        functools.partial(_proj_kernel, bounds=(jv, ju, jq, jg), rope=latent, head_dim=head_dim,
                          sub=_tile(tm, 512)),
        grid=(N // tm, nj),
        in_specs=[pl.BlockSpec((tm, D), lambda i, j: (i, 0)),
                  pl.BlockSpec((1, D), lambda i, j: (0, 0)), vec, vec,
                  pl.BlockSpec((D, tn), lambda i, j: (0, j)),
                  tab, tab, hvec],
        out_specs=out_specs,
        out_shape=out_shape,
        scratch_shapes=[pltpu.VMEM((tm, D), BF16)],
        compiler_params=_params("arbitrary", "arbitrary"),
        name="proj_latent" if latent else "proj_context",
    )(x, ng, sh, sc, w, cos, sin, kg)


def _rope_tables(L, head_dim):
    t = np.arange(L)
    row = (t // GRID_W).astype(np.float64)
    col = (t % GRID_W).astype(np.float64)
    half = head_dim // 4
    inv_freq = ROPE_THETA ** (-np.arange(half, dtype=np.float64) / half)
    ar = row[:, None] * inv_freq
    ac = col[:, None] * inv_freq
    cos = np.concatenate([np.cos(ar), np.cos(ar), np.cos(ac), np.cos(ac)], axis=-1)
    sin = np.concatenate([-np.sin(ar), np.sin(ar), -np.sin(ac), np.sin(ac)], axis=-1)
    return jnp.asarray(cos, F32), jnp.asarray(sin, F32)


def _attn_kernel(q_ref, cos_ref, sin_ref, qn_ref, cosn_ref, sinn_ref, qg_ref,
                 kc_ref, k_ref, vc_ref, v_ref, *rest, head_dim):
    n_cast = (len(rest) - 4) // 2
    o_ref = rest[n_cast]
    kall_ref, vext_ref, qs_ref = rest[-3:]
    _side_cast(rest[:n_cast], rest[n_cast + 1:2 * n_cast + 1])
    lc = kc_ref.shape[0]
    i = pl.program_id(2)

    @pl.when(i == 0)
    def _():
        kall_ref[0:lc, :] = kc_ref[...]
        kall_ref[lc:, :] = k_ref[...]
        vext_ref[0:lc, 0:head_dim] = vc_ref[...]
        vext_ref[lc:, 0:head_dim] = v_ref[...]
        vext_ref[:, head_dim:] = jnp.ones((vext_ref.shape[0], head_dim), BF16)
        qs_ref[0] = _head_norm(q_ref[...], qg_ref[...], cos_ref[...], sin_ref[...], head_dim)

    k = kall_ref[...]
    vext = vext_ref[...]
    n_rep = q_ref.shape[1] // head_dim
    heads = [slice(r * head_dim, (r + 1) * head_dim) for r in range(n_rep)]

    def scores(sl):
        return lax.dot_general(qs_ref[i % 2, :, sl], k, (((1,), (1,)), ((), ())),
                               preferred_element_type=F32)

    s_next = scores(heads[0])
    for r, sl in enumerate(heads):
        s = s_next
        if r + 1 < n_rep:
            s_next = scores(heads[r + 1])
        qs_ref[(i + 1) % 2, :, sl] = _head_norm(qn_ref[:, sl], qg_ref[...], cosn_ref[...],
                                                sinn_ref[...], head_dim)
        m = jnp.max(s, axis=-1, keepdims=True)
        p = jnp.exp2(s - m).astype(BF16)
        oe = jnp.dot(p, vext, preferred_element_type=F32)
        o_ref[:, sl] = (oe[:, :head_dim] / oe[:, head_dim:]).astype(o_ref.dtype)


def _attention(q, cos, sin, qg, kc, k, vc, v, B, head_dim, side_cast=()):
    N, W = q.shape
    L, Lc = N // B, kc.shape[0] // B
    kvh = k.shape[1] // head_dim
    gw = W // kvh
    tq = _tile(L, 1024)
    tpb = L // tq
    kv_lat = pl.BlockSpec((L, head_dim), lambda b, h, i: (b, h))
    kv_ctx = pl.BlockSpec((Lc, head_dim), lambda b, h, i: (b, h))
    nxt = lambda i: jnp.minimum(i + 1, tpb - 1)
    tab = pl.BlockSpec((tq, head_dim), lambda b, h, i: (i, 0))
    tab_next = pl.BlockSpec((tq, head_dim), lambda b, h, i: (nxt(i), 0))
    cast_specs, cast_shapes = _side_cast_specs(side_cast, B * kvh * tpb,
                                               lambda b, h, i: (b * kvh + h) * tpb + i)
    res = pl.pallas_call(
        functools.partial(_attn_kernel, head_dim=head_dim),
        grid=(B, kvh, tpb),
        in_specs=[pl.BlockSpec((tq, gw), lambda b, h, i: (b * tpb + i, h)), tab, tab,
                  pl.BlockSpec((tq, gw), lambda b, h, i: (b * tpb + nxt(i), h)), tab_next, tab_next,
                  pl.BlockSpec((1, head_dim), lambda b, h, i: (0, 0)),
                  kv_ctx, kv_lat, kv_ctx, kv_lat] + cast_specs,
        out_specs=[pl.BlockSpec((tq, gw), lambda b, h, i: (b * tpb + i, h))] + cast_specs,
        out_shape=[jax.ShapeDtypeStruct((N, W), BF16)] + cast_shapes,
        scratch_shapes=[pltpu.VMEM((L + Lc, head_dim), BF16),
                        pltpu.VMEM((L + Lc, 2 * head_dim), BF16),
                        pltpu.VMEM((2, tq, gw), BF16)],
        compiler_params=_params("arbitrary", "arbitrary", "arbitrary"),
        name="attention",
    )(q, cos, sin, q, cos, sin, qg, kc, k, vc, v, *side_cast)
    return res[0], res[1:]


def _ssm_operators(a_re, a_im, log_dt, b_re, b_im, c_re, c_im):
    T = SSM_CHUNK
    a_re, a_im = a_re.astype(F32), a_im.astype(F32)
    dt = jnp.exp(log_dt.astype(F32))[..., None]
    mag = jnp.exp(a_re * dt)
    lr = mag * jnp.cos(a_im * dt)
    li = mag * jnp.sin(a_im * dt)
    den = a_re * a_re + a_im * a_im
    cr = ((lr - 1.0) * a_re + li * a_im) / den
    ci = (li * a_re - (lr - 1.0) * a_im) / den
    pr, pi = [jnp.ones_like(lr)], [jnp.zeros_like(lr)]
    for _ in range(T):
        pr.append(pr[-1] * lr - pi[-1] * li)
        pi.append(pr[-2] * li + pi[-1] * lr)
    pw_r, pw_i = jnp.stack(pr), jnp.stack(pi)
    _, G, P, E = b_re.shape
    bt_r = jnp.swapaxes(b_re.astype(F32), 2, 3)
    bt_i = jnp.swapaxes(b_im.astype(F32), 2, 3)
    bb_r = cr[:, :, None, :] * bt_r - ci[:, :, None, :] * bt_i
    bb_i = cr[:, :, None, :] * bt_i + ci[:, :, None, :] * bt_r
    c_r, c_i = c_re.astype(F32), c_im.astype(F32)

    def lag_table(pw, fwd_lags, bwd_lags, reps):
        tab = jnp.stack([jnp.moveaxis(pw[fwd_lags, 0], 0, 1), jnp.moveaxis(pw[bwd_lags, 1], 0, 1)])
        return jnp.concatenate([tab] * reps, axis=-1)

    steps = np.arange(T)
    lags = np.arange(T + 1)
    p1 = lag_table(pw_r, T - 1 - steps, steps, 4)
    p2 = lag_table(pw_i, T - 1 - steps, steps, 4)
    q1 = lag_table(pw_r, lags, T - lags, 2)
    q2 = lag_table(pw_i, lags, T - lags, 2)
    bb1 = jnp.concatenate([bb_r, bb_i, bb_i, bb_r], axis=-1)
    bb2 = jnp.concatenate([-bb_i, bb_r, bb_r, -bb_i], axis=-1)
    cc1 = jnp.concatenate([c_r, -c_i], axis=-1)
    cc2 = jnp.concatenate([-c_i, -c_r], axis=-1)
    ar, ai = pw_r[T], pw_i[T]
    dec = jnp.stack([jnp.concatenate([ar, ar], -1), jnp.concatenate([-ai, ai], -1)], axis=2)
    return p1, p2, bb1, bb2, q1, q2, cc1, cc2, dec


def _state_in_operator(p1_ref, p2_ref, bb1_ref, bb2_ref, d, g):
    bb1, bb2 = bb1_ref[d, g], bb2_ref[d, g]
    rows = [p1_ref[d, g, s:s + 1, :] * bb1 + p2_ref[d, g, s:s + 1, :] * bb2
            for s in range(SSM_CHUNK)]
    return jnp.concatenate(rows, axis=0).astype(BF16)


def _readout_table(q1_ref, q2_ref, cc1_ref, cc2_ref, d, g):
    cc1, cc2 = cc1_ref[d, g], cc2_ref[d, g]
    rows = [q1_ref[d, g, j:j + 1, :] * cc1 + q2_ref[d, g, j:j + 1, :] * cc2
            for j in range(SSM_CHUNK + 1)]
    return jnp.concatenate(rows, axis=0)


def _intra_operator(wf, wb, e):
    T = SSM_CHUNK
    t_of_lane = lax.broadcasted_iota(jnp.int32, wf.shape, 1) // e
    rows = []
    for s in range(T):
        f = wf if s == 0 else jnp.where(t_of_lane >= s, pltpu.roll(wf, s * e, 1), 0.0)
        b = wb if s == T - 1 else jnp.where(t_of_lane <= s, pltpu.roll(wb, (s + 1) * e, 1), 0.0)
        rows.append(f + b)
    return jnp.concatenate(rows, axis=0).astype(BF16)


def _row_pitch(n):
    p = -(-n // 8)
    return 8 * (p if p % 2 else p + 1)


def _ssm_kernel(u_ref, uc_ref, d_ref, p1_ref, p2_ref, bb1_ref, bb2_ref, q1_ref, q2_ref,
                cc1_ref, cc2_ref, dec_ref, y_ref, z_ref, s1_ref, s2_ref, hp_ref, *, batch, e):
    T = SSM_CHUNK
    gpb = dec_ref.shape[1]
    rp = z_ref.shape[1]
    n_lat, n_ctx = u_ref.shape[0] // T, uc_ref.shape[0] // T
    ncl, ncc = n_lat // batch, n_ctx // batch
    pl_, pc_ = _row_pitch(ncl), _row_pitch(ncc)
    lat0 = batch * pc_
    r = lat0 + batch * pl_
    lanes = u_ref.shape[1]
    pw = hp_ref.shape[3]

    def padded(rows, n, pitch):
        if pitch == n:
            return [rows]
        out = []
        for b in range(batch):
            out += [rows[b * n:(b + 1) * n, :], jnp.zeros((pitch - n, lanes), F32)]
        return out

    xt = []
    for s in range(T):
        parts = (padded(uc_ref[pl.ds(s, n_ctx, stride=T), :], ncc, pc_)
                 + padded(u_ref[pl.ds(s, n_lat, stride=T), :], ncl, pl_))
        if rp > r:
            parts.append(jnp.zeros((rp - r, lanes), F32))
        xt.append(jnp.concatenate(parts, axis=0).astype(BF16).T)
    for g in range(gpb):
        zt = jnp.concatenate([xt[s][g * e:(g + 1) * e, :] for s in range(T)], axis=0)
        z_ref[g] = zt.T
    for d in range(2):
        for g in range(gpb):
            st = jnp.dot(z_ref[g], _state_in_operator(p1_ref, p2_ref, bb1_ref, bb2_ref, d, g),
                         preferred_element_type=F32)
            s1_ref[d, g] = st[:, :pw]
            s2_ref[d, g] = st[:, pw:]

    def sweep(base, count, stride, carry):
        def body(it, carry):
            out = []
            for d in range(2):
                n = it if d == 0 else count - 1 - it
                rows = pl.ds(base + n, batch, stride=stride)
                for g in range(gpb):
                    h1, h2 = carry[2 * (d * gpb + g)], carry[2 * (d * gpb + g) + 1]
                    hp_ref[d, g, rows, :] = h1
                    a1 = dec_ref[d, g, 0:1, :]
                    a2 = dec_ref[d, g, 1:2, :]
                    out.append(a1 * h1 + a2 * h2 + s1_ref[d, g, rows, :])
                    out.append(a1 * h2 - a2 * h1 + s2_ref[d, g, rows, :])
            return tuple(out)
        return lax.fori_loop(0, count, body, carry)

    for d in range(2):
        for g in range(gpb):
            for b in range(batch):
                if pc_ > ncc:
                    hp_ref[d, g, b * pc_ + ncc:(b + 1) * pc_, :] = jnp.zeros((pc_ - ncc, pw), F32)
                if pl_ > ncl:
                    hp_ref[d, g, lat0 + b * pl_ + ncl:lat0 + (b + 1) * pl_, :] = jnp.zeros((pl_ - ncl, pw), F32)
            if rp > r:
                hp_ref[d, g, r:rp, :] = jnp.zeros((rp - r, pw), F32)
    zero = jnp.zeros((batch, pw), F32)
    carry = sweep(0, ncc, pc_, tuple([zero] * (4 * gpb)))
    sweep(lat0, ncl, pl_, carry)
    nt = (((1,), (1,)), ((), ()))
    te = T * e
    yt = []
    for g in range(gpb):
        rf = _readout_table(q1_ref, q2_ref, cc1_ref, cc2_ref, 0, g)
        rb = _readout_table(q1_ref, q2_ref, cc1_ref, cc2_ref, 1, g)
        wf = lax.dot_general(bb1_ref[0, g][:, :pw], rf[:te], nt, precision=lax.Precision.HIGHEST,
                             preferred_element_type=F32)
        wb = lax.dot_general(bb1_ref[1, g][:, :pw], rb[e:], nt, precision=lax.Precision.HIGHEST,
                             preferred_element_type=F32)
        y = jnp.dot(z_ref[g], _intra_operator(wf, wb, e), preferred_element_type=F32)
        y += lax.dot_general(hp_ref[0, g].astype(BF16), rf[e:].astype(BF16), nt,
                             preferred_element_type=F32)
        y += lax.dot_general(hp_ref[1, g].astype(BF16), rb[:te].astype(BF16), nt,
                             preferred_element_type=F32)
        yt.append(y.T)
    dvec = d_ref[...]
    for t in range(T):
        blk = jnp.concatenate([yt[g][t * e:(t + 1) * e, :] for g in range(gpb)], axis=0).T
        for b in range(batch):
            rows = pl.ds(b * ncl * T + t, ncl, stride=T)
            y_ref[rows, :] = (blk[lat0 + b * pl_:lat0 + b * pl_ + ncl, :] + dvec * u_ref[rows, :])


def _s5(u, uc, d, ops, batch):
    dec = ops[-1]
    N, W = u.shape
    Nc = uc.shape[0]
    G = dec.shape[1]
    E = W // G
    gpb = LANES // E
    TE = SSM_CHUNK * E
    P2 = dec.shape[-1]
    assert P2 == LANES and W % LANES == 0

    def table(a):
        return pl.BlockSpec((2, gpb) + a.shape[2:], lambda j: (0, j, 0, 0))
    r = batch * (_row_pitch(N // SSM_CHUNK // batch) + _row_pitch(Nc // SSM_CHUNK // batch))
    rp = -(-r // LANES) * LANES
    return pl.pallas_call(
        functools.partial(_ssm_kernel, batch=batch, e=E),
        grid=(W // LANES,),
        in_specs=[pl.BlockSpec((N, LANES), lambda j: (0, j)),
                  pl.BlockSpec((Nc, LANES), lambda j: (0, j)),
                  pl.BlockSpec((1, LANES), lambda j: (0, j))] + [table(a) for a in ops],
        out_specs=pl.BlockSpec((N, LANES), lambda j: (0, j)),
        out_shape=jax.ShapeDtypeStruct((N, W), F32),
        scratch_shapes=[pltpu.VMEM((gpb, rp, TE), BF16),
                        pltpu.VMEM((2, gpb, rp, P2), F32),
                        pltpu.VMEM((2, gpb, rp, P2), F32),
                        pltpu.VMEM((2, gpb, rp, P2), F32)],
        compiler_params=_params("arbitrary"),
        name="s5",
    )(u, uc, d, *ops)


def _mixer_out_kernel(attn_ref, y_ref, ga_ref, gs_ref, x_ref, g2_ref, wglu_ref, bglu_ref,
                      wa_ref, ws_ref, wo_ref, o_ref, y2_ref, m_ref, *, sub):
    j = pl.program_id(1)
    tm, tn = o_ref.shape
    nd = m_ref.shape[1] // tn
    half = tn // 2

    def merge(rows, attn, y2):
        for c in (0, half):
            pa = jnp.dot(attn, wa_ref[:, c:c + half], preferred_element_type=F32)
            ps = jnp.dot(y2, ws_ref[:, c:c + half], preferred_element_type=F32)
            merged = (jax.nn.sigmoid(ga_ref[rows, c:c + half].astype(F32)) * pa
                      + jax.nn.sigmoid(gs_ref[rows, c:c + half].astype(F32)) * ps)
            m_ref[rows, pl.ds(pl.multiple_of(j * tn + c, half), half)] = merged.astype(BF16)

    @pl.when(j == 0)
    def _():
        for r in range(0, tm, sub):
            rows = slice(r, r + sub)
            y = jax.nn.gelu(y_ref[rows, :])
            z = jnp.dot(y.astype(BF16), wglu_ref[...], preferred_element_type=F32) + bglu_ref[...]
            y2 = (y * jax.nn.sigmoid(z)).astype(BF16)
            y2_ref[rows, :] = y2
            merge(rows, attn_ref[rows, :], y2)

    @pl.when((j > 0) & (j < nd))
    def _():
        merge(slice(None), attn_ref[...], y2_ref[...])

    @pl.when(j >= nd)
    def _():
        for c in (0, half):
            col = pl.ds(pl.multiple_of((j - nd) * tn + c, half), half)
            acc = jnp.dot(m_ref[...], wo_ref[:, col], preferred_element_type=F32)
            o_ref[:, c:c + half] = x_ref[:, c:c + half] + g2_ref[:, c:c + half] * acc


def _mixer_out(attn, y_ssm, gate, x, g2, w_glu, b_glu, wa, ws, wo, rows_per_mod):
    N, D = x.shape
    WA, WS = attn.shape[1], y_ssm.shape[1]
    tm = _tile(rows_per_mod, 1024)
    tn = _tile(D, 512, 2 * LANES)
    nd = D // tn
    tpb = rows_per_mod // tm
    lo = lambda j: jnp.minimum(j, nd - 1)
    hi = lambda j: jnp.maximum(j - nd, 0)
    return pl.pallas_call(
        functools.partial(_mixer_out_kernel, sub=_tile(tm, 512)),
        grid=(N // tm, 2 * nd),
        in_specs=[pl.BlockSpec((tm, WA), lambda i, j: (i, 0)),
                  pl.BlockSpec((tm, WS), lambda i, j: (i, 0)),
                  pl.BlockSpec((tm, tn), lambda i, j: (i, lo(j))),
                  pl.BlockSpec((tm, tn), lambda i, j: (i, nd + lo(j))),
                  pl.BlockSpec((tm, tn), lambda i, j: (i, hi(j))),
                  pl.BlockSpec((None, 1, tn), lambda i, j: (i // tpb, 0, hi(j))),
                  pl.BlockSpec((WS, WS), lambda i, j: (0, 0), pipeline_mode=pl.Buffered(1)),
                  pl.BlockSpec((1, WS), lambda i, j: (0, 0)),
                  pl.BlockSpec((WA, tn), lambda i, j: (0, lo(j))),
                  pl.BlockSpec((WS, tn), lambda i, j: (0, lo(j))),
                  pl.BlockSpec((D, D), lambda i, j: (0, 0), pipeline_mode=pl.Buffered(1))],
        out_specs=pl.BlockSpec((tm, tn), lambda i, j: (i, hi(j))),
        out_shape=jax.ShapeDtypeStruct((N, D), F32),
        scratch_shapes=[pltpu.VMEM((tm, WS), BF16), pltpu.VMEM((tm, D), BF16)],
        compiler_params=_params("arbitrary", "arbitrary"),
        name="mixer_out",
    )(attn, y_ssm, gate, gate, x, g2, w_glu, b_glu, wa, ws, wo)


def kernel(x, c, ctx, c_ctx, w_mod, b_mod, norm_g, w_ffn1_gate, w_ffn1_up, w_ffn1_down, w_in, q_norm_g, k_norm_g, ssm_a_re, ssm_a_im, ssm_log_dt, ssm_b_re, ssm_b_im, ssm_c_re, ssm_c_im, ssm_d, w_glu, b_glu, w_br_attn, w_br_ssm, w_out, w_ffn2_gate, w_ffn2_up, w_ffn2_down):
    B, L, D = x.shape
    Lc = ctx.shape[1]
    assert w_mod.shape[0] == 1, "only the single (last) layer configuration is implemented"
    hd = q_norm_g.shape[1]
    ssm_w = w_glu.shape[1]
    attn_w = w_br_attn.shape[1]
    kv_w = attn_w // Q_PER_KV
    widths = (kv_w, ssm_w, attn_w, 2 * D)
    assert L % GRID_W == 0 and L % SSM_CHUNK == 0 and Lc % SSM_CHUNK == 0 and B <= 7
    N, Nc = B * L, B * Lc
    l = 0

    cc = jnp.zeros((8, D), F32).at[:B].set(c).at[B].set(c_ctx)
    mod = _modulation(cc, w_mod[l], b_mod[l][None, :])

    def mvec(k, lo, hi):
        return mod[lo:hi, k * D:(k + 1) * D][:, None, :]

    xm = [mvec(k, 0, B) for k in range(N_MOD)]
    cm = [mvec(k, B, B + 1) for k in range(5)]
    ng = norm_g[l][:, None, :]

    x2 = x.reshape(N, D)
    c2 = ctx.reshape(Nc, D)

    x2, wd1, (w_in_b,) = _ffn(x2, ng[0], xm[0], xm[1], xm[2], w_ffn1_gate[l], w_ffn1_up[l],
                              w_ffn1_down[l], L, side_cast=(w_in[l],))
    c2, _, _ = _ffn(c2, ng[0], cm[0], cm[1], cm[2], w_ffn1_gate[l], w_ffn1_up[l], wd1, Nc)

    cos, sin = _rope_tables(L, hd)
    kg = k_norm_g[l][None, :]
    qg = q_norm_g[l][None, :] * (hd ** -0.5 * math.log2(math.e))
    k, v, u, q, gate = _proj(x2, ng[1], xm[3], xm[4], w_in_b, cos, sin, kg, L, widths, True)
    kc, vc, uc = _proj(c2, ng[1], cm[3], cm[4], w_in_b, cos, sin, kg, Nc, widths, False)

    attn, _ = _attention(q, cos, sin, qg, kc, k, vc, v, B, hd)
    ops = _ssm_operators(ssm_a_re[l], ssm_a_im[l], ssm_log_dt[l], ssm_b_re[l], ssm_b_im[l],
                         ssm_c_re[l], ssm_c_im[l])
    y_ssm = _s5(u, uc, ssm_d[l][None, :], ops, B)

    x2 = _mixer_out(attn, y_ssm, gate, x2, xm[5], w_glu[l].astype(BF16), b_glu[l][None, :],
                    w_br_attn[l].astype(BF16), w_br_ssm[l].astype(BF16), w_out[l].astype(BF16), L)

    x2, _, _ = _ffn(x2, ng[2], xm[6], xm[7], xm[8], w_ffn2_gate[l], w_ffn2_up[l],
                    w_ffn2_down[l], L)
    return x2.reshape(B, L, D)
```

```python
import functools
import math

import jax
import jax.numpy as jnp
import numpy as np
from jax import lax
from jax.experimental import pallas as pl
from jax.experimental.pallas import tpu as pltpu

F32 = jnp.float32
BF16 = jnp.bfloat16

GRID_W = 64
ROPE_THETA = 10000.0
NORM_EPS = 1e-6
N_MOD = 9
Q_PER_KV = 4
SSM_CHUNK = 16
LANES = 128
VMEM_LIMIT_BYTES = 58 * 1024 * 1024


def _tile(n, pref, mult=8):
    if n <= pref:
        return n
    for t in range(pref, 0, -1):
        if n % t == 0 and t % mult == 0:
            return t
    return n


def _params(*sem):
    return pltpu.CompilerParams(dimension_semantics=sem, vmem_limit_bytes=VMEM_LIMIT_BYTES)


def _as_bf16(w):
    return w if w.dtype == BF16 else w.astype(BF16)


def _norm_mod(x, g, sh, sc):
    ms = jnp.mean(x * x, axis=-1, keepdims=True)
    h = (x * lax.rsqrt(ms + NORM_EPS)) * g
    return (h * (1.0 + sc) + sh).astype(BF16)


def _mod_kernel(c_ref, w_ref, b_ref, o_ref):
    c = c_ref[...]
    s = c * jax.nn.sigmoid(c)
    o_ref[...] = jnp.dot(s.astype(BF16), w_ref[...].astype(BF16),
                         preferred_element_type=F32) + b_ref[...]


def _modulation(cc, w_mod, b_mod):
    D, ncols = w_mod.shape
    tn = _tile(ncols, 1024, LANES)
    return pl.pallas_call(
        _mod_kernel,
        grid=(ncols // tn,),
        in_specs=[pl.BlockSpec((cc.shape[0], D), lambda j: (0, 0)),
                  pl.BlockSpec((D, tn), lambda j: (0, j)),
                  pl.BlockSpec((1, tn), lambda j: (0, j))],
        out_specs=pl.BlockSpec((cc.shape[0], tn), lambda j: (0, j)),
        out_shape=jax.ShapeDtypeStruct((cc.shape[0], ncols), F32),
        compiler_params=_params("arbitrary"),
        name="modulation",
    )(cc, w_mod, b_mod)


def _mod_row(ref, row0, tiles_per_row):
    return ref[pl.ds(row0 + pl.program_id(0) // tiles_per_row, 1), :]


def _ffn_act_kernel(x_ref, ng_ref, sh_ref, sc_ref, wg_ref, wu_ref, *rest, sub, row0, tpb):
    if len(rest) == 4:
        wd_ref, a_ref, wdb_ref, h_ref = rest

        @pl.when(pl.program_id(0) == 0)
        def _():
            wdb_ref[...] = wd_ref[...].astype(BF16)
    else:
        a_ref, h_ref = rest

    half = a_ref.shape[1] // 2
    j = pl.program_id(1)

    def emit(row_blocks):
        halves = [(c, _as_bf16(wg_ref[:, c:c + half]), _as_bf16(wu_ref[:, c:c + half]))
                  for c in (0, half)]
        for rows, make_h in row_blocks:
            h = make_h()
            for c, wg, wu in halves:
                g = jnp.dot(h, wg, preferred_element_type=F32)
                u = jnp.dot(h, wu, preferred_element_type=F32)
                a_ref[rows, c:c + half] = ((g * jax.nn.sigmoid(g)) * u).astype(a_ref.dtype)

    def normed(rows):
        def make_h():
            h = _norm_mod(x_ref[rows, :], ng_ref[...], _mod_row(sh_ref, row0, tpb),
                          _mod_row(sc_ref, row0, tpb))
            h_ref[rows, :] = h
            return h
        return make_h

    @pl.when(j == 0)
    def _():
        emit([(slice(r, r + sub), normed(slice(r, r + sub))) for r in range(0, x_ref.shape[0], sub)])

    @pl.when(j > 0)
    def _():
        emit([(slice(None), lambda: h_ref[...])])


def _side_cast_specs(arrays, nsteps, step_of):
    specs, shapes = [], []
    for a in arrays:
        rows = a.shape[0] // nsteps
        assert a.shape[0] % nsteps == 0 and rows % 16 == 0, (a.shape, nsteps)
        specs.append(pl.BlockSpec((rows, a.shape[1]), lambda *ids: (step_of(*ids), 0)))
        shapes.append(jax.ShapeDtypeStruct(a.shape, BF16))
    return specs, shapes


def _side_cast(src_refs, dst_refs):
    for src, dst in zip(src_refs, dst_refs):
        dst[...] = src[...].astype(BF16)


def _ffn_down_kernel(a_ref, x_ref, gate_ref, wd_ref, *rest, row0, tpb):
    n_cast = len(rest) // 2
    o_ref = rest[n_cast]
    _side_cast(rest[:n_cast], rest[n_cast + 1:])
    half = o_ref.shape[1] // 2
    half_gate = 0.5 * _mod_row(gate_ref, row0, tpb)
    for c in (0, half):
        acc = jnp.dot(a_ref[...], wd_ref[:, c:c + half], preferred_element_type=F32)
        o_ref[:, c:c + half] = x_ref[:, c:c + half] + half_gate[:, c:c + half] * acc


def _ffn(x, ng, mod, k0, row0, wg, wu, wd, rows_per_mod, side_cast=()):
    N, D = x.shape
    F = wg.shape[1]
    tm = _tile(rows_per_mod, 1024)
    tf = _tile(F, 512, 2 * LANES)
    tn = _tile(D, 512, 2 * LANES)
    tpb = rows_per_mod // tm
    nj = F // tf
    nrow = mod.shape[0]
    in_specs = [pl.BlockSpec((tm, D), lambda i, j: (i, 0)),
                pl.BlockSpec((1, D), lambda i, j: (0, 0)),
                pl.BlockSpec((nrow, D), lambda i, j: (0, k0)),
                pl.BlockSpec((nrow, D), lambda i, j: (0, k0 + 1)),
                pl.BlockSpec((D, tf), lambda i, j: (0, j)),
                pl.BlockSpec((D, tf), lambda i, j: (0, j))]
    out_specs = [pl.BlockSpec((tm, tf), lambda i, j: (i, j))]
    out_shape = [jax.ShapeDtypeStruct((N, F), BF16)]
    operands = [x, ng, mod, mod, wg, wu]
    if wd.dtype != BF16:
        wd_rows = pl.BlockSpec((tf, D), lambda i, j: (jnp.where(i == 0, j, nj - 1), 0))
        in_specs.append(wd_rows)
        out_specs.append(wd_rows)
        out_shape.append(jax.ShapeDtypeStruct((F, D), BF16))
        operands.append(wd)
    res = pl.pallas_call(
        functools.partial(_ffn_act_kernel, sub=_tile(tm, 512), row0=row0, tpb=tpb),
        grid=(N // tm, nj),
        in_specs=in_specs,
        out_specs=out_specs,
        out_shape=out_shape,
        scratch_shapes=[pltpu.VMEM((tm, D), BF16)],
        compiler_params=_params("arbitrary", "arbitrary"),
        name="ffn_act",
    )(*operands)
    a = res[0]
    if wd.dtype != BF16:
        wd = res[1]
    nd = D // tn
    cast_specs, cast_shapes = _side_cast_specs(side_cast, (N // tm) * nd, lambda i, j: i * nd + j)
    res = pl.pallas_call(
        functools.partial(_ffn_down_kernel, row0=row0, tpb=tpb),
        grid=(N // tm, nd),
        in_specs=[pl.BlockSpec((tm, F), lambda i, j: (i, 0)),
                  pl.BlockSpec((tm, tn), lambda i, j: (i, j)),
                  pl.BlockSpec((nrow, tn), lambda i, j: (0, (k0 + 2) * nd + j)),
                  pl.BlockSpec((F, tn), lambda i, j: (0, j))] + cast_specs,
        out_specs=[pl.BlockSpec((tm, tn), lambda i, j: (i, j))] + cast_specs,
        out_shape=[jax.ShapeDtypeStruct((N, D), F32)] + cast_shapes,
        compiler_params=_params("arbitrary", "arbitrary"),
        name="ffn_down",
    )(a, x, mod, wd, *side_cast)
    return res[0], wd, res[1:]


def _head_norm(acc, g, cos, sin, head_dim):
    quarter = head_dim // 4
    out = []
    for hh in range(acc.shape[1] // head_dim):
        xh = acc[:, hh * head_dim:(hh + 1) * head_dim]
        ms = jnp.mean(xh * xh, axis=-1, keepdims=True)
        xh = (xh * lax.rsqrt(ms + NORM_EPS)) * g
        if cos is not None:
            lane = lax.broadcasted_iota(jnp.int32, xh.shape, 1)
            partner = jnp.where((lane % (2 * quarter)) < quarter,
                                pltpu.roll(xh, head_dim - quarter, 1),
                                pltpu.roll(xh, quarter, 1))
            xh = xh * cos + partner * sin
        out.append(xh.astype(BF16))
    return out[0] if len(out) == 1 else jnp.concatenate(out, axis=1)


def _proj_kernel(x_ref, ng_ref, sh_ref, sc_ref, w_ref, cos_ref, sin_ref, kg_ref,
                 k_ref, v_ref, u_ref, *rest, bounds, rope, head_dim, sub, row0, tpb):
    h_ref = rest[-1]
    j = pl.program_id(1)
    tm = x_ref.shape[0]
    jv, ju, jq, jg = bounds
    assert jv == 1

    @pl.when(j == 0)
    def _():
        for r in range(0, tm, sub):
            rows = slice(r, r + sub)
            h = _norm_mod(x_ref[rows, :], ng_ref[...], _mod_row(sh_ref, row0, tpb),
                          _mod_row(sc_ref, row0, tpb))
            h_ref[rows, :] = h
            acc = jnp.dot(h, w_ref[...], preferred_element_type=F32)
            cos = cos_ref[rows, :] if rope else None
            sin = sin_ref[rows, :] if rope else None
            k_ref[rows, :] = _head_norm(acc, kg_ref[...], cos, sin, head_dim)

    @pl.when(j > 0)
    def _():
        acc = jnp.dot(h_ref[...], w_ref[...], preferred_element_type=F32)

        @pl.when(j < ju)
        def _():
            v_ref[...] = acc.astype(BF16)

        @pl.when((j >= ju) & (j < jq))
        def _():
            u_ref[...] = acc

        if len(rest) == 3:
            q_ref, gate_ref = rest[0], rest[1]

            @pl.when((j >= jq) & (j < jg))
            def _():
                q_ref[...] = acc

            @pl.when(j >= jg)
            def _():
                gate_ref[...] = acc.astype(BF16)


def _proj(x, ng, mod, k0, row0, w, cos, sin, kg, rows_per_seq, widths, latent):
    N, D = x.shape
    kv_w, ssm_w, attn_w, gate_w = widths
    head_dim = kg.shape[1]
    tn = kv_w
    assert tn % LANES == 0 and ssm_w % tn == 0 and attn_w % tn == 0 and gate_w % tn == 0
    jv, ju = 1, 2
    jq = ju + ssm_w // tn
    jg = jq + attn_w // tn
    nj = jg + gate_w // tn if latent else jq
    tm = _tile(rows_per_seq, 1024)
    tpb = rows_per_seq // tm
    nrow = mod.shape[0]
    tab = pl.BlockSpec((tm, head_dim), lambda i, j: (i % tpb, 0))
    hvec = pl.BlockSpec((1, head_dim), lambda i, j: (0, 0))

    def out_spec(j0, nblk):
        return pl.BlockSpec((tm, tn), lambda i, j: (i, jnp.clip(j - j0, 0, nblk - 1)))

    out_specs = [out_spec(0, 1), out_spec(jv, 1), out_spec(ju, jq - ju)]
    out_shape = [jax.ShapeDtypeStruct((N, kv_w), BF16), jax.ShapeDtypeStruct((N, kv_w), BF16),
                 jax.ShapeDtypeStruct((N, ssm_w), F32)]
    if latent:
        out_specs += [out_spec(jq, jg - jq), out_spec(jg, nj - jg)]
        out_shape += [jax.ShapeDtypeStruct((N, attn_w), F32), jax.ShapeDtypeStruct((N, gate_w), BF16)]
    return pl.pallas_call(
        functools.partial(_proj_kernel, bounds=(jv, ju, jq, jg), rope=latent, head_dim=head_dim,
                          sub=_tile(tm, 512), row0=row0, tpb=tpb),
        grid=(N // tm, nj),
        in_specs=[pl.BlockSpec((tm, D), lambda i, j: (i, 0)),
                  pl.BlockSpec((1, D), lambda i, j: (0, 0)),
                  pl.BlockSpec((nrow, D), lambda i, j: (0, k0)),
                  pl.BlockSpec((nrow, D), lambda i, j: (0, k0 + 1)),
                  pl.BlockSpec((D, tn), lambda i, j: (0, j)),
                  tab, tab, hvec],
        out_specs=out_specs,
        out_shape=out_shape,
        scratch_shapes=[pltpu.VMEM((tm, D), BF16)],
        compiler_params=_params("arbitrary", "arbitrary"),
        name="proj_latent" if latent else "proj_context",
    )(x, ng, mod, mod, w, cos, sin, kg)


def _rope_tables(L, head_dim):
    t = np.arange(L)
    row = (t // GRID_W).astype(np.float64)
    col = (t % GRID_W).astype(np.float64)
    half = head_dim // 4
    inv_freq = ROPE_THETA ** (-np.arange(half, dtype=np.float64) / half)
    ar = row[:, None] * inv_freq
    ac = col[:, None] * inv_freq
    cos = np.concatenate([np.cos(ar), np.cos(ar), np.cos(ac), np.cos(ac)], axis=-1)
    sin = np.concatenate([-np.sin(ar), np.sin(ar), -np.sin(ac), np.sin(ac)], axis=-1)
    return jnp.asarray(cos, F32), jnp.asarray(sin, F32)


def _attn_kernel(q_ref, cos_ref, sin_ref, qn_ref, cosn_ref, sinn_ref, qg_ref,
                 kc_ref, k_ref, vc_ref, v_ref, o_ref, kall_ref, vext_ref, qs_ref, *, head_dim):
    lc = kc_ref.shape[0]
    i = pl.program_id(2)

    @pl.when(i == 0)
    def _():
        kall_ref[0:lc, :] = kc_ref[...]
        kall_ref[lc:, :] = k_ref[...]
        vext_ref[0:lc, 0:head_dim] = vc_ref[...]
        vext_ref[lc:, 0:head_dim] = v_ref[...]
        vext_ref[:, head_dim:] = jnp.ones((vext_ref.shape[0], head_dim), BF16)
        qs_ref[0] = _head_norm(q_ref[...], qg_ref[...], cos_ref[...], sin_ref[...], head_dim)

    k = kall_ref[...]
    vext = vext_ref[...]
    n_rep = q_ref.shape[1] // head_dim
    heads = [slice(r * head_dim, (r + 1) * head_dim) for r in range(n_rep)]

    def scores(sl):
        return lax.dot_general(qs_ref[i % 2, :, sl], k, (((1,), (1,)), ((), ())),
                               preferred_element_type=F32)

    s_next = scores(heads[0])
    for r, sl in enumerate(heads):
        s = s_next
        if r + 1 < n_rep:
            s_next = scores(heads[r + 1])
        qs_ref[(i + 1) % 2, :, sl] = _head_norm(qn_ref[:, sl], qg_ref[...], cosn_ref[...],
                                                sinn_ref[...], head_dim)
        m = jnp.max(s, axis=-1, keepdims=True)
        p = jnp.exp2(s - m).astype(BF16)
        oe = jnp.dot(p, vext, preferred_element_type=F32)
        o_ref[:, sl] = (oe[:, :head_dim] / oe[:, head_dim:]).astype(o_ref.dtype)


def _attention(q, cos, sin, qg, kc, k, vc, v, B, head_dim):
    N, W = q.shape
    L, Lc = N // B, kc.shape[0] // B
    kvh = k.shape[1] // head_dim
    gw = W // kvh
    tq = _tile(L, 512)
    tpb = L // tq
    kv_lat = pl.BlockSpec((L, head_dim), lambda b, h, i: (b, h))
    kv_ctx = pl.BlockSpec((Lc, head_dim), lambda b, h, i: (b, h))
    nxt = lambda i: jnp.minimum(i + 1, tpb - 1)
    tab = pl.BlockSpec((tq, head_dim), lambda b, h, i: (i, 0))
    tab_next = pl.BlockSpec((tq, head_dim), lambda b, h, i: (nxt(i), 0))
    return pl.pallas_call(
        functools.partial(_attn_kernel, head_dim=head_dim),
        grid=(B, kvh, tpb),
        in_specs=[pl.BlockSpec((tq, gw), lambda b, h, i: (b * tpb + i, h)), tab, tab,
                  pl.BlockSpec((tq, gw), lambda b, h, i: (b * tpb + nxt(i), h)), tab_next, tab_next,
                  pl.BlockSpec((1, head_dim), lambda b, h, i: (0, 0)),
                  kv_ctx, kv_lat, kv_ctx, kv_lat],
        out_specs=pl.BlockSpec((tq, gw), lambda b, h, i: (b * tpb + i, h)),
        out_shape=jax.ShapeDtypeStruct((N, W), BF16),
        scratch_shapes=[pltpu.VMEM((L + Lc, head_dim), BF16),
                        pltpu.VMEM((L + Lc, 2 * head_dim), BF16),
                        pltpu.VMEM((2, tq, gw), BF16)],
        compiler_params=_params("arbitrary", "arbitrary", "arbitrary"),
        name="attention",
    )(q, cos, sin, q, cos, sin, qg, kc, k, vc, v)


def _ssm_operators(a_re, a_im, log_dt, b_re, b_im, c_re, c_im):
    T = SSM_CHUNK
    a_re, a_im = a_re.astype(F32), a_im.astype(F32)
    dt = jnp.exp(log_dt.astype(F32))[..., None]
    mag = jnp.exp(a_re * dt)
    lr = mag * jnp.cos(a_im * dt)
    li = mag * jnp.sin(a_im * dt)
    den = a_re * a_re + a_im * a_im
    cr = ((lr - 1.0) * a_re + li * a_im) / den
    ci = (li * a_re - (lr - 1.0) * a_im) / den
    pr, pi = [jnp.ones_like(lr)], [jnp.zeros_like(lr)]
    for _ in range(T):
        pr.append(pr[-1] * lr - pi[-1] * li)
        pi.append(pr[-2] * li + pi[-1] * lr)
    pw_r, pw_i = jnp.stack(pr), jnp.stack(pi)
    _, G, P, E = b_re.shape
    bt_r = jnp.swapaxes(b_re.astype(F32), 2, 3)
    bt_i = jnp.swapaxes(b_im.astype(F32), 2, 3)
    bb_r = cr[:, :, None, :] * bt_r - ci[:, :, None, :] * bt_i
    bb_i = cr[:, :, None, :] * bt_i + ci[:, :, None, :] * bt_r
    c_r, c_i = c_re.astype(F32), c_im.astype(F32)

    def lag_table(pw, fwd_lags, bwd_lags, reps):
        tab = jnp.stack([jnp.moveaxis(pw[fwd_lags, 0], 0, 1), jnp.moveaxis(pw[bwd_lags, 1], 0, 1)])
        return jnp.concatenate([tab] * reps, axis=-1)

    steps = np.arange(T)
    lags = np.arange(T + 1)
    p1 = lag_table(pw_r, T - 1 - steps, steps, 4)
    p2 = lag_table(pw_i, T - 1 - steps, steps, 4)
    q1 = lag_table(pw_r, lags, T - lags, 2)
    q2 = lag_table(pw_i, lags, T - lags, 2)
    bb1 = jnp.concatenate([bb_r, bb_i, bb_i, bb_r], axis=-1)
    bb2 = jnp.concatenate([-bb_i, bb_r, bb_r, -bb_i], axis=-1)
    cc1 = jnp.concatenate([c_r, -c_i], axis=-1)
    cc2 = jnp.concatenate([-c_i, -c_r], axis=-1)
    ar, ai = pw_r[T], pw_i[T]
    dec = jnp.stack([jnp.concatenate([ar, ar], -1), jnp.concatenate([-ai, ai], -1)], axis=2)
    return p1, p2, bb1, bb2, q1, q2, cc1, cc2, dec


def _state_in_operator(p1_ref, p2_ref, bb1_ref, bb2_ref, d, g):
    bb1, bb2 = bb1_ref[d, g], bb2_ref[d, g]
    rows = [p1_ref[d, g, s:s + 1, :] * bb1 + p2_ref[d, g, s:s + 1, :] * bb2
            for s in range(SSM_CHUNK)]
    return jnp.concatenate(rows, axis=0).astype(BF16)


def _readout_table(q1_ref, q2_ref, cc1_ref, cc2_ref, d, g):
    cc1, cc2 = cc1_ref[d, g], cc2_ref[d, g]
    rows = [q1_ref[d, g, j:j + 1, :] * cc1 + q2_ref[d, g, j:j + 1, :] * cc2
            for j in range(SSM_CHUNK + 1)]
    return jnp.concatenate(rows, axis=0)


def _intra_operator(wf, wb, e):
    T = SSM_CHUNK
    t_of_lane = lax.broadcasted_iota(jnp.int32, wf.shape, 1) // e
    rows = []
    for s in range(T):
        f = wf if s == 0 else jnp.where(t_of_lane >= s, pltpu.roll(wf, s * e, 1), 0.0)
        b = wb if s == T - 1 else jnp.where(t_of_lane <= s, pltpu.roll(wb, (s + 1) * e, 1), 0.0)
        rows.append(f + b)
    return jnp.concatenate(rows, axis=0).astype(BF16)


def _row_pitch(n):
    p = -(-n // 8)
    return 8 * (p if p % 2 else p + 1)


def _ssm_kernel(u_ref, uc_ref, d_ref, p1_ref, p2_ref, bb1_ref, bb2_ref, q1_ref, q2_ref,
                cc1_ref, cc2_ref, dec_ref, y_ref, z_ref, s1_ref, s2_ref, hp_ref, *, batch, e):
    T = SSM_CHUNK
    gpb = dec_ref.shape[1]
    rp = z_ref.shape[1]
    n_lat, n_ctx = u_ref.shape[0] // T, uc_ref.shape[0] // T
    ncl, ncc = n_lat // batch, n_ctx // batch
    pl_, pc_ = _row_pitch(ncl), _row_pitch(ncc)
    lat0 = batch * pc_
    r = lat0 + batch * pl_
    lanes = u_ref.shape[1]
    pw = hp_ref.shape[3]

    def padded(rows, n, pitch):
        if pitch == n:
            return [rows]
        out = []
        for b in range(batch):
            out += [rows[b * n:(b + 1) * n, :], jnp.zeros((pitch - n, lanes), F32)]
        return out

    xt = []
    for s in range(T):
        parts = (padded(uc_ref[pl.ds(s, n_ctx, stride=T), :], ncc, pc_)
                 + padded(u_ref[pl.ds(s, n_lat, stride=T), :], ncl, pl_))
        if rp > r:
            parts.append(jnp.zeros((rp - r, lanes), F32))
        xt.append(jnp.concatenate(parts, axis=0).astype(BF16).T)
    for g in range(gpb):
        zt = jnp.concatenate([xt[s][g * e:(g + 1) * e, :] for s in range(T)], axis=0)
        z_ref[g] = zt.T
    for d in range(2):
        for g in range(gpb):
            st = jnp.dot(z_ref[g], _state_in_operator(p1_ref, p2_ref, bb1_ref, bb2_ref, d, g),
                         preferred_element_type=F32)
            s1_ref[d, g] = st[:, :pw]
            s2_ref[d, g] = st[:, pw:]

    def sweep(base, count, stride, carry):
        def body(it, carry):
            out = []
            for d in range(2):
                n = it if d == 0 else count - 1 - it
                rows = pl.ds(base + n, batch, stride=stride)
                for g in range(gpb):
                    h1, h2 = carry[2 * (d * gpb + g)], carry[2 * (d * gpb + g) + 1]
                    hp_ref[d, g, rows, :] = h1
                    a1 = dec_ref[d, g, 0:1, :]
                    a2 = dec_ref[d, g, 1:2, :]
                    out.append(a1 * h1 + a2 * h2 + s1_ref[d, g, rows, :])
                    out.append(a1 * h2 - a2 * h1 + s2_ref[d, g, rows, :])
            return tuple(out)
        return lax.fori_loop(0, count, body, carry)

    for d in range(2):
        for g in range(gpb):
            for b in range(batch):
                if pc_ > ncc:
                    hp_ref[d, g, b * pc_ + ncc:(b + 1) * pc_, :] = jnp.zeros((pc_ - ncc, pw), F32)
                if pl_ > ncl:
                    hp_ref[d, g, lat0 + b * pl_ + ncl:lat0 + (b + 1) * pl_, :] = jnp.zeros((pl_ - ncl, pw), F32)
            if rp > r:
                hp_ref[d, g, r:rp, :] = jnp.zeros((rp - r, pw), F32)
    zero = jnp.zeros((batch, pw), F32)
    carry = sweep(0, ncc, pc_, tuple([zero] * (4 * gpb)))
    sweep(lat0, ncl, pl_, carry)
    nt = (((1,), (1,)), ((), ()))
    te = T * e
    yt = []
    for g in range(gpb):
        rf = _readout_table(q1_ref, q2_ref, cc1_ref, cc2_ref, 0, g)
        rb = _readout_table(q1_ref, q2_ref, cc1_ref, cc2_ref, 1, g)
        wf = lax.dot_general(bb1_ref[0, g][:, :pw], rf[:te], nt, precision=lax.Precision.HIGHEST,
                             preferred_element_type=F32)
        wb = lax.dot_general(bb1_ref[1, g][:, :pw], rb[e:], nt, precision=lax.Precision.HIGHEST,
                             preferred_element_type=F32)
        y = jnp.dot(z_ref[g], _intra_operator(wf, wb, e), preferred_element_type=F32)
        y += lax.dot_general(hp_ref[0, g].astype(BF16), rf[e:].astype(BF16), nt,
                             preferred_element_type=F32)
        y += lax.dot_general(hp_ref[1, g].astype(BF16), rb[:te].astype(BF16), nt,
                             preferred_element_type=F32)
        yt.append(y.T)
    dvec = d_ref[...]
    for t in range(T):
        blk = jnp.concatenate([yt[g][t * e:(t + 1) * e, :] for g in range(gpb)], axis=0).T
        for b in range(batch):
            rows = pl.ds(b * ncl * T + t, ncl, stride=T)
            y_ref[rows, :] = (blk[lat0 + b * pl_:lat0 + b * pl_ + ncl, :] + dvec * u_ref[rows, :])


def _s5(u, uc, d, ops, batch):
    dec = ops[-1]
    N, W = u.shape
    Nc = uc.shape[0]
    G = dec.shape[1]
    E = W // G
    gpb = LANES // E
    TE = SSM_CHUNK * E
    P2 = dec.shape[-1]
    assert P2 == LANES and W % LANES == 0

    def table(a):
        return pl.BlockSpec((2, gpb) + a.shape[2:], lambda j: (0, j, 0, 0))
    r = batch * (_row_pitch(N // SSM_CHUNK // batch) + _row_pitch(Nc // SSM_CHUNK // batch))
    rp = -(-r // LANES) * LANES
    return pl.pallas_call(
        functools.partial(_ssm_kernel, batch=batch, e=E),
        grid=(W // LANES,),
        in_specs=[pl.BlockSpec((N, LANES), lambda j: (0, j)),
                  pl.BlockSpec((Nc, LANES), lambda j: (0, j)),
                  pl.BlockSpec((1, LANES), lambda j: (0, j))] + [table(a) for a in ops],
        out_specs=pl.BlockSpec((N, LANES), lambda j: (0, j)),
        out_shape=jax.ShapeDtypeStruct((N, W), F32),
        scratch_shapes=[pltpu.VMEM((gpb, rp, TE), BF16),
                        pltpu.VMEM((2, gpb, rp, P2), F32),
                        pltpu.VMEM((2, gpb, rp, P2), F32),
                        pltpu.VMEM((2, gpb, rp, P2), F32)],
        compiler_params=_params("arbitrary"),
        name="s5",
    )(u, uc, d, *ops)


def _mixer_out_kernel(attn_ref, y_ref, ga_ref, gs_ref, x_ref, g2_ref, wglu_ref, bglu_ref,
                      wa_ref, ws_ref, wo_ref, o_ref, y2_ref, m_ref, *, sub, tpb):
    j = pl.program_id(1)
    tm, tn = o_ref.shape
    nd = m_ref.shape[1] // tn
    half = tn // 2

    def merge(rows, attn, y2):
        for c in (0, half):
            pa = jnp.dot(attn, wa_ref[:, c:c + half], preferred_element_type=F32)
            ps = jnp.dot(y2, ws_ref[:, c:c + half], preferred_element_type=F32)
            merged = (jax.nn.sigmoid(ga_ref[rows, c:c + half].astype(F32)) * pa
                      + jax.nn.sigmoid(gs_ref[rows, c:c + half].astype(F32)) * ps)
            m_ref[rows, pl.ds(pl.multiple_of(j * tn + c, half), half)] = merged.astype(BF16)

    @pl.when(j == 0)
    def _():
        for r in range(0, tm, sub):
            rows = slice(r, r + sub)
            y = jax.nn.gelu(y_ref[rows, :])
            z = jnp.dot(y.astype(BF16), wglu_ref[...], preferred_element_type=F32) + bglu_ref[...]
            y2 = (y * jax.nn.sigmoid(z)).astype(BF16)
            y2_ref[rows, :] = y2
            merge(rows, attn_ref[rows, :], y2)

    @pl.when((j > 0) & (j < nd))
    def _():
        merge(slice(None), attn_ref[...], y2_ref[...])

    @pl.when(j >= nd)
    def _():
        for c in (0, half):
            col = pl.ds(pl.multiple_of((j - nd) * tn + c, half), half)
            acc = jnp.dot(m_ref[...], wo_ref[:, col], preferred_element_type=F32)
            g2 = _mod_row(g2_ref, 0, tpb)
            o_ref[:, c:c + half] = x_ref[:, c:c + half] + g2[:, c:c + half] * acc


def _mixer_out(attn, y_ssm, gate, x, mod, k_gate, w_glu, b_glu, wa, ws, wo, rows_per_mod):
    N, D = x.shape
    WA, WS = attn.shape[1], y_ssm.shape[1]
    tm = _tile(rows_per_mod, 1024)
    tn = _tile(D, 512, 2 * LANES)
    nd = D // tn
    tpb = rows_per_mod // tm
    lo = lambda j: jnp.minimum(j, nd - 1)
    hi = lambda j: jnp.maximum(j - nd, 0)
    return pl.pallas_call(
        functools.partial(_mixer_out_kernel, sub=_tile(tm, 512), tpb=tpb),
        grid=(N // tm, 2 * nd),
        in_specs=[pl.BlockSpec((tm, WA), lambda i, j: (i, 0)),
                  pl.BlockSpec((tm, WS), lambda i, j: (i, 0)),
                  pl.BlockSpec((tm, tn), lambda i, j: (i, lo(j))),
                  pl.BlockSpec((tm, tn), lambda i, j: (i, nd + lo(j))),
                  pl.BlockSpec((tm, tn), lambda i, j: (i, hi(j))),
                  pl.BlockSpec((mod.shape[0], tn), lambda i, j: (0, k_gate * nd + hi(j))),
                  pl.BlockSpec((WS, WS), lambda i, j: (0, 0), pipeline_mode=pl.Buffered(1)),
                  pl.BlockSpec((1, WS), lambda i, j: (0, 0)),
                  pl.BlockSpec((WA, tn), lambda i, j: (0, lo(j))),
                  pl.BlockSpec((WS, tn), lambda i, j: (0, lo(j))),
                  pl.BlockSpec((D, D), lambda i, j: (0, 0), pipeline_mode=pl.Buffered(1))],
        out_specs=pl.BlockSpec((tm, tn), lambda i, j: (i, hi(j))),
        out_shape=jax.ShapeDtypeStruct((N, D), F32),
        scratch_shapes=[pltpu.VMEM((tm, WS), BF16), pltpu.VMEM((tm, D), BF16)],
        compiler_params=_params("arbitrary", "arbitrary"),
        name="mixer_out",
    )(attn, y_ssm, gate, gate, x, mod, w_glu, b_glu, wa, ws, wo)


def kernel(x, c, ctx, c_ctx, w_mod, b_mod, norm_g, w_ffn1_gate, w_ffn1_up, w_ffn1_down, w_in, q_norm_g, k_norm_g, ssm_a_re, ssm_a_im, ssm_log_dt, ssm_b_re, ssm_b_im, ssm_c_re, ssm_c_im, ssm_d, w_glu, b_glu, w_br_attn, w_br_ssm, w_out, w_ffn2_gate, w_ffn2_up, w_ffn2_down):
    B, L, D = x.shape
    Lc = ctx.shape[1]
    assert w_mod.shape[0] == 1, "only the single (last) layer configuration is implemented"
    assert w_mod.shape[2] == N_MOD * D
    hd = q_norm_g.shape[1]
    ssm_w = w_glu.shape[1]
    attn_w = w_br_attn.shape[1]
    kv_w = attn_w // Q_PER_KV
    widths = (kv_w, ssm_w, attn_w, 2 * D)
    assert L % GRID_W == 0 and L % SSM_CHUNK == 0 and Lc % SSM_CHUNK == 0 and B <= 7
    N, Nc = B * L, B * Lc
    l = 0

    cc = jnp.zeros((8, D), F32).at[:B].set(c).at[B].set(c_ctx)
    mod = _modulation(cc, w_mod[l], b_mod[l][None, :])
    ng = norm_g[l][:, None, :]

    x2 = x.reshape(N, D)
    c2 = ctx.reshape(Nc, D)

    x2, wd1, (w_in_b,) = _ffn(x2, ng[0], mod, 0, 0, w_ffn1_gate[l], w_ffn1_up[l],
                              w_ffn1_down[l], L, side_cast=(w_in[l],))
    c2, _, _ = _ffn(c2, ng[0], mod, 0, B, w_ffn1_gate[l], w_ffn1_up[l], wd1, Nc)

    cos, sin = _rope_tables(L, hd)
    kg = k_norm_g[l][None, :]
    qg = q_norm_g[l][None, :] * (hd ** -0.5 * math.log2(math.e))
    k, v, u, q, gate = _proj(x2, ng[1], mod, 3, 0, w_in_b, cos, sin, kg, L, widths, True)
    kc, vc, uc = _proj(c2, ng[1], mod, 3, B, w_in_b, cos, sin, kg, Nc, widths, False)

    attn = _attention(q, cos, sin, qg, kc, k, vc, v, B, hd)
    ops = _ssm_operators(ssm_a_re[l], ssm_a_im[l], ssm_log_dt[l], ssm_b_re[l], ssm_b_im[l],
                         ssm_c_re[l], ssm_c_im[l])
    y_ssm = _s5(u, uc, ssm_d[l][None, :], ops, B)

    x2 = _mixer_out(attn, y_ssm, gate, x2, mod, 5, w_glu[l].astype(BF16), b_glu[l][None, :],
                    w_br_attn[l].astype(BF16), w_br_ssm[l].astype(BF16), w_out[l].astype(BF16), L)

    x2, _, _ = _ffn(x2, ng[2], mod, 6, 0, w_ffn2_gate[l], w_ffn2_up[l], w_ffn2_down[l], L)
    return x2.reshape(B, L, D)
```

```python
import functools
import math

import jax
import jax.numpy as jnp
import numpy as np
from jax import lax
from jax.experimental import pallas as pl
from jax.experimental.pallas import tpu as pltpu

F32 = jnp.float32
BF16 = jnp.bfloat16

GRID_W = 64
ROPE_THETA = 10000.0
NORM_EPS = 1e-6
N_MOD = 9
Q_PER_KV = 4
SSM_CHUNK = 16
LANES = 128
SUBLANES = 8
BF16_ROWS = 16
MXU_WIDTH = 256
VMEM_LIMIT_BYTES = 58 * 1024 * 1024
TOKEN_TILE = 1024
COL_TILE = 512
ROW_BLOCK = 512
Q_TILE = 512
MOD_COL_TILE = 2048


def _tile(n, pref, mult=SUBLANES):
    if n <= pref:
        return n
    for t in range(pref, 0, -1):
        if n % t == 0 and t % mult == 0:
            return t
    return n


def _params(*sem):
    return pltpu.CompilerParams(dimension_semantics=sem, vmem_limit_bytes=VMEM_LIMIT_BYTES)


def _as_bf16(w):
    return w if w.dtype == BF16 else w.astype(BF16)


def _norm_mod(x, g, sh, sc):
    ms = jnp.mean(x * x, axis=-1, keepdims=True)
    h = (x * lax.rsqrt(ms + NORM_EPS)) * g
    return (h * (1.0 + sc) + sh).astype(BF16)


def _mod_kernel(c_ref, w_ref, b_ref, o_ref):
    c = c_ref[...]
    s = c * jax.nn.sigmoid(c)
    o_ref[...] = jnp.dot(s.astype(BF16), w_ref[...].astype(BF16),
                         preferred_element_type=F32) + b_ref[...]


def _modulation(cc, w_mod, b_mod):
    D, ncols = w_mod.shape
    tn = _tile(ncols, MOD_COL_TILE, LANES)
    return pl.pallas_call(
        _mod_kernel,
        grid=(ncols // tn,),
        in_specs=[pl.BlockSpec((cc.shape[0], D), lambda j: (0, 0)),
                  pl.BlockSpec((D, tn), lambda j: (0, j)),
                  pl.BlockSpec((1, tn), lambda j: (0, j))],
        out_specs=pl.BlockSpec((cc.shape[0], tn), lambda j: (0, j)),
        out_shape=jax.ShapeDtypeStruct((cc.shape[0], ncols), F32),
        compiler_params=_params("arbitrary"),
        name="modulation",
    )(cc, w_mod, b_mod)


def _mod_row(ref, row0, tiles_per_row):
    return ref[pl.ds(row0 + pl.program_id(0) // tiles_per_row, 1), :]


def _ffn_act_kernel(x_ref, ng_ref, sh_ref, sc_ref, wg_ref, wu_ref, *rest, sub, row0, tpb):
    if len(rest) == 4:
        wd_ref, a_ref, wdb_ref, h_ref = rest

        @pl.when(pl.program_id(0) == 0)
        def _():
            wdb_ref[...] = wd_ref[...].astype(BF16)
    else:
        a_ref, h_ref = rest

    half = a_ref.shape[1] // 2
    j = pl.program_id(1)

    def emit(row_blocks):
        halves = [(c, _as_bf16(wg_ref[:, c:c + half]), _as_bf16(wu_ref[:, c:c + half]))
                  for c in (0, half)]
        for rows, make_h in row_blocks:
            h = make_h()
            for c, wg, wu in halves:
                g = jnp.dot(h, wg, preferred_element_type=F32)
                u = jnp.dot(h, wu, preferred_element_type=F32)
                a_ref[rows, c:c + half] = ((g * jax.nn.sigmoid(g)) * u).astype(a_ref.dtype)

    def normed(rows):
        def make_h():
            h = _norm_mod(x_ref[rows, :], ng_ref[...], _mod_row(sh_ref, row0, tpb),
                          _mod_row(sc_ref, row0, tpb))
            h_ref[rows, :] = h
            return h
        return make_h

    @pl.when(j == 0)
    def _():
        emit([(slice(r, r + sub), normed(slice(r, r + sub))) for r in range(0, x_ref.shape[0], sub)])

    @pl.when(j > 0)
    def _():
        emit([(slice(None), lambda: h_ref[...])])


def _side_cast_specs(arrays, nsteps, step_of):
    specs, shapes = [], []
    for a in arrays:
        rows = a.shape[0] // nsteps
        assert a.shape[0] % nsteps == 0 and rows % BF16_ROWS == 0, (a.shape, nsteps)
        specs.append(pl.BlockSpec((rows, a.shape[1]), lambda *ids: (step_of(*ids), 0)))
        shapes.append(jax.ShapeDtypeStruct(a.shape, BF16))
    return specs, shapes


def _side_cast(src_refs, dst_refs):
    for src, dst in zip(src_refs, dst_refs):
        dst[...] = src[...].astype(BF16)


def _ffn_down_kernel(a_ref, x_ref, gate_ref, wd_ref, *rest, row0, tpb):
    n_cast = len(rest) // 2
    o_ref = rest[n_cast]
    _side_cast(rest[:n_cast], rest[n_cast + 1:])
    half = o_ref.shape[1] // 2
    half_gate = 0.5 * _mod_row(gate_ref, row0, tpb)
    for c in (0, half):
        acc = jnp.dot(a_ref[...], wd_ref[:, c:c + half], preferred_element_type=F32)
        o_ref[:, c:c + half] = x_ref[:, c:c + half] + half_gate[:, c:c + half] * acc


def _ffn(x, ng, mod, k0, row0, wg, wu, wd, rows_per_mod, side_cast=()):
    N, D = x.shape
    F = wg.shape[1]
    tm = _tile(rows_per_mod, TOKEN_TILE)
    tf = _tile(F, COL_TILE, MXU_WIDTH)
    tn = _tile(D, COL_TILE, MXU_WIDTH)
    tpb = rows_per_mod // tm
    nj = F // tf
    nrow = mod.shape[0]
    in_specs = [pl.BlockSpec((tm, D), lambda i, j: (i, 0)),
                pl.BlockSpec((1, D), lambda i, j: (0, 0)),
                pl.BlockSpec((nrow, D), lambda i, j: (0, k0)),
                pl.BlockSpec((nrow, D), lambda i, j: (0, k0 + 1)),
                pl.BlockSpec((D, tf), lambda i, j: (0, j)),
                pl.BlockSpec((D, tf), lambda i, j: (0, j))]
    out_specs = [pl.BlockSpec((tm, tf), lambda i, j: (i, j))]
    out_shape = [jax.ShapeDtypeStruct((N, F), BF16)]
    operands = [x, ng, mod, mod, wg, wu]
    if wd.dtype != BF16:
        wd_rows = pl.BlockSpec((tf, D), lambda i, j: (jnp.where(i == 0, j, nj - 1), 0))
        in_specs.append(wd_rows)
        out_specs.append(wd_rows)
        out_shape.append(jax.ShapeDtypeStruct((F, D), BF16))
        operands.append(wd)
    res = pl.pallas_call(
        functools.partial(_ffn_act_kernel, sub=_tile(tm, ROW_BLOCK), row0=row0, tpb=tpb),
        grid=(N // tm, nj),
        in_specs=in_specs,
        out_specs=out_specs,
        out_shape=out_shape,
        scratch_shapes=[pltpu.VMEM((tm, D), BF16)],
        compiler_params=_params("arbitrary", "arbitrary"),
        name="ffn_act",
    )(*operands)
    a = res[0]
    if wd.dtype != BF16:
        wd = res[1]
    nd = D // tn
    cast_specs, cast_shapes = _side_cast_specs(side_cast, (N // tm) * nd, lambda i, j: i * nd + j)
    res = pl.pallas_call(
        functools.partial(_ffn_down_kernel, row0=row0, tpb=tpb),
        grid=(N // tm, nd),
        in_specs=[pl.BlockSpec((tm, F), lambda i, j: (i, 0)),
                  pl.BlockSpec((tm, tn), lambda i, j: (i, j)),
                  pl.BlockSpec((nrow, tn), lambda i, j: (0, (k0 + 2) * nd + j)),
                  pl.BlockSpec((F, tn), lambda i, j: (0, j))] + cast_specs,
        out_specs=[pl.BlockSpec((tm, tn), lambda i, j: (i, j))] + cast_specs,
        out_shape=[jax.ShapeDtypeStruct((N, D), F32)] + cast_shapes,
        compiler_params=_params("arbitrary", "arbitrary"),
        name="ffn_down",
    )(a, x, mod, wd, *side_cast)
    return res[0], wd, res[1:]


def _head_norm(acc, g, cos, sin, head_dim):
    quarter = head_dim // 4
    out = []
    for hh in range(acc.shape[1] // head_dim):
        xh = acc[:, hh * head_dim:(hh + 1) * head_dim]
        ms = jnp.mean(xh * xh, axis=-1, keepdims=True)
        xh = (xh * lax.rsqrt(ms + NORM_EPS)) * g
        if cos is not None:
            lane = lax.broadcasted_iota(jnp.int32, xh.shape, 1)
            partner = jnp.where((lane % (2 * quarter)) < quarter,
                                pltpu.roll(xh, head_dim - quarter, 1),
                                pltpu.roll(xh, quarter, 1))
            xh = xh * cos + partner * sin
        out.append(xh.astype(BF16))
    return out[0] if len(out) == 1 else jnp.concatenate(out, axis=1)


def _proj_kernel(x_ref, ng_ref, sh_ref, sc_ref, w_ref, cos_ref, sin_ref, kg_ref,
                 k_ref, v_ref, u_ref, *rest, bounds, rope, head_dim, sub, row0, tpb):
    h_ref = rest[-1]
    j = pl.program_id(1)
    tm = x_ref.shape[0]
    jv, ju, jq, jg = bounds
    assert jv == 1

    @pl.when(j == 0)
    def _():
        for r in range(0, tm, sub):
            rows = slice(r, r + sub)
            h = _norm_mod(x_ref[rows, :], ng_ref[...], _mod_row(sh_ref, row0, tpb),
                          _mod_row(sc_ref, row0, tpb))
            h_ref[rows, :] = h
            acc = jnp.dot(h, w_ref[...], preferred_element_type=F32)
            cos = cos_ref[rows, :] if rope else None
            sin = sin_ref[rows, :] if rope else None
            k_ref[rows, :] = _head_norm(acc, kg_ref[...], cos, sin, head_dim)

    @pl.when(j > 0)
    def _():
        acc = jnp.dot(h_ref[...], w_ref[...], preferred_element_type=F32)

        @pl.when(j < ju)
        def _():
            v_ref[...] = acc.astype(BF16)

        @pl.when((j >= ju) & (j < jq))
        def _():
            u_ref[...] = acc

        if len(rest) == 3:
            q_ref, gate_ref = rest[0], rest[1]

            @pl.when((j >= jq) & (j < jg))
            def _():
                q_ref[...] = acc

            @pl.when(j >= jg)
            def _():
                gate_ref[...] = acc.astype(BF16)


def _proj(x, ng, mod, k0, row0, w, cos, sin, kg, rows_per_seq, widths, latent):
    N, D = x.shape
    kv_w, ssm_w, attn_w, gate_w = widths
    head_dim = kg.shape[1]
    tn = kv_w
    assert tn % LANES == 0 and ssm_w % tn == 0 and attn_w % tn == 0 and gate_w % tn == 0
    jv, ju = 1, 2
    jq = ju + ssm_w // tn
    jg = jq + attn_w // tn
    nj = jg + gate_w // tn if latent else jq
    tm = _tile(rows_per_seq, TOKEN_TILE)
    tpb = rows_per_seq // tm
    nrow = mod.shape[0]
    tab = pl.BlockSpec((tm, head_dim), lambda i, j: (i % tpb, 0))
    hvec = pl.BlockSpec((1, head_dim), lambda i, j: (0, 0))

    def out_spec(j0, nblk):
        return pl.BlockSpec((tm, tn), lambda i, j: (i, jnp.clip(j - j0, 0, nblk - 1)))

    out_specs = [out_spec(0, 1), out_spec(jv, 1), out_spec(ju, jq - ju)]
    out_shape = [jax.ShapeDtypeStruct((N, kv_w), BF16), jax.ShapeDtypeStruct((N, kv_w), BF16),
                 jax.ShapeDtypeStruct((N, ssm_w), F32)]
    if latent:
        out_specs += [out_spec(jq, jg - jq), out_spec(jg, nj - jg)]
        out_shape += [jax.ShapeDtypeStruct((N, attn_w), F32), jax.ShapeDtypeStruct((N, gate_w), BF16)]
    return pl.pallas_call(
        functools.partial(_proj_kernel, bounds=(jv, ju, jq, jg), rope=latent, head_dim=head_dim,
                          sub=_tile(tm, ROW_BLOCK), row0=row0, tpb=tpb),
        grid=(N // tm, nj),
        in_specs=[pl.BlockSpec((tm, D), lambda i, j: (i, 0)),
                  pl.BlockSpec((1, D), lambda i, j: (0, 0)),
                  pl.BlockSpec((nrow, D), lambda i, j: (0, k0)),
                  pl.BlockSpec((nrow, D), lambda i, j: (0, k0 + 1)),
                  pl.BlockSpec((D, tn), lambda i, j: (0, j)),
                  tab, tab, hvec],
        out_specs=out_specs,
        out_shape=out_shape,
        scratch_shapes=[pltpu.VMEM((tm, D), BF16)],
        compiler_params=_params("arbitrary", "arbitrary"),
        name="proj_latent" if latent else "proj_context",
    )(x, ng, mod, mod, w, cos, sin, kg)


def _rope_tables(L, head_dim):
    t = np.arange(L)
    row = (t // GRID_W).astype(np.float64)
    col = (t % GRID_W).astype(np.float64)
    half = head_dim // 4
    inv_freq = ROPE_THETA ** (-np.arange(half, dtype=np.float64) / half)
    ar = row[:, None] * inv_freq
    ac = col[:, None] * inv_freq
    cos = np.concatenate([np.cos(ar), np.cos(ar), np.cos(ac), np.cos(ac)], axis=-1)
    sin = np.concatenate([-np.sin(ar), np.sin(ar), -np.sin(ac), np.sin(ac)], axis=-1)
    return jnp.asarray(cos, F32), jnp.asarray(sin, F32)


def _attn_kernel(q_ref, cos_ref, sin_ref, qn_ref, cosn_ref, sinn_ref, qg_ref,
                 kc_ref, k_ref, vc_ref, v_ref, o_ref, kall_ref, vext_ref, qs_ref, *, head_dim):
    lc = kc_ref.shape[0]
    i = pl.program_id(2)

    @pl.when(i == 0)
    def _():
        kall_ref[0:lc, :] = kc_ref[...]
        kall_ref[lc:, :] = k_ref[...]
        vext_ref[0:lc, 0:head_dim] = vc_ref[...]
        vext_ref[lc:, 0:head_dim] = v_ref[...]
        vext_ref[:, head_dim:] = jnp.ones((vext_ref.shape[0], head_dim), BF16)
        qs_ref[0] = _head_norm(q_ref[...], qg_ref[...], cos_ref[...], sin_ref[...], head_dim)

    k = kall_ref[...]
    vext = vext_ref[...]
    n_rep = q_ref.shape[1] // head_dim
    heads = [slice(r * head_dim, (r + 1) * head_dim) for r in range(n_rep)]

    def scores(sl):
        return lax.dot_general(qs_ref[i % 2, :, sl], k, (((1,), (1,)), ((), ())),
                               preferred_element_type=F32)

    s_next = scores(heads[0])
    for r, sl in enumerate(heads):
        s = s_next
        if r + 1 < n_rep:
            s_next = scores(heads[r + 1])
        qs_ref[(i + 1) % 2, :, sl] = _head_norm(qn_ref[:, sl], qg_ref[...], cosn_ref[...],
                                                sinn_ref[...], head_dim)
        m = jnp.max(s, axis=-1, keepdims=True)
        p = jnp.exp2(s - m).astype(BF16)
        oe = jnp.dot(p, vext, preferred_element_type=F32)
        o_ref[:, sl] = (oe[:, :head_dim] / oe[:, head_dim:]).astype(o_ref.dtype)


def _attention(q, cos, sin, qg, kc, k, vc, v, B, head_dim):
    N, W = q.shape
    L, Lc = N // B, kc.shape[0] // B
    kvh = k.shape[1] // head_dim
    gw = W // kvh
    tq = _tile(L, Q_TILE)
    tpb = L // tq
    kv_lat = pl.BlockSpec((L, head_dim), lambda b, h, i: (b, h))
    kv_ctx = pl.BlockSpec((Lc, head_dim), lambda b, h, i: (b, h))
    nxt = lambda i: jnp.minimum(i + 1, tpb - 1)
    tab = pl.BlockSpec((tq, head_dim), lambda b, h, i: (i, 0))
    tab_next = pl.BlockSpec((tq, head_dim), lambda b, h, i: (nxt(i), 0))
    return pl.pallas_call(
        functools.partial(_attn_kernel, head_dim=head_dim),
        grid=(B, kvh, tpb),
        in_specs=[pl.BlockSpec((tq, gw), lambda b, h, i: (b * tpb + i, h)), tab, tab,
                  pl.BlockSpec((tq, gw), lambda b, h, i: (b * tpb + nxt(i), h)), tab_next, tab_next,
                  pl.BlockSpec((1, head_dim), lambda b, h, i: (0, 0)),
                  kv_ctx, kv_lat, kv_ctx, kv_lat],
        out_specs=pl.BlockSpec((tq, gw), lambda b, h, i: (b * tpb + i, h)),
        out_shape=jax.ShapeDtypeStruct((N, W), BF16),
        scratch_shapes=[pltpu.VMEM((L + Lc, head_dim), BF16),
                        pltpu.VMEM((L + Lc, 2 * head_dim), BF16),
                        pltpu.VMEM((2, tq, gw), BF16)],
        compiler_params=_params("arbitrary", "arbitrary", "arbitrary"),
        name="attention",
    )(q, cos, sin, q, cos, sin, qg, kc, k, vc, v)


def _ssm_operators(a_re, a_im, log_dt, b_re, b_im, c_re, c_im):
    T = SSM_CHUNK
    a_re, a_im = a_re.astype(F32), a_im.astype(F32)
    dt = jnp.exp(log_dt.astype(F32))[..., None]
    mag = jnp.exp(a_re * dt)
    lr = mag * jnp.cos(a_im * dt)
    li = mag * jnp.sin(a_im * dt)
    den = a_re * a_re + a_im * a_im
    cr = ((lr - 1.0) * a_re + li * a_im) / den
    ci = (li * a_re - (lr - 1.0) * a_im) / den
    pr, pi = [jnp.ones_like(lr)], [jnp.zeros_like(lr)]
    for _ in range(T):
        pr.append(pr[-1] * lr - pi[-1] * li)
        pi.append(pr[-2] * li + pi[-1] * lr)
    pw_r, pw_i = jnp.stack(pr), jnp.stack(pi)
    _, G, P, E = b_re.shape
    bt_r = jnp.swapaxes(b_re.astype(F32), 2, 3)
    bt_i = jnp.swapaxes(b_im.astype(F32), 2, 3)
    bb_r = cr[:, :, None, :] * bt_r - ci[:, :, None, :] * bt_i
    bb_i = cr[:, :, None, :] * bt_i + ci[:, :, None, :] * bt_r
    c_r, c_i = c_re.astype(F32), c_im.astype(F32)

    def lag_table(pw, fwd_lags, bwd_lags, reps):
        tab = jnp.stack([jnp.moveaxis(pw[fwd_lags, 0], 0, 1), jnp.moveaxis(pw[bwd_lags, 1], 0, 1)])
        return jnp.concatenate([tab] * reps, axis=-1)

    steps = np.arange(T)
    lags = np.arange(T + 1)
    p1 = lag_table(pw_r, T - 1 - steps, steps, 4)
    p2 = lag_table(pw_i, T - 1 - steps, steps, 4)
    q1 = lag_table(pw_r, lags, T - lags, 2)
    q2 = lag_table(pw_i, lags, T - lags, 2)
    bb1 = jnp.concatenate([bb_r, bb_i, bb_i, bb_r], axis=-1)
    bb2 = jnp.concatenate([-bb_i, bb_r, bb_r, -bb_i], axis=-1)
    cc1 = jnp.concatenate([c_r, -c_i], axis=-1)
    cc2 = jnp.concatenate([-c_i, -c_r], axis=-1)
    ar, ai = pw_r[T], pw_i[T]
    dec = jnp.stack([jnp.concatenate([ar, ar], -1), jnp.concatenate([-ai, ai], -1)], axis=2)
    return p1, p2, bb1, bb2, q1, q2, cc1, cc2, dec


def _state_in_operator(p1_ref, p2_ref, bb1_ref, bb2_ref, d, g):
    bb1, bb2 = bb1_ref[d, g], bb2_ref[d, g]
    rows = [p1_ref[d, g, s:s + 1, :] * bb1 + p2_ref[d, g, s:s + 1, :] * bb2
            for s in range(SSM_CHUNK)]
    return jnp.concatenate(rows, axis=0).astype(BF16)


def _readout_table(q1_ref, q2_ref, cc1_ref, cc2_ref, d, g):
    cc1, cc2 = cc1_ref[d, g], cc2_ref[d, g]
    rows = [q1_ref[d, g, j:j + 1, :] * cc1 + q2_ref[d, g, j:j + 1, :] * cc2
            for j in range(SSM_CHUNK + 1)]
    return jnp.concatenate(rows, axis=0)


def _intra_operator(wf, wb, e):
    T = SSM_CHUNK
    t_of_lane = lax.broadcasted_iota(jnp.int32, wf.shape, 1) // e
    rows = []
    for s in range(T):
        f = wf if s == 0 else jnp.where(t_of_lane >= s, pltpu.roll(wf, s * e, 1), 0.0)
        b = wb if s == T - 1 else jnp.where(t_of_lane <= s, pltpu.roll(wb, (s + 1) * e, 1), 0.0)
        rows.append(f + b)
    return jnp.concatenate(rows, axis=0).astype(BF16)


def _row_pitch(n):
    p = -(-n // SUBLANES)
    return SUBLANES * (p if p % 2 else p + 1)


def _ssm_kernel(u_ref, uc_ref, d_ref, p1_ref, p2_ref, bb1_ref, bb2_ref, q1_ref, q2_ref,
                cc1_ref, cc2_ref, dec_ref, y_ref, z_ref, s1_ref, s2_ref, hp_ref, *, batch, e):
    T = SSM_CHUNK
    gpb = dec_ref.shape[1]
    rp = z_ref.shape[1]
    n_lat, n_ctx = u_ref.shape[0] // T, uc_ref.shape[0] // T
    ncl, ncc = n_lat // batch, n_ctx // batch
    pl_, pc_ = _row_pitch(ncl), _row_pitch(ncc)
    lat0 = batch * pc_
    r = lat0 + batch * pl_
    lanes = u_ref.shape[1]
    pw = hp_ref.shape[3]

    def padded(rows, n, pitch):
        if pitch == n:
            return [rows]
        out = []
        for b in range(batch):
            out += [rows[b * n:(b + 1) * n, :], jnp.zeros((pitch - n, lanes), F32)]
        return out

    xt = []
    for s in range(T):
        parts = (padded(uc_ref[pl.ds(s, n_ctx, stride=T), :], ncc, pc_)
                 + padded(u_ref[pl.ds(s, n_lat, stride=T), :], ncl, pl_))
        if rp > r:
            parts.append(jnp.zeros((rp - r, lanes), F32))
        xt.append(jnp.concatenate(parts, axis=0).astype(BF16).T)
    for g in range(gpb):
        zt = jnp.concatenate([xt[s][g * e:(g + 1) * e, :] for s in range(T)], axis=0)
        z_ref[g] = zt.T
    for d in range(2):
        for g in range(gpb):
            st = jnp.dot(z_ref[g], _state_in_operator(p1_ref, p2_ref, bb1_ref, bb2_ref, d, g),
                         preferred_element_type=F32)
            s1_ref[d, g] = st[:, :pw]
            s2_ref[d, g] = st[:, pw:]

    def sweep(base, count, stride, carry):
        def body(it, carry):
            out = []
            for d in range(2):
                n = it if d == 0 else count - 1 - it
                rows = pl.ds(base + n, batch, stride=stride)
                for g in range(gpb):
                    h1, h2 = carry[2 * (d * gpb + g)], carry[2 * (d * gpb + g) + 1]
                    hp_ref[d, g, rows, :] = h1
                    a1 = dec_ref[d, g, 0:1, :]
                    a2 = dec_ref[d, g, 1:2, :]
                    out.append(a1 * h1 + a2 * h2 + s1_ref[d, g, rows, :])
                    out.append(a1 * h2 - a2 * h1 + s2_ref[d, g, rows, :])
            return tuple(out)
        return lax.fori_loop(0, count, body, carry)

    for d in range(2):
        for g in range(gpb):
            for b in range(batch):
                if pc_ > ncc:
                    hp_ref[d, g, b * pc_ + ncc:(b + 1) * pc_, :] = jnp.zeros((pc_ - ncc, pw), F32)
                if pl_ > ncl:
                    hp_ref[d, g, lat0 + b * pl_ + ncl:lat0 + (b + 1) * pl_, :] = jnp.zeros((pl_ - ncl, pw), F32)
            if rp > r:
                hp_ref[d, g, r:rp, :] = jnp.zeros((rp - r, pw), F32)
    zero = jnp.zeros((batch, pw), F32)
    carry = sweep(0, ncc, pc_, tuple([zero] * (4 * gpb)))
    sweep(lat0, ncl, pl_, carry)
    nt = (((1,), (1,)), ((), ()))
    te = T * e
    yt = []
    for g in range(gpb):
        rf = _readout_table(q1_ref, q2_ref, cc1_ref, cc2_ref, 0, g)
        rb = _readout_table(q1_ref, q2_ref, cc1_ref, cc2_ref, 1, g)
        wf = lax.dot_general(bb1_ref[0, g][:, :pw], rf[:te], nt, precision=lax.Precision.HIGHEST,
                             preferred_element_type=F32)
        wb = lax.dot_general(bb1_ref[1, g][:, :pw], rb[e:], nt, precision=lax.Precision.HIGHEST,
                             preferred_element_type=F32)
        y = jnp.dot(z_ref[g], _intra_operator(wf, wb, e), preferred_element_type=F32)
        y += lax.dot_general(hp_ref[0, g].astype(BF16), rf[e:].astype(BF16), nt,
                             preferred_element_type=F32)
        y += lax.dot_general(hp_ref[1, g].astype(BF16), rb[:te].astype(BF16), nt,
                             preferred_element_type=F32)
        yt.append(y.T)
    dvec = d_ref[...]
    for t in range(T):
        blk = jnp.concatenate([yt[g][t * e:(t + 1) * e, :] for g in range(gpb)], axis=0).T
        for b in range(batch):
            rows = pl.ds(b * ncl * T + t, ncl, stride=T)
            y_ref[rows, :] = (blk[lat0 + b * pl_:lat0 + b * pl_ + ncl, :] + dvec * u_ref[rows, :])


def _s5(u, uc, d, ops, batch):
    dec = ops[-1]
    N, W = u.shape
    Nc = uc.shape[0]
    G = dec.shape[1]
    E = W // G
    gpb = LANES // E
    TE = SSM_CHUNK * E
    P2 = dec.shape[-1]
    assert P2 == LANES and W % LANES == 0

    def table(a):
        return pl.BlockSpec((2, gpb) + a.shape[2:], lambda j: (0, j, 0, 0))
    r = batch * (_row_pitch(N // SSM_CHUNK // batch) + _row_pitch(Nc // SSM_CHUNK // batch))
    rp = -(-r // LANES) * LANES
    return pl.pallas_call(
        functools.partial(_ssm_kernel, batch=batch, e=E),
        grid=(W // LANES,),
        in_specs=[pl.BlockSpec((N, LANES), lambda j: (0, j)),
                  pl.BlockSpec((Nc, LANES), lambda j: (0, j)),
                  pl.BlockSpec((1, LANES), lambda j: (0, j))] + [table(a) for a in ops],
        out_specs=pl.BlockSpec((N, LANES), lambda j: (0, j)),
        out_shape=jax.ShapeDtypeStruct((N, W), F32),
        scratch_shapes=[pltpu.VMEM((gpb, rp, TE), BF16),
                        pltpu.VMEM((2, gpb, rp, P2), F32),
                        pltpu.VMEM((2, gpb, rp, P2), F32),
                        pltpu.VMEM((2, gpb, rp, P2), F32)],
        compiler_params=_params("arbitrary"),
        name="s5",
    )(u, uc, d, *ops)


def _mixer_out_kernel(attn_ref, y_ref, ga_ref, gs_ref, x_ref, g2_ref, wglu_ref, bglu_ref,
                      wa_ref, ws_ref, wo_ref, o_ref, y2_ref, m_ref, *, sub, tpb):
    j = pl.program_id(1)
    tm, tn = o_ref.shape
    nd = m_ref.shape[1] // tn
    half = tn // 2

    def merge(rows, attn, y2):
        for c in (0, half):
            pa = jnp.dot(attn, wa_ref[:, c:c + half], preferred_element_type=F32)
            ps = jnp.dot(y2, ws_ref[:, c:c + half], preferred_element_type=F32)
            merged = (jax.nn.sigmoid(ga_ref[rows, c:c + half].astype(F32)) * pa
                      + jax.nn.sigmoid(gs_ref[rows, c:c + half].astype(F32)) * ps)
            m_ref[rows, pl.ds(pl.multiple_of(j * tn + c, half), half)] = merged.astype(BF16)

    @pl.when(j == 0)
    def _():
        for r in range(0, tm, sub):
            rows = slice(r, r + sub)
            y = jax.nn.gelu(y_ref[rows, :])
            z = jnp.dot(y.astype(BF16), wglu_ref[...], preferred_element_type=F32) + bglu_ref[...]
            y2 = (y * jax.nn.sigmoid(z)).astype(BF16)
            y2_ref[rows, :] = y2
            merge(rows, attn_ref[rows, :], y2)

    @pl.when((j > 0) & (j < nd))
    def _():
        merge(slice(None), attn_ref[...], y2_ref[...])

    @pl.when(j >= nd)
    def _():
        for c in (0, half):
            col = pl.ds(pl.multiple_of((j - nd) * tn + c, half), half)
            acc = jnp.dot(m_ref[...], wo_ref[:, col], preferred_element_type=F32)
            g2 = _mod_row(g2_ref, 0, tpb)
            o_ref[:, c:c + half] = x_ref[:, c:c + half] + g2[:, c:c + half] * acc


def _mixer_out(attn, y_ssm, gate, x, mod, k_gate, w_glu, b_glu, wa, ws, wo, rows_per_mod):
    N, D = x.shape
    WA, WS = attn.shape[1], y_ssm.shape[1]
    tm = _tile(rows_per_mod, TOKEN_TILE)
    tn = _tile(D, COL_TILE, MXU_WIDTH)
    nd = D // tn
    tpb = rows_per_mod // tm
    lo = lambda j: jnp.minimum(j, nd - 1)
    hi = lambda j: jnp.maximum(j - nd, 0)
    return pl.pallas_call(
        functools.partial(_mixer_out_kernel, sub=_tile(tm, ROW_BLOCK), tpb=tpb),
        grid=(N // tm, 2 * nd),
        in_specs=[pl.BlockSpec((tm, WA), lambda i, j: (i, 0)),
                  pl.BlockSpec((tm, WS), lambda i, j: (i, 0)),
                  pl.BlockSpec((tm, tn), lambda i, j: (i, lo(j))),
                  pl.BlockSpec((tm, tn), lambda i, j: (i, nd + lo(j))),
                  pl.BlockSpec((tm, tn), lambda i, j: (i, hi(j))),
                  pl.BlockSpec((mod.shape[0], tn), lambda i, j: (0, k_gate * nd + hi(j))),
                  pl.BlockSpec((WS, WS), lambda i, j: (0, 0), pipeline_mode=pl.Buffered(1)),
                  pl.BlockSpec((1, WS), lambda i, j: (0, 0)),
                  pl.BlockSpec((WA, tn), lambda i, j: (0, lo(j))),
                  pl.BlockSpec((WS, tn), lambda i, j: (0, lo(j))),
                  pl.BlockSpec((D, D), lambda i, j: (0, 0), pipeline_mode=pl.Buffered(1))],
        out_specs=pl.BlockSpec((tm, tn), lambda i, j: (i, hi(j))),
        out_shape=jax.ShapeDtypeStruct((N, D), F32),
        scratch_shapes=[pltpu.VMEM((tm, WS), BF16), pltpu.VMEM((tm, D), BF16)],
        compiler_params=_params("arbitrary", "arbitrary"),
        name="mixer_out",
    )(attn, y_ssm, gate, gate, x, mod, w_glu, b_glu, wa, ws, wo)


def kernel(x, c, ctx, c_ctx, w_mod, b_mod, norm_g, w_ffn1_gate, w_ffn1_up, w_ffn1_down, w_in, q_norm_g, k_norm_g, ssm_a_re, ssm_a_im, ssm_log_dt, ssm_b_re, ssm_b_im, ssm_c_re, ssm_c_im, ssm_d, w_glu, b_glu, w_br_attn, w_br_ssm, w_out, w_ffn2_gate, w_ffn2_up, w_ffn2_down):
    B, L, D = x.shape
    Lc = ctx.shape[1]
    assert w_mod.shape[0] == 1, "only the single (last) layer configuration is implemented"
    assert w_mod.shape[2] == N_MOD * D
    hd = q_norm_g.shape[1]
    ssm_w = w_glu.shape[1]
    attn_w = w_br_attn.shape[1]
    kv_w = attn_w // Q_PER_KV
    widths = (kv_w, ssm_w, attn_w, 2 * D)
    assert L % GRID_W == 0 and L % SSM_CHUNK == 0 and Lc % SSM_CHUNK == 0 and B < SUBLANES
    N, Nc = B * L, B * Lc
    l = 0

    cc = jnp.zeros((SUBLANES, D), F32).at[:B].set(c).at[B].set(c_ctx)
    mod = _modulation(cc, w_mod[l], b_mod[l][None, :])
    ng = norm_g[l][:, None, :]

    x2 = x.reshape(N, D)
    c2 = ctx.reshape(Nc, D)

    x2, wd1, (w_in_b,) = _ffn(x2, ng[0], mod, 0, 0, w_ffn1_gate[l], w_ffn1_up[l],
                              w_ffn1_down[l], L, side_cast=(w_in[l],))
    c2, _, _ = _ffn(c2, ng[0], mod, 0, B, w_ffn1_gate[l], w_ffn1_up[l], wd1, Nc)

    cos, sin = _rope_tables(L, hd)
    kg = k_norm_g[l][None, :]
    qg = q_norm_g[l][None, :] * (hd ** -0.5 * math.log2(math.e))
    k, v, u, q, gate = _proj(x2, ng[1], mod, 3, 0, w_in_b, cos, sin, kg, L, widths, True)
    kc, vc, uc = _proj(c2, ng[1], mod, 3, B, w_in_b, cos, sin, kg, Nc, widths, False)

    attn = _attention(q, cos, sin, qg, kc, k, vc, v, B, hd)
    ops = _ssm_operators(ssm_a_re[l], ssm_a_im[l], ssm_log_dt[l], ssm_b_re[l], ssm_b_im[l],
                         ssm_c_re[l], ssm_c_im[l])
    y_ssm = _s5(u, uc, ssm_d[l][None, :], ops, B)

    x2 = _mixer_out(attn, y_ssm, gate, x2, mod, 5, w_glu[l].astype(BF16), b_glu[l][None, :],
                    w_br_attn[l].astype(BF16), w_br_ssm[l].astype(BF16), w_out[l].astype(BF16), L)

    x2, _, _ = _ffn(x2, ng[2], mod, 6, 0, w_ffn2_gate[l], w_ffn2_up[l], w_ffn2_down[l], L)
    return x2.reshape(B, L, D)
```

```python
import functools
import math

import jax
import jax.numpy as jnp
import numpy as np
from jax import lax
from jax.experimental import pallas as pl
from jax.experimental.pallas import tpu as pltpu

F32 = jnp.float32
BF16 = jnp.bfloat16

GRID_W = 64
ROPE_THETA = 10000.0
NORM_EPS = 1e-6
N_MOD = 9
Q_PER_KV = 4
SSM_CHUNK = 16
LANES = 128
SUBLANES = 8
BF16_ROWS = 16
MXU_WIDTH = 256
VMEM_LIMIT_BYTES = 58 * 1024 * 1024
TOKEN_TILE = 1024
COL_TILE = 512
ROW_BLOCK = 512
Q_TILE = 512
MOD_COL_TILE = 1024


def _tile(n, pref, mult=SUBLANES):
    if n <= pref:
        return n
    for t in range(pref, 0, -1):
        if n % t == 0 and t % mult == 0:
            return t
    return n


def _params(*sem):
    return pltpu.CompilerParams(dimension_semantics=sem, vmem_limit_bytes=VMEM_LIMIT_BYTES)


def _as_bf16(w):
    return w if w.dtype == BF16 else w.astype(BF16)


def _norm_mod(x, g, sh, sc):
    ms = jnp.mean(x * x, axis=-1, keepdims=True)
    h = (x * lax.rsqrt(ms + NORM_EPS)) * g
    return (h * (1.0 + sc) + sh).astype(BF16)


def _mod_kernel(c_ref, w_ref, b_ref, o_ref):
    c = c_ref[...]
    s = c * jax.nn.sigmoid(c)
    o_ref[...] = jnp.dot(s.astype(BF16), w_ref[...].astype(BF16),
                         preferred_element_type=F32) + b_ref[...]


def _modulation(cc, w_mod, b_mod):
    D, ncols = w_mod.shape
    tn = _tile(ncols, MOD_COL_TILE, LANES)
    return pl.pallas_call(
        _mod_kernel,
        grid=(ncols // tn,),
        in_specs=[pl.BlockSpec((cc.shape[0], D), lambda j: (0, 0)),
                  pl.BlockSpec((D, tn), lambda j: (0, j)),
                  pl.BlockSpec((1, tn), lambda j: (0, j))],
        out_specs=pl.BlockSpec((cc.shape[0], tn), lambda j: (0, j)),
        out_shape=jax.ShapeDtypeStruct((cc.shape[0], ncols), F32),
        compiler_params=_params("arbitrary"),
        name="modulation",
    )(cc, w_mod, b_mod)


def _mod_row(ref, row0, tiles_per_row):
    return ref[pl.ds(row0 + pl.program_id(0) // tiles_per_row, 1), :]


def _ffn_act_kernel(x_ref, ng_ref, sh_ref, sc_ref, wg_ref, wu_ref, *rest, sub, row0, tpb):
    if len(rest) == 4:
        wd_ref, a_ref, wdb_ref, h_ref = rest

        @pl.when(pl.program_id(0) == 0)
        def _():
            wdb_ref[...] = wd_ref[...].astype(BF16)
    else:
        a_ref, h_ref = rest

    half = a_ref.shape[1] // 2
    j = pl.program_id(1)

    def emit(row_blocks):
        halves = [(c, _as_bf16(wg_ref[:, c:c + half]), _as_bf16(wu_ref[:, c:c + half]))
                  for c in (0, half)]
        for rows, make_h in row_blocks:
            h = make_h()
            for c, wg, wu in halves:
                g = jnp.dot(h, wg, preferred_element_type=F32)
                u = jnp.dot(h, wu, preferred_element_type=F32)
                a_ref[rows, c:c + half] = ((g * jax.nn.sigmoid(g)) * u).astype(a_ref.dtype)

    def normed(rows):
        def make_h():
            h = _norm_mod(x_ref[rows, :], ng_ref[...], _mod_row(sh_ref, row0, tpb),
                          _mod_row(sc_ref, row0, tpb))
            h_ref[rows, :] = h
            return h
        return make_h

    @pl.when(j == 0)
    def _():
        emit([(slice(r, r + sub), normed(slice(r, r + sub))) for r in range(0, x_ref.shape[0], sub)])

    @pl.when(j > 0)
    def _():
        emit([(slice(None), lambda: h_ref[...])])


def _side_cast_specs(arrays, nsteps, step_of):
    specs, shapes = [], []
    for a in arrays:
        rows = a.shape[0] // nsteps
        assert a.shape[0] % nsteps == 0 and rows % BF16_ROWS == 0, (a.shape, nsteps)
        specs.append(pl.BlockSpec((rows, a.shape[1]), lambda *ids: (step_of(*ids), 0)))
        shapes.append(jax.ShapeDtypeStruct(a.shape, BF16))
    return specs, shapes


def _side_cast(src_refs, dst_refs):
    for src, dst in zip(src_refs, dst_refs):
        dst[...] = src[...].astype(BF16)


def _ffn_down_kernel(a_ref, x_ref, gate_ref, wd_ref, *rest, row0, tpb):
    n_cast = len(rest) // 2
    o_ref = rest[n_cast]
    _side_cast(rest[:n_cast], rest[n_cast + 1:])
    half = o_ref.shape[1] // 2
    half_gate = 0.5 * _mod_row(gate_ref, row0, tpb)
    for c in (0, half):
        acc = jnp.dot(a_ref[...], wd_ref[:, c:c + half], preferred_element_type=F32)
        o_ref[:, c:c + half] = x_ref[:, c:c + half] + half_gate[:, c:c + half] * acc


def _ffn(x, ng, mod, k0, row0, wg, wu, wd, rows_per_mod, side_cast=()):
    N, D = x.shape
    F = wg.shape[1]
    tm = _tile(rows_per_mod, TOKEN_TILE)
    tf = _tile(F, COL_TILE, MXU_WIDTH)
    tn = _tile(D, COL_TILE, MXU_WIDTH)
    tpb = rows_per_mod // tm
    nj = F // tf
    nrow = mod.shape[0]
    in_specs = [pl.BlockSpec((tm, D), lambda i, j: (i, 0)),
                pl.BlockSpec((1, D), lambda i, j: (0, 0)),
                pl.BlockSpec((nrow, D), lambda i, j: (0, k0)),
                pl.BlockSpec((nrow, D), lambda i, j: (0, k0 + 1)),
                pl.BlockSpec((D, tf), lambda i, j: (0, j)),
                pl.BlockSpec((D, tf), lambda i, j: (0, j))]
    out_specs = [pl.BlockSpec((tm, tf), lambda i, j: (i, j))]
    out_shape = [jax.ShapeDtypeStruct((N, F), BF16)]
    operands = [x, ng, mod, mod, wg, wu]
    if wd.dtype != BF16:
        wd_rows = pl.BlockSpec((tf, D), lambda i, j: (jnp.where(i == 0, j, nj - 1), 0))
        in_specs.append(wd_rows)
        out_specs.append(wd_rows)
        out_shape.append(jax.ShapeDtypeStruct((F, D), BF16))
        operands.append(wd)
    res = pl.pallas_call(
        functools.partial(_ffn_act_kernel, sub=_tile(tm, ROW_BLOCK), row0=row0, tpb=tpb),
        grid=(N // tm, nj),
        in_specs=in_specs,
        out_specs=out_specs,
        out_shape=out_shape,
        scratch_shapes=[pltpu.VMEM((tm, D), BF16)],
        compiler_params=_params("arbitrary", "arbitrary"),
        name="ffn_act",
    )(*operands)
    a = res[0]
    if wd.dtype != BF16:
        wd = res[1]
    nd = D // tn
    cast_specs, cast_shapes = _side_cast_specs(side_cast, (N // tm) * nd, lambda i, j: i * nd + j)
    res = pl.pallas_call(
        functools.partial(_ffn_down_kernel, row0=row0, tpb=tpb),
        grid=(N // tm, nd),
        in_specs=[pl.BlockSpec((tm, F), lambda i, j: (i, 0)),
                  pl.BlockSpec((tm, tn), lambda i, j: (i, j)),
                  pl.BlockSpec((nrow, tn), lambda i, j: (0, (k0 + 2) * nd + j)),
                  pl.BlockSpec((F, tn), lambda i, j: (0, j))] + cast_specs,
        out_specs=[pl.BlockSpec((tm, tn), lambda i, j: (i, j))] + cast_specs,
        out_shape=[jax.ShapeDtypeStruct((N, D), F32)] + cast_shapes,
        compiler_params=_params("arbitrary", "arbitrary"),
        name="ffn_down",
    )(a, x, mod, wd, *side_cast)
    return res[0], wd, res[1:]


def _head_norm(acc, g, cos, sin, head_dim):
    quarter = head_dim // 4
    out = []
    for hh in range(acc.shape[1] // head_dim):
        xh = acc[:, hh * head_dim:(hh + 1) * head_dim]
        ms = jnp.mean(xh * xh, axis=-1, keepdims=True)
        xh = (xh * lax.rsqrt(ms + NORM_EPS)) * g
        if cos is not None:
            lane = lax.broadcasted_iota(jnp.int32, xh.shape, 1)
            partner = jnp.where((lane % (2 * quarter)) < quarter,
                                pltpu.roll(xh, head_dim - quarter, 1),
                                pltpu.roll(xh, quarter, 1))
            xh = xh * cos + partner * sin
        out.append(xh.astype(BF16))
    return out[0] if len(out) == 1 else jnp.concatenate(out, axis=1)


def _proj_kernel(x_ref, ng_ref, sh_ref, sc_ref, w_ref, cos_ref, sin_ref, kg_ref,
                 k_ref, v_ref, u_ref, *rest, bounds, rope, head_dim, sub, row0, tpb):
    h_ref = rest[-1]
    j = pl.program_id(1)
    tm = x_ref.shape[0]
    jv, ju, jq, jg = bounds
    assert jv == 1

    @pl.when(j == 0)
    def _():
        for r in range(0, tm, sub):
            rows = slice(r, r + sub)
            h = _norm_mod(x_ref[rows, :], ng_ref[...], _mod_row(sh_ref, row0, tpb),
                          _mod_row(sc_ref, row0, tpb))
            h_ref[rows, :] = h
            acc = jnp.dot(h, w_ref[...], preferred_element_type=F32)
            cos = cos_ref[rows, :] if rope else None
            sin = sin_ref[rows, :] if rope else None
            k_ref[rows, :] = _head_norm(acc, kg_ref[...], cos, sin, head_dim)

    @pl.when(j > 0)
    def _():
        acc = jnp.dot(h_ref[...], w_ref[...], preferred_element_type=F32)

        @pl.when(j < ju)
        def _():
            v_ref[...] = acc.astype(BF16)

        @pl.when((j >= ju) & (j < jq))
        def _():
            u_ref[...] = acc

        if len(rest) == 3:
            q_ref, gate_ref = rest[0], rest[1]

            @pl.when((j >= jq) & (j < jg))
            def _():
                q_ref[...] = acc

            @pl.when(j >= jg)
            def _():
                gate_ref[...] = acc.astype(BF16)


def _proj(x, ng, mod, k0, row0, w, cos, sin, kg, rows_per_seq, widths, latent):
    N, D = x.shape
    kv_w, ssm_w, attn_w, gate_w = widths
    head_dim = kg.shape[1]
    tn = kv_w
    assert tn % LANES == 0 and ssm_w % tn == 0 and attn_w % tn == 0 and gate_w % tn == 0
    jv, ju = 1, 2
    jq = ju + ssm_w // tn
    jg = jq + attn_w // tn
    nj = jg + gate_w // tn if latent else jq
    tm = _tile(rows_per_seq, TOKEN_TILE)
    tpb = rows_per_seq // tm
    nrow = mod.shape[0]
    tab = pl.BlockSpec((tm, head_dim), lambda i, j: (i % tpb, 0))
    hvec = pl.BlockSpec((1, head_dim), lambda i, j: (0, 0))

    def out_spec(j0, nblk):
        return pl.BlockSpec((tm, tn), lambda i, j: (i, jnp.clip(j - j0, 0, nblk - 1)))

    out_specs = [out_spec(0, 1), out_spec(jv, 1), out_spec(ju, jq - ju)]
    out_shape = [jax.ShapeDtypeStruct((N, kv_w), BF16), jax.ShapeDtypeStruct((N, kv_w), BF16),
                 jax.ShapeDtypeStruct((N, ssm_w), F32)]
    if latent:
        out_specs += [out_spec(jq, jg - jq), out_spec(jg, nj - jg)]
        out_shape += [jax.ShapeDtypeStruct((N, attn_w), F32), jax.ShapeDtypeStruct((N, gate_w), BF16)]
    return pl.pallas_call(
        functools.partial(_proj_kernel, bounds=(jv, ju, jq, jg), rope=latent, head_dim=head_dim,
                          sub=_tile(tm, ROW_BLOCK), row0=row0, tpb=tpb),
        grid=(N // tm, nj),
        in_specs=[pl.BlockSpec((tm, D), lambda i, j: (i, 0)),
                  pl.BlockSpec((1, D), lambda i, j: (0, 0)),
                  pl.BlockSpec((nrow, D), lambda i, j: (0, k0)),
                  pl.BlockSpec((nrow, D), lambda i, j: (0, k0 + 1)),
                  pl.BlockSpec((D, tn), lambda i, j: (0, j)),
                  tab, tab, hvec],
        out_specs=out_specs,
        out_shape=out_shape,
        scratch_shapes=[pltpu.VMEM((tm, D), BF16)],
        compiler_params=_params("arbitrary", "arbitrary"),
        name="proj_latent" if latent else "proj_context",
    )(x, ng, mod, mod, w, cos, sin, kg)


def _rope_tables(L, head_dim):
    t = np.arange(L)
    row = (t // GRID_W).astype(np.float64)
    col = (t % GRID_W).astype(np.float64)
    half = head_dim // 4
    inv_freq = ROPE_THETA ** (-np.arange(half, dtype=np.float64) / half)
    ar = row[:, None] * inv_freq
    ac = col[:, None] * inv_freq
    cos = np.concatenate([np.cos(ar), np.cos(ar), np.cos(ac), np.cos(ac)], axis=-1)
    sin = np.concatenate([-np.sin(ar), np.sin(ar), -np.sin(ac), np.sin(ac)], axis=-1)
    return jnp.asarray(cos, F32), jnp.asarray(sin, F32)


def _attn_kernel(q_ref, cos_ref, sin_ref, qn_ref, cosn_ref, sinn_ref, qg_ref,
                 kc_ref, k_ref, vc_ref, v_ref, o_ref, kall_ref, vext_ref, qs_ref, *, head_dim):
    lc = kc_ref.shape[0]
    i = pl.program_id(2)
    step = (pl.program_id(0) * pl.num_programs(1) + pl.program_id(1)) * pl.num_programs(2) + i

    @pl.when(step == 0)
    def _():
        qs_ref[0] = _head_norm(q_ref[...], qg_ref[...], cos_ref[...], sin_ref[...], head_dim)

    @pl.when(i == 0)
    def _():
        kall_ref[0:lc, :] = kc_ref[...]
        kall_ref[lc:, :] = k_ref[...]
        vext_ref[0:lc, 0:head_dim] = vc_ref[...]
        vext_ref[lc:, 0:head_dim] = v_ref[...]
        vext_ref[:, head_dim:] = jnp.ones((vext_ref.shape[0], head_dim), BF16)

    k = kall_ref[...]
    vext = vext_ref[...]
    n_rep = q_ref.shape[1] // head_dim
    heads = [slice(r * head_dim, (r + 1) * head_dim) for r in range(n_rep)]

    def scores(sl):
        return lax.dot_general(qs_ref[step % 2, :, sl], k, (((1,), (1,)), ((), ())),
                               preferred_element_type=F32)

    s_next = scores(heads[0])
    for r, sl in enumerate(heads):
        s = s_next
        if r + 1 < n_rep:
            s_next = scores(heads[r + 1])
        qs_ref[(step + 1) % 2, :, sl] = _head_norm(qn_ref[:, sl], qg_ref[...], cosn_ref[...],
                                                   sinn_ref[...], head_dim)
        m = jnp.max(s, axis=-1, keepdims=True)
        p = jnp.exp2(s - m).astype(BF16)
        oe = jnp.dot(p, vext, preferred_element_type=F32)
        o_ref[:, sl] = (oe[:, :head_dim] / oe[:, head_dim:]).astype(o_ref.dtype)


def _attention(q, cos, sin, qg, kc, k, vc, v, B, head_dim):
    N, W = q.shape
    L, Lc = N // B, kc.shape[0] // B
    kvh = k.shape[1] // head_dim
    gw = W // kvh
    tq = _tile(L, Q_TILE)
    tpb = L // tq
    kv_lat = pl.BlockSpec((L, head_dim), lambda b, h, i: (b, h))
    kv_ctx = pl.BlockSpec((Lc, head_dim), lambda b, h, i: (b, h))

    def nxt(b, h, i):
        wrap = i + 1 >= tpb
        g = jnp.minimum(b * kvh + h + 1, B * kvh - 1)
        return (jnp.where(wrap, g // kvh, b), jnp.where(wrap, g % kvh, h),
                jnp.where(wrap, 0, i + 1))

    def q_next(b, h, i):
        nb, nh, ni = nxt(b, h, i)
        return nb * tpb + ni, nh

    tab = pl.BlockSpec((tq, head_dim), lambda b, h, i: (i, 0))
    tab_next = pl.BlockSpec((tq, head_dim), lambda b, h, i: (nxt(b, h, i)[2], 0))
    return pl.pallas_call(
        functools.partial(_attn_kernel, head_dim=head_dim),
        grid=(B, kvh, tpb),
        in_specs=[pl.BlockSpec((tq, gw), lambda b, h, i: (b * tpb + i, h)), tab, tab,
                  pl.BlockSpec((tq, gw), q_next), tab_next, tab_next,
                  pl.BlockSpec((1, head_dim), lambda b, h, i: (0, 0)),
                  kv_ctx, kv_lat, kv_ctx, kv_lat],
        out_specs=pl.BlockSpec((tq, gw), lambda b, h, i: (b * tpb + i, h)),
        out_shape=jax.ShapeDtypeStruct((N, W), BF16),
        scratch_shapes=[pltpu.VMEM((L + Lc, head_dim), BF16),
                        pltpu.VMEM((L + Lc, 2 * head_dim), BF16),
                        pltpu.VMEM((2, tq, gw), BF16)],
        compiler_params=_params("arbitrary", "arbitrary", "arbitrary"),
        name="attention",
    )(q, cos, sin, q, cos, sin, qg, kc, k, vc, v)


def _ssm_operators(a_re, a_im, log_dt, b_re, b_im, c_re, c_im):
    T = SSM_CHUNK
    a_re, a_im = a_re.astype(F32), a_im.astype(F32)
    dt = jnp.exp(log_dt.astype(F32))[..., None]
    mag = jnp.exp(a_re * dt)
    lr = mag * jnp.cos(a_im * dt)
    li = mag * jnp.sin(a_im * dt)
    den = a_re * a_re + a_im * a_im
    cr = ((lr - 1.0) * a_re + li * a_im) / den
    ci = (li * a_re - (lr - 1.0) * a_im) / den
    pr, pi = [jnp.ones_like(lr)], [jnp.zeros_like(lr)]
    for _ in range(T):
        pr.append(pr[-1] * lr - pi[-1] * li)
        pi.append(pr[-2] * li + pi[-1] * lr)
    pw_r, pw_i = jnp.stack(pr), jnp.stack(pi)
    _, G, P, E = b_re.shape
    bt_r = jnp.swapaxes(b_re.astype(F32), 2, 3)
    bt_i = jnp.swapaxes(b_im.astype(F32), 2, 3)
    bb_r = cr[:, :, None, :] * bt_r - ci[:, :, None, :] * bt_i
    bb_i = cr[:, :, None, :] * bt_i + ci[:, :, None, :] * bt_r
    c_r, c_i = c_re.astype(F32), c_im.astype(F32)

    def lag_table(pw, fwd_lags, bwd_lags, reps):
        tab = jnp.stack([jnp.moveaxis(pw[fwd_lags, 0], 0, 1), jnp.moveaxis(pw[bwd_lags, 1], 0, 1)])
        return jnp.concatenate([tab] * reps, axis=-1)

    steps = np.arange(T)
    lags = np.arange(T + 1)
    p1 = lag_table(pw_r, T - 1 - steps, steps, 4)
    p2 = lag_table(pw_i, T - 1 - steps, steps, 4)
    q1 = lag_table(pw_r, lags, T - lags, 2)
    q2 = lag_table(pw_i, lags, T - lags, 2)
    bb1 = jnp.concatenate([bb_r, bb_i, bb_i, bb_r], axis=-1)
    bb2 = jnp.concatenate([-bb_i, bb_r, bb_r, -bb_i], axis=-1)
    cc1 = jnp.concatenate([c_r, -c_i], axis=-1)
    cc2 = jnp.concatenate([-c_i, -c_r], axis=-1)
    ar, ai = pw_r[T], pw_i[T]
    dec = jnp.stack([jnp.concatenate([ar, ar], -1), jnp.concatenate([-ai, ai], -1)], axis=2)
    return p1, p2, bb1, bb2, q1, q2, cc1, cc2, dec


def _state_in_operator(p1_ref, p2_ref, bb1_ref, bb2_ref, d, g):
    bb1, bb2 = bb1_ref[d, g], bb2_ref[d, g]
    rows = [p1_ref[d, g, s:s + 1, :] * bb1 + p2_ref[d, g, s:s + 1, :] * bb2
            for s in range(SSM_CHUNK)]
    return jnp.concatenate(rows, axis=0).astype(BF16)


def _readout_table(q1_ref, q2_ref, cc1_ref, cc2_ref, d, g):
    cc1, cc2 = cc1_ref[d, g], cc2_ref[d, g]
    rows = [q1_ref[d, g, j:j + 1, :] * cc1 + q2_ref[d, g, j:j + 1, :] * cc2
            for j in range(SSM_CHUNK + 1)]
    return jnp.concatenate(rows, axis=0)


def _intra_operator(wf, wb, e):
    T = SSM_CHUNK
    t_of_lane = lax.broadcasted_iota(jnp.int32, wf.shape, 1) // e
    rows = []
    for s in range(T):
        f = wf if s == 0 else jnp.where(t_of_lane >= s, pltpu.roll(wf, s * e, 1), 0.0)
        b = wb if s == T - 1 else jnp.where(t_of_lane <= s, pltpu.roll(wb, (s + 1) * e, 1), 0.0)
        rows.append(f + b)
    return jnp.concatenate(rows, axis=0).astype(BF16)


def _row_pitch(n):
    p = -(-n // SUBLANES)
    return SUBLANES * (p if p % 2 else p + 1)


def _ssm_kernel(u_ref, uc_ref, d_ref, p1_ref, p2_ref, bb1_ref, bb2_ref, q1_ref, q2_ref,
                cc1_ref, cc2_ref, dec_ref, y_ref, z_ref, s1_ref, s2_ref, hp_ref, *, batch, e):
    T = SSM_CHUNK
    gpb = dec_ref.shape[1]
    rp = z_ref.shape[1]
    n_lat, n_ctx = u_ref.shape[0] // T, uc_ref.shape[0] // T
    ncl, ncc = n_lat // batch, n_ctx // batch
    pl_, pc_ = _row_pitch(ncl), _row_pitch(ncc)
    lat0 = batch * pc_
    r = lat0 + batch * pl_
    lanes = u_ref.shape[1]
    pw = hp_ref.shape[3]

    def padded(rows, n, pitch):
        if pitch == n:
            return [rows]
        out = []
        for b in range(batch):
            out += [rows[b * n:(b + 1) * n, :], jnp.zeros((pitch - n, lanes), F32)]
        return out

    xt = []
    for s in range(T):
        parts = (padded(uc_ref[pl.ds(s, n_ctx, stride=T), :], ncc, pc_)
                 + padded(u_ref[pl.ds(s, n_lat, stride=T), :], ncl, pl_))
        if rp > r:
            parts.append(jnp.zeros((rp - r, lanes), F32))
        xt.append(jnp.concatenate(parts, axis=0).astype(BF16).T)
    for g in range(gpb):
        zt = jnp.concatenate([xt[s][g * e:(g + 1) * e, :] for s in range(T)], axis=0)
        z_ref[g] = zt.T
    for d in range(2):
        for g in range(gpb):
            st = jnp.dot(z_ref[g], _state_in_operator(p1_ref, p2_ref, bb1_ref, bb2_ref, d, g),
                         preferred_element_type=F32)
            s1_ref[d, g] = st[:, :pw]
            s2_ref[d, g] = st[:, pw:]

    def sweep(base, count, stride, carry):
        def body(it, carry):
            out = []
            for d in range(2):
                n = it if d == 0 else count - 1 - it
                rows = pl.ds(base + n, batch, stride=stride)
                for g in range(gpb):
                    h1, h2 = carry[2 * (d * gpb + g)], carry[2 * (d * gpb + g) + 1]
                    hp_ref[d, g, rows, :] = h1
                    a1 = dec_ref[d, g, 0:1, :]
                    a2 = dec_ref[d, g, 1:2, :]
                    out.append(a1 * h1 + a2 * h2 + s1_ref[d, g, rows, :])
                    out.append(a1 * h2 - a2 * h1 + s2_ref[d, g, rows, :])
            return tuple(out)
        return lax.fori_loop(0, count, body, carry)

    for d in range(2):
        for g in range(gpb):
            for b in range(batch):
                if pc_ > ncc:
                    hp_ref[d, g, b * pc_ + ncc:(b + 1) * pc_, :] = jnp.zeros((pc_ - ncc, pw), F32)
                if pl_ > ncl:
                    hp_ref[d, g, lat0 + b * pl_ + ncl:lat0 + (b + 1) * pl_, :] = jnp.zeros((pl_ - ncl, pw), F32)
            if rp > r:
                hp_ref[d, g, r:rp, :] = jnp.zeros((rp - r, pw), F32)
    zero = jnp.zeros((batch, pw), F32)
    carry = sweep(0, ncc, pc_, tuple([zero] * (4 * gpb)))
    sweep(lat0, ncl, pl_, carry)
    nt = (((1,), (1,)), ((), ()))
    te = T * e
    yt = []
    for g in range(gpb):
        rf = _readout_table(q1_ref, q2_ref, cc1_ref, cc2_ref, 0, g)
        rb = _readout_table(q1_ref, q2_ref, cc1_ref, cc2_ref, 1, g)
        wf = lax.dot_general(bb1_ref[0, g][:, :pw], rf[:te], nt, precision=lax.Precision.HIGHEST,
                             preferred_element_type=F32)
        wb = lax.dot_general(bb1_ref[1, g][:, :pw], rb[e:], nt, precision=lax.Precision.HIGHEST,
                             preferred_element_type=F32)
        y = jnp.dot(z_ref[g], _intra_operator(wf, wb, e), preferred_element_type=F32)
        y += lax.dot_general(hp_ref[0, g].astype(BF16), rf[e:].astype(BF16), nt,
                             preferred_element_type=F32)
        y += lax.dot_general(hp_ref[1, g].astype(BF16), rb[:te].astype(BF16), nt,
                             preferred_element_type=F32)
        yt.append(y.T)
    dvec = d_ref[...]
    for t in range(T):
        blk = jnp.concatenate([yt[g][t * e:(t + 1) * e, :] for g in range(gpb)], axis=0).T
        for b in range(batch):
            rows = pl.ds(b * ncl * T + t, ncl, stride=T)
            y_ref[rows, :] = (blk[lat0 + b * pl_:lat0 + b * pl_ + ncl, :] + dvec * u_ref[rows, :])


def _s5(u, uc, d, ops, batch):
    dec = ops[-1]
    N, W = u.shape
    Nc = uc.shape[0]
    G = dec.shape[1]
    E = W // G
    gpb = LANES // E
    TE = SSM_CHUNK * E
    P2 = dec.shape[-1]
    assert P2 == LANES and W % LANES == 0

    def table(a):
        return pl.BlockSpec((2, gpb) + a.shape[2:], lambda j: (0, j, 0, 0))
    r = batch * (_row_pitch(N // SSM_CHUNK // batch) + _row_pitch(Nc // SSM_CHUNK // batch))
    rp = -(-r // LANES) * LANES
    return pl.pallas_call(
        functools.partial(_ssm_kernel, batch=batch, e=E),
        grid=(W // LANES,),
        in_specs=[pl.BlockSpec((N, LANES), lambda j: (0, j)),
                  pl.BlockSpec((Nc, LANES), lambda j: (0, j)),
                  pl.BlockSpec((1, LANES), lambda j: (0, j))] + [table(a) for a in ops],
        out_specs=pl.BlockSpec((N, LANES), lambda j: (0, j)),
        out_shape=jax.ShapeDtypeStruct((N, W), F32),
        scratch_shapes=[pltpu.VMEM((gpb, rp, TE), BF16),
                        pltpu.VMEM((2, gpb, rp, P2), F32),
                        pltpu.VMEM((2, gpb, rp, P2), F32),
                        pltpu.VMEM((2, gpb, rp, P2), F32)],
        compiler_params=_params("arbitrary"),
        name="s5",
    )(u, uc, d, *ops)


def _mixer_out_kernel(attn_ref, y_ref, ga_ref, gs_ref, x_ref, g2_ref, wglu_ref, bglu_ref,
                      wa_ref, ws_ref, wo_ref, o_ref, y2_ref, m_ref, *, sub, tpb):
    j = pl.program_id(1)
    tm, tn = o_ref.shape
    nd = m_ref.shape[1] // tn
    half = tn // 2

    def merge(rows, attn, y2):
        for c in (0, half):
            pa = jnp.dot(attn, wa_ref[:, c:c + half], preferred_element_type=F32)
            ps = jnp.dot(y2, ws_ref[:, c:c + half], preferred_element_type=F32)
            merged = (jax.nn.sigmoid(ga_ref[rows, c:c + half].astype(F32)) * pa
                      + jax.nn.sigmoid(gs_ref[rows, c:c + half].astype(F32)) * ps)
            m_ref[rows, pl.ds(pl.multiple_of(j * tn + c, half), half)] = merged.astype(BF16)

    @pl.when(j == 0)
    def _():
        for r in range(0, tm, sub):
            rows = slice(r, r + sub)
            y = jax.nn.gelu(y_ref[rows, :])
            z = jnp.dot(y.astype(BF16), wglu_ref[...], preferred_element_type=F32) + bglu_ref[...]
            y2 = (y * jax.nn.sigmoid(z)).astype(BF16)
            y2_ref[rows, :] = y2
            merge(rows, attn_ref[rows, :], y2)

    @pl.when((j > 0) & (j < nd))
    def _():
        merge(slice(None), attn_ref[...], y2_ref[...])

    @pl.when(j >= nd)
    def _():
        for c in (0, half):
            col = pl.ds(pl.multiple_of((j - nd) * tn + c, half), half)
            acc = jnp.dot(m_ref[...], wo_ref[:, col], preferred_element_type=F32)
            g2 = _mod_row(g2_ref, 0, tpb)
            o_ref[:, c:c + half] = x_ref[:, c:c + half] + g2[:, c:c + half] * acc


def _mixer_out(attn, y_ssm, gate, x, mod, k_gate, w_glu, b_glu, wa, ws, wo, rows_per_mod):
    N, D = x.shape
    WA, WS = attn.shape[1], y_ssm.shape[1]
    tm = _tile(rows_per_mod, TOKEN_TILE)
    tn = _tile(D, COL_TILE, MXU_WIDTH)
    nd = D // tn
    tpb = rows_per_mod // tm
    lo = lambda j: jnp.minimum(j, nd - 1)
    hi = lambda j: jnp.maximum(j - nd, 0)
    return pl.pallas_call(
        functools.partial(_mixer_out_kernel, sub=_tile(tm, ROW_BLOCK), tpb=tpb),
        grid=(N // tm, 2 * nd),
        in_specs=[pl.BlockSpec((tm, WA), lambda i, j: (i, 0)),
                  pl.BlockSpec((tm, WS), lambda i, j: (i, 0)),
                  pl.BlockSpec((tm, tn), lambda i, j: (i, lo(j))),
                  pl.BlockSpec((tm, tn), lambda i, j: (i, nd + lo(j))),
                  pl.BlockSpec((tm, tn), lambda i, j: (i, hi(j))),
                  pl.BlockSpec((mod.shape[0], tn), lambda i, j: (0, k_gate * nd + hi(j))),
                  pl.BlockSpec((WS, WS), lambda i, j: (0, 0), pipeline_mode=pl.Buffered(1)),
                  pl.BlockSpec((1, WS), lambda i, j: (0, 0)),
                  pl.BlockSpec((WA, tn), lambda i, j: (0, lo(j))),
                  pl.BlockSpec((WS, tn), lambda i, j: (0, lo(j))),
                  pl.BlockSpec((D, D), lambda i, j: (0, 0), pipeline_mode=pl.Buffered(1))],
        out_specs=pl.BlockSpec((tm, tn), lambda i, j: (i, hi(j))),
        out_shape=jax.ShapeDtypeStruct((N, D), F32),
        scratch_shapes=[pltpu.VMEM((tm, WS), BF16), pltpu.VMEM((tm, D), BF16)],
        compiler_params=_params("arbitrary", "arbitrary"),
        name="mixer_out",
    )(attn, y_ssm, gate, gate, x, mod, w_glu, b_glu, wa, ws, wo)


def kernel(x, c, ctx, c_ctx, w_mod, b_mod, norm_g, w_ffn1_gate, w_ffn1_up, w_ffn1_down, w_in, q_norm_g, k_norm_g, ssm_a_re, ssm_a_im, ssm_log_dt, ssm_b_re, ssm_b_im, ssm_c_re, ssm_c_im, ssm_d, w_glu, b_glu, w_br_attn, w_br_ssm, w_out, w_ffn2_gate, w_ffn2_up, w_ffn2_down):
    B, L, D = x.shape
    Lc = ctx.shape[1]
    assert w_mod.shape[0] == 1, "only the single (last) layer configuration is implemented"
    assert w_mod.shape[2] == N_MOD * D
    hd = q_norm_g.shape[1]
    ssm_w = w_glu.shape[1]
    attn_w = w_br_attn.shape[1]
    kv_w = attn_w // Q_PER_KV
    widths = (kv_w, ssm_w, attn_w, 2 * D)
    assert L % GRID_W == 0 and L % SSM_CHUNK == 0 and Lc % SSM_CHUNK == 0 and B < SUBLANES
    N, Nc = B * L, B * Lc
    l = 0

    cc = jnp.zeros((SUBLANES, D), F32).at[:B].set(c).at[B].set(c_ctx)
    mod = _modulation(cc, w_mod[l], b_mod[l][None, :])
    ng = norm_g[l][:, None, :]

    x2 = x.reshape(N, D)
    c2 = ctx.reshape(Nc, D)

    x2, wd1, (w_in_b,) = _ffn(x2, ng[0], mod, 0, 0, w_ffn1_gate[l], w_ffn1_up[l],
                              w_ffn1_down[l], L, side_cast=(w_in[l],))
    c2, _, _ = _ffn(c2, ng[0], mod, 0, B, w_ffn1_gate[l], w_ffn1_up[l], wd1, Nc)

    cos, sin = _rope_tables(L, hd)
    kg = k_norm_g[l][None, :]
    qg = q_norm_g[l][None, :] * (hd ** -0.5 * math.log2(math.e))
    k, v, u, q, gate = _proj(x2, ng[1], mod, 3, 0, w_in_b, cos, sin, kg, L, widths, True)
    kc, vc, uc = _proj(c2, ng[1], mod, 3, B, w_in_b, cos, sin, kg, Nc, widths, False)

    attn = _attention(q, cos, sin, qg, kc, k, vc, v, B, hd)
    ops = _ssm_operators(ssm_a_re[l], ssm_a_im[l], ssm_log_dt[l], ssm_b_re[l], ssm_b_im[l],
                         ssm_c_re[l], ssm_c_im[l])
    y_ssm = _s5(u, uc, ssm_d[l][None, :], ops, B)

    x2 = _mixer_out(attn, y_ssm, gate, x2, mod, 5, w_glu[l].astype(BF16), b_glu[l][None, :],
                    w_br_attn[l].astype(BF16), w_br_ssm[l].astype(BF16), w_out[l].astype(BF16), L)

    x2, _, _ = _ffn(x2, ng[2], mod, 6, 0, w_ffn2_gate[l], w_ffn2_up[l], w_ffn2_down[l], L)
    return x2.reshape(B, L, D)
```

```python
import functools
import math

import jax
import jax.numpy as jnp
import numpy as np
from jax import lax
from jax.experimental import pallas as pl
from jax.experimental.pallas import tpu as pltpu

F32 = jnp.float32
BF16 = jnp.bfloat16

GRID_W = 64
ROPE_THETA = 10000.0
NORM_EPS = 1e-6
N_MOD = 9
Q_PER_KV = 4
SSM_CHUNK = 16
LANES = 128
SUBLANES = 8
BF16_ROWS = 16
MXU_WIDTH = 256
VMEM_LIMIT_BYTES = 58 * 1024 * 1024
TOKEN_TILE = 1024
COL_TILE = 512
ROW_BLOCK = 256
Q_TILE = 512
MOD_COL_TILE = 1024


def _tile(n, pref, mult=SUBLANES):
    if n <= pref:
        return n
    for t in range(pref, 0, -1):
        if n % t == 0 and t % mult == 0:
            return t
    return n


def _params(*sem):
    return pltpu.CompilerParams(dimension_semantics=sem, vmem_limit_bytes=VMEM_LIMIT_BYTES)


def _as_bf16(w):
    return w if w.dtype == BF16 else w.astype(BF16)


def _norm_mod(x, g, sh, sc):
    ms = jnp.mean(x * x, axis=-1, keepdims=True)
    h = (x * lax.rsqrt(ms + NORM_EPS)) * g
    return (h * (1.0 + sc) + sh).astype(BF16)


def _mod_kernel(c_ref, w_ref, b_ref, o_ref):
    c = c_ref[...]
    s = c * jax.nn.sigmoid(c)
    o_ref[...] = jnp.dot(s.astype(BF16), w_ref[...].astype(BF16),
                         preferred_element_type=F32) + b_ref[...]


def _modulation(cc, w_mod, b_mod):
    D, ncols = w_mod.shape
    tn = _tile(ncols, MOD_COL_TILE, LANES)
    return pl.pallas_call(
        _mod_kernel,
        grid=(ncols // tn,),
        in_specs=[pl.BlockSpec((cc.shape[0], D), lambda j: (0, 0)),
                  pl.BlockSpec((D, tn), lambda j: (0, j)),
                  pl.BlockSpec((1, tn), lambda j: (0, j))],
        out_specs=pl.BlockSpec((cc.shape[0], tn), lambda j: (0, j)),
        out_shape=jax.ShapeDtypeStruct((cc.shape[0], ncols), F32),
        compiler_params=_params("arbitrary"),
        name="modulation",
    )(cc, w_mod, b_mod)


def _mod_row(ref, row0, tiles_per_row):
    return ref[pl.ds(row0 + pl.program_id(0) // tiles_per_row, 1), :]


def _ffn_act_kernel(x_ref, ng_ref, sh_ref, sc_ref, wg_ref, wu_ref, *rest, sub, row0, tpb):
    if len(rest) == 4:
        wd_ref, a_ref, wdb_ref, h_ref = rest

        @pl.when(pl.program_id(0) == 0)
        def _():
            wdb_ref[...] = wd_ref[...].astype(BF16)
    else:
        a_ref, h_ref = rest

    half = a_ref.shape[1] // 2
    j = pl.program_id(1)

    def emit(row_blocks):
        halves = [(c, _as_bf16(wg_ref[:, c:c + half]), _as_bf16(wu_ref[:, c:c + half]))
                  for c in (0, half)]
        for rows, make_h in row_blocks:
            h = make_h()
            for c, wg, wu in halves:
                g = jnp.dot(h, wg, preferred_element_type=F32)
                u = jnp.dot(h, wu, preferred_element_type=F32)
                a_ref[rows, c:c + half] = ((g * jax.nn.sigmoid(g)) * u).astype(a_ref.dtype)

    def normed(rows):
        def make_h():
            h = _norm_mod(x_ref[rows, :], ng_ref[...], _mod_row(sh_ref, row0, tpb),
                          _mod_row(sc_ref, row0, tpb))
            h_ref[rows, :] = h
            return h
        return make_h

    @pl.when(j == 0)
    def _():
        emit([(slice(r, r + sub), normed(slice(r, r + sub))) for r in range(0, x_ref.shape[0], sub)])

    @pl.when(j > 0)
    def _():
        emit([(slice(None), lambda: h_ref[...])])


def _side_cast_specs(arrays, nsteps, step_of):
    specs, shapes = [], []
    for a in arrays:
        rows = a.shape[0] // nsteps
        assert a.shape[0] % nsteps == 0 and rows % BF16_ROWS == 0, (a.shape, nsteps)
        specs.append(pl.BlockSpec((rows, a.shape[1]), lambda *ids: (step_of(*ids), 0)))
        shapes.append(jax.ShapeDtypeStruct(a.shape, BF16))
    return specs, shapes


def _side_cast(src_refs, dst_refs):
    for src, dst in zip(src_refs, dst_refs):
        dst[...] = src[...].astype(BF16)


def _ffn_down_kernel(a_ref, x_ref, gate_ref, wd_ref, *rest, row0, tpb):
    n_cast = len(rest) // 2
    o_ref = rest[n_cast]
    _side_cast(rest[:n_cast], rest[n_cast + 1:])
    half = o_ref.shape[1] // 2
    half_gate = 0.5 * _mod_row(gate_ref, row0, tpb)
    for c in (0, half):
        acc = jnp.dot(a_ref[...], wd_ref[:, c:c + half], preferred_element_type=F32)
        o_ref[:, c:c + half] = x_ref[:, c:c + half] + half_gate[:, c:c + half] * acc


def _ffn(x, ng, mod, k0, row0, wg, wu, wd, rows_per_mod, side_cast=()):
    N, D = x.shape
    F = wg.shape[1]
    tm = _tile(rows_per_mod, TOKEN_TILE)
    tf = _tile(F, COL_TILE, MXU_WIDTH)
    tn = _tile(D, COL_TILE, MXU_WIDTH)
    tpb = rows_per_mod // tm
    nj = F // tf
    nrow = mod.shape[0]
    in_specs = [pl.BlockSpec((tm, D), lambda i, j: (i, 0)),
                pl.BlockSpec((1, D), lambda i, j: (0, 0)),
                pl.BlockSpec((nrow, D), lambda i, j: (0, k0)),
                pl.BlockSpec((nrow, D), lambda i, j: (0, k0 + 1)),
                pl.BlockSpec((D, tf), lambda i, j: (0, j)),
                pl.BlockSpec((D, tf), lambda i, j: (0, j))]
    out_specs = [pl.BlockSpec((tm, tf), lambda i, j: (i, j))]
    out_shape = [jax.ShapeDtypeStruct((N, F), BF16)]
    operands = [x, ng, mod, mod, wg, wu]
    if wd.dtype != BF16:
        wd_rows = pl.BlockSpec((tf, D), lambda i, j: (jnp.where(i == 0, j, nj - 1), 0))
        in_specs.append(wd_rows)
        out_specs.append(wd_rows)
        out_shape.append(jax.ShapeDtypeStruct((F, D), BF16))
        operands.append(wd)
    res = pl.pallas_call(
        functools.partial(_ffn_act_kernel, sub=_tile(tm, ROW_BLOCK), row0=row0, tpb=tpb),
        grid=(N // tm, nj),
        in_specs=in_specs,
        out_specs=out_specs,
        out_shape=out_shape,
        scratch_shapes=[pltpu.VMEM((tm, D), BF16)],
        compiler_params=_params("arbitrary", "arbitrary"),
        name="ffn_act",
    )(*operands)
    a = res[0]
    if wd.dtype != BF16:
        wd = res[1]
    nd = D // tn
    cast_specs, cast_shapes = _side_cast_specs(side_cast, (N // tm) * nd, lambda i, j: i * nd + j)
    res = pl.pallas_call(
        functools.partial(_ffn_down_kernel, row0=row0, tpb=tpb),
        grid=(N // tm, nd),
        in_specs=[pl.BlockSpec((tm, F), lambda i, j: (i, 0)),
                  pl.BlockSpec((tm, tn), lambda i, j: (i, j)),
                  pl.BlockSpec((nrow, tn), lambda i, j: (0, (k0 + 2) * nd + j)),
                  pl.BlockSpec((F, tn), lambda i, j: (0, j))] + cast_specs,
        out_specs=[pl.BlockSpec((tm, tn), lambda i, j: (i, j))] + cast_specs,
        out_shape=[jax.ShapeDtypeStruct((N, D), F32)] + cast_shapes,
        compiler_params=_params("arbitrary", "arbitrary"),
        name="ffn_down",
    )(a, x, mod, wd, *side_cast)
    return res[0], wd, res[1:]


def _head_norm(acc, g, cos, sin, head_dim):
    quarter = head_dim // 4
    out = []
    for hh in range(acc.shape[1] // head_dim):
        xh = acc[:, hh * head_dim:(hh + 1) * head_dim]
        ms = jnp.mean(xh * xh, axis=-1, keepdims=True)
        xh = (xh * lax.rsqrt(ms + NORM_EPS)) * g
        if cos is not None:
            lane = lax.broadcasted_iota(jnp.int32, xh.shape, 1)
            partner = jnp.where((lane % (2 * quarter)) < quarter,
                                pltpu.roll(xh, head_dim - quarter, 1),
                                pltpu.roll(xh, quarter, 1))
            xh = xh * cos + partner * sin
        out.append(xh.astype(BF16))
    return out[0] if len(out) == 1 else jnp.concatenate(out, axis=1)


def _proj_kernel(x_ref, ng_ref, sh_ref, sc_ref, w_ref, cos_ref, sin_ref, kg_ref,
                 k_ref, v_ref, u_ref, *rest, bounds, rope, head_dim, sub, row0, tpb):
    h_ref = rest[-1]
    j = pl.program_id(1)
    tm = x_ref.shape[0]
    jv, ju, jq, jg = bounds
    assert jv == 1

    @pl.when(j == 0)
    def _():
        for r in range(0, tm, sub):
            rows = slice(r, r + sub)
            h = _norm_mod(x_ref[rows, :], ng_ref[...], _mod_row(sh_ref, row0, tpb),
                          _mod_row(sc_ref, row0, tpb))
            h_ref[rows, :] = h
            acc = jnp.dot(h, w_ref[...], preferred_element_type=F32)
            cos = cos_ref[rows, :] if rope else None
            sin = sin_ref[rows, :] if rope else None
            k_ref[rows, :] = _head_norm(acc, kg_ref[...], cos, sin, head_dim)

    @pl.when(j > 0)
    def _():
        acc = jnp.dot(h_ref[...], w_ref[...], preferred_element_type=F32)

        @pl.when(j < ju)
        def _():
            v_ref[...] = acc.astype(BF16)

        @pl.when((j >= ju) & (j < jq))
        def _():
            u_ref[...] = acc

        if len(rest) == 3:
            q_ref, gate_ref = rest[0], rest[1]

            @pl.when((j >= jq) & (j < jg))
            def _():
                q_ref[...] = acc

            @pl.when(j >= jg)
            def _():
                gate_ref[...] = acc.astype(BF16)


def _proj(x, ng, mod, k0, row0, w, cos, sin, kg, rows_per_seq, widths, latent):
    N, D = x.shape
    kv_w, ssm_w, attn_w, gate_w = widths
    head_dim = kg.shape[1]
    tn = kv_w
    assert tn % LANES == 0 and ssm_w % tn == 0 and attn_w % tn == 0 and gate_w % tn == 0
    jv, ju = 1, 2
    jq = ju + ssm_w // tn
    jg = jq + attn_w // tn
    nj = jg + gate_w // tn if latent else jq
    tm = _tile(rows_per_seq, TOKEN_TILE)
    tpb = rows_per_seq // tm
    nrow = mod.shape[0]
    tab = pl.BlockSpec((tm, head_dim), lambda i, j: (i % tpb, 0))
    hvec = pl.BlockSpec((1, head_dim), lambda i, j: (0, 0))

    def out_spec(j0, nblk):
        return pl.BlockSpec((tm, tn), lambda i, j: (i, jnp.clip(j - j0, 0, nblk - 1)))

    out_specs = [out_spec(0, 1), out_spec(jv, 1), out_spec(ju, jq - ju)]
    out_shape = [jax.ShapeDtypeStruct((N, kv_w), BF16), jax.ShapeDtypeStruct((N, kv_w), BF16),
                 jax.ShapeDtypeStruct((N, ssm_w), F32)]
    if latent:
        out_specs += [out_spec(jq, jg - jq), out_spec(jg, nj - jg)]
        out_shape += [jax.ShapeDtypeStruct((N, attn_w), F32), jax.ShapeDtypeStruct((N, gate_w), BF16)]
    return pl.pallas_call(
        functools.partial(_proj_kernel, bounds=(jv, ju, jq, jg), rope=latent, head_dim=head_dim,
                          sub=_tile(tm, ROW_BLOCK), row0=row0, tpb=tpb),
        grid=(N // tm, nj),
        in_specs=[pl.BlockSpec((tm, D), lambda i, j: (i, 0)),
                  pl.BlockSpec((1, D), lambda i, j: (0, 0)),
                  pl.BlockSpec((nrow, D), lambda i, j: (0, k0)),
                  pl.BlockSpec((nrow, D), lambda i, j: (0, k0 + 1)),
                  pl.BlockSpec((D, tn), lambda i, j: (0, j)),
                  tab, tab, hvec],
        out_specs=out_specs,
        out_shape=out_shape,
        scratch_shapes=[pltpu.VMEM((tm, D), BF16)],
        compiler_params=_params("arbitrary", "arbitrary"),
        name="proj_latent" if latent else "proj_context",
    )(x, ng, mod, mod, w, cos, sin, kg)


def _rope_tables(L, head_dim):
    t = np.arange(L)
    row = (t // GRID_W).astype(np.float64)
    col = (t % GRID_W).astype(np.float64)
    half = head_dim // 4
    inv_freq = ROPE_THETA ** (-np.arange(half, dtype=np.float64) / half)
    ar = row[:, None] * inv_freq
    ac = col[:, None] * inv_freq
    cos = np.concatenate([np.cos(ar), np.cos(ar), np.cos(ac), np.cos(ac)], axis=-1)
    sin = np.concatenate([-np.sin(ar), np.sin(ar), -np.sin(ac), np.sin(ac)], axis=-1)
    return jnp.asarray(cos, F32), jnp.asarray(sin, F32)


def _attn_kernel(q_ref, cos_ref, sin_ref, qn_ref, cosn_ref, sinn_ref, qg_ref,
                 kc_ref, k_ref, vc_ref, v_ref, o_ref, kall_ref, vext_ref, qs_ref, *, head_dim):
    lc = kc_ref.shape[0]
    i = pl.program_id(2)
    step = (pl.program_id(0) * pl.num_programs(1) + pl.program_id(1)) * pl.num_programs(2) + i

    @pl.when(step == 0)
    def _():
        qs_ref[0] = _head_norm(q_ref[...], qg_ref[...], cos_ref[...], sin_ref[...], head_dim)

    @pl.when(i == 0)
    def _():
        kall_ref[0:lc, :] = kc_ref[...]
        kall_ref[lc:, :] = k_ref[...]
        vext_ref[0:lc, 0:head_dim] = vc_ref[...]
        vext_ref[lc:, 0:head_dim] = v_ref[...]
        vext_ref[:, head_dim:] = jnp.ones((vext_ref.shape[0], head_dim), BF16)

    k = kall_ref[...]
    vext = vext_ref[...]
    n_rep = q_ref.shape[1] // head_dim
    heads = [slice(r * head_dim, (r + 1) * head_dim) for r in range(n_rep)]

    def scores(sl):
        return lax.dot_general(qs_ref[step % 2, :, sl], k, (((1,), (1,)), ((), ())),
                               preferred_element_type=F32)

    s_next = scores(heads[0])
    for r, sl in enumerate(heads):
        s = s_next
        if r + 1 < n_rep:
            s_next = scores(heads[r + 1])
        qs_ref[(step + 1) % 2, :, sl] = _head_norm(qn_ref[:, sl], qg_ref[...], cosn_ref[...],
                                                   sinn_ref[...], head_dim)
        m = jnp.max(s, axis=-1, keepdims=True)
        p = jnp.exp2(s - m).astype(BF16)
        oe = jnp.dot(p, vext, preferred_element_type=F32)
        o_ref[:, sl] = (oe[:, :head_dim] / oe[:, head_dim:]).astype(o_ref.dtype)


def _attention(q, cos, sin, qg, kc, k, vc, v, B, head_dim):
    N, W = q.shape
    L, Lc = N // B, kc.shape[0] // B
    kvh = k.shape[1] // head_dim
    gw = W // kvh
    tq = _tile(L, Q_TILE)
    tpb = L // tq
    kv_lat = pl.BlockSpec((L, head_dim), lambda b, h, i: (b, h))
    kv_ctx = pl.BlockSpec((Lc, head_dim), lambda b, h, i: (b, h))

    def nxt(b, h, i):
        wrap = i + 1 >= tpb
        g = jnp.minimum(b * kvh + h + 1, B * kvh - 1)
        return (jnp.where(wrap, g // kvh, b), jnp.where(wrap, g % kvh, h),
                jnp.where(wrap, 0, i + 1))

    def q_next(b, h, i):
        nb, nh, ni = nxt(b, h, i)
        return nb * tpb + ni, nh

    tab = pl.BlockSpec((tq, head_dim), lambda b, h, i: (i, 0))
    tab_next = pl.BlockSpec((tq, head_dim), lambda b, h, i: (nxt(b, h, i)[2], 0))
    return pl.pallas_call(
        functools.partial(_attn_kernel, head_dim=head_dim),
        grid=(B, kvh, tpb),
        in_specs=[pl.BlockSpec((tq, gw), lambda b, h, i: (b * tpb + i, h)), tab, tab,
                  pl.BlockSpec((tq, gw), q_next), tab_next, tab_next,
                  pl.BlockSpec((1, head_dim), lambda b, h, i: (0, 0)),
                  kv_ctx, kv_lat, kv_ctx, kv_lat],
        out_specs=pl.BlockSpec((tq, gw), lambda b, h, i: (b * tpb + i, h)),
        out_shape=jax.ShapeDtypeStruct((N, W), BF16),
        scratch_shapes=[pltpu.VMEM((L + Lc, head_dim), BF16),
                        pltpu.VMEM((L + Lc, 2 * head_dim), BF16),
                        pltpu.VMEM((2, tq, gw), BF16)],
        compiler_params=_params("arbitrary", "arbitrary", "arbitrary"),
        name="attention",
    )(q, cos, sin, q, cos, sin, qg, kc, k, vc, v)


def _ssm_operators(a_re, a_im, log_dt, b_re, b_im, c_re, c_im):
    T = SSM_CHUNK
    a_re, a_im = a_re.astype(F32), a_im.astype(F32)
    dt = jnp.exp(log_dt.astype(F32))[..., None]
    mag = jnp.exp(a_re * dt)
    lr = mag * jnp.cos(a_im * dt)
    li = mag * jnp.sin(a_im * dt)
    den = a_re * a_re + a_im * a_im
    cr = ((lr - 1.0) * a_re + li * a_im) / den
    ci = (li * a_re - (lr - 1.0) * a_im) / den
    pr, pi = [jnp.ones_like(lr)], [jnp.zeros_like(lr)]
    for _ in range(T):
        pr.append(pr[-1] * lr - pi[-1] * li)
        pi.append(pr[-2] * li + pi[-1] * lr)
    pw_r, pw_i = jnp.stack(pr), jnp.stack(pi)
    _, G, P, E = b_re.shape
    bt_r = jnp.swapaxes(b_re.astype(F32), 2, 3)
    bt_i = jnp.swapaxes(b_im.astype(F32), 2, 3)
    bb_r = cr[:, :, None, :] * bt_r - ci[:, :, None, :] * bt_i
    bb_i = cr[:, :, None, :] * bt_i + ci[:, :, None, :] * bt_r
    c_r, c_i = c_re.astype(F32), c_im.astype(F32)

    def lag_table(pw, fwd_lags, bwd_lags, reps):
        tab = jnp.stack([jnp.moveaxis(pw[fwd_lags, 0], 0, 1), jnp.moveaxis(pw[bwd_lags, 1], 0, 1)])
        return jnp.concatenate([tab] * reps, axis=-1)

    steps = np.arange(T)
    lags = np.arange(T + 1)
    p1 = lag_table(pw_r, T - 1 - steps, steps, 4)
    p2 = lag_table(pw_i, T - 1 - steps, steps, 4)
    q1 = lag_table(pw_r, lags, T - lags, 2)
    q2 = lag_table(pw_i, lags, T - lags, 2)
    bb1 = jnp.concatenate([bb_r, bb_i, bb_i, bb_r], axis=-1)
    bb2 = jnp.concatenate([-bb_i, bb_r, bb_r, -bb_i], axis=-1)
    cc1 = jnp.concatenate([c_r, -c_i], axis=-1)
    cc2 = jnp.concatenate([-c_i, -c_r], axis=-1)
    ar, ai = pw_r[T], pw_i[T]
    dec = jnp.stack([jnp.concatenate([ar, ar], -1), jnp.concatenate([-ai, ai], -1)], axis=2)
    return p1, p2, bb1, bb2, q1, q2, cc1, cc2, dec


def _state_in_operator(p1_ref, p2_ref, bb1_ref, bb2_ref, d, g):
    bb1, bb2 = bb1_ref[d, g], bb2_ref[d, g]
    rows = [p1_ref[d, g, s:s + 1, :] * bb1 + p2_ref[d, g, s:s + 1, :] * bb2
            for s in range(SSM_CHUNK)]
    return jnp.concatenate(rows, axis=0).astype(BF16)


def _readout_table(q1_ref, q2_ref, cc1_ref, cc2_ref, d, g):
    cc1, cc2 = cc1_ref[d, g], cc2_ref[d, g]
    rows = [q1_ref[d, g, j:j + 1, :] * cc1 + q2_ref[d, g, j:j + 1, :] * cc2
            for j in range(SSM_CHUNK + 1)]
    return jnp.concatenate(rows, axis=0)


def _intra_operator(wf, wb, e):
    T = SSM_CHUNK
    t_of_lane = lax.broadcasted_iota(jnp.int32, wf.shape, 1) // e
    rows = []
    for s in range(T):
        f = wf if s == 0 else jnp.where(t_of_lane >= s, pltpu.roll(wf, s * e, 1), 0.0)
        b = wb if s == T - 1 else jnp.where(t_of_lane <= s, pltpu.roll(wb, (s + 1) * e, 1), 0.0)
        rows.append(f + b)
    return jnp.concatenate(rows, axis=0).astype(BF16)


def _row_pitch(n):
    p = -(-n // SUBLANES)
    return SUBLANES * (p if p % 2 else p + 1)


def _ssm_kernel(u_ref, uc_ref, d_ref, p1_ref, p2_ref, bb1_ref, bb2_ref, q1_ref, q2_ref,
                cc1_ref, cc2_ref, dec_ref, y_ref, z_ref, s1_ref, s2_ref, hp_ref, *, batch, e):
    T = SSM_CHUNK
    gpb = dec_ref.shape[1]
    rp = z_ref.shape[1]
    n_lat, n_ctx = u_ref.shape[0] // T, uc_ref.shape[0] // T
    ncl, ncc = n_lat // batch, n_ctx // batch
    pl_, pc_ = _row_pitch(ncl), _row_pitch(ncc)
    lat0 = batch * pc_
    r = lat0 + batch * pl_
    lanes = u_ref.shape[1]
    pw = hp_ref.shape[3]

    def padded(rows, n, pitch):
        if pitch == n:
            return [rows]
        out = []
        for b in range(batch):
            out += [rows[b * n:(b + 1) * n, :], jnp.zeros((pitch - n, lanes), F32)]
        return out

    xt = []
    for s in range(T):
        parts = (padded(uc_ref[pl.ds(s, n_ctx, stride=T), :], ncc, pc_)
                 + padded(u_ref[pl.ds(s, n_lat, stride=T), :], ncl, pl_))
        if rp > r:
            parts.append(jnp.zeros((rp - r, lanes), F32))
        xt.append(jnp.concatenate(parts, axis=0).astype(BF16).T)
    for g in range(gpb):
        zt = jnp.concatenate([xt[s][g * e:(g + 1) * e, :] for s in range(T)], axis=0)
        z_ref[g] = zt.T
    for d in range(2):
        for g in range(gpb):
            st = jnp.dot(z_ref[g], _state_in_operator(p1_ref, p2_ref, bb1_ref, bb2_ref, d, g),
                         preferred_element_type=F32)
            s1_ref[d, g] = st[:, :pw]
            s2_ref[d, g] = st[:, pw:]

    def sweep(base, count, stride, carry):
        def body(it, carry):
            out = []
            for d in range(2):
                n = it if d == 0 else count - 1 - it
                rows = pl.ds(base + n, batch, stride=stride)
                for g in range(gpb):
                    h1, h2 = carry[2 * (d * gpb + g)], carry[2 * (d * gpb + g) + 1]
                    hp_ref[d, g, rows, :] = h1
                    a1 = dec_ref[d, g, 0:1, :]
                    a2 = dec_ref[d, g, 1:2, :]
                    out.append(a1 * h1 + a2 * h2 + s1_ref[d, g, rows, :])
                    out.append(a1 * h2 - a2 * h1 + s2_ref[d, g, rows, :])
            return tuple(out)
        return lax.fori_loop(0, count, body, carry)

    for d in range(2):
        for g in range(gpb):
            for b in range(batch):
                if pc_ > ncc:
                    hp_ref[d, g, b * pc_ + ncc:(b + 1) * pc_, :] = jnp.zeros((pc_ - ncc, pw), F32)
                if pl_ > ncl:
                    hp_ref[d, g, lat0 + b * pl_ + ncl:lat0 + (b + 1) * pl_, :] = jnp.zeros((pl_ - ncl, pw), F32)
            if rp > r:
                hp_ref[d, g, r:rp, :] = jnp.zeros((rp - r, pw), F32)
    zero = jnp.zeros((batch, pw), F32)
    carry = sweep(0, ncc, pc_, tuple([zero] * (4 * gpb)))
    sweep(lat0, ncl, pl_, carry)
    nt = (((1,), (1,)), ((), ()))
    te = T * e
    yt = []
    for g in range(gpb):
        rf = _readout_table(q1_ref, q2_ref, cc1_ref, cc2_ref, 0, g)
        rb = _readout_table(q1_ref, q2_ref, cc1_ref, cc2_ref, 1, g)
        wf = lax.dot_general(bb1_ref[0, g][:, :pw], rf[:te], nt, precision=lax.Precision.HIGHEST,
                             preferred_element_type=F32)
        wb = lax.dot_general(bb1_ref[1, g][:, :pw], rb[e:], nt, precision=lax.Precision.HIGHEST,
                             preferred_element_type=F32)
        y = jnp.dot(z_ref[g], _intra_operator(wf, wb, e), preferred_element_type=F32)
        y += lax.dot_general(hp_ref[0, g].astype(BF16), rf[e:].astype(BF16), nt,
                             preferred_element_type=F32)
        y += lax.dot_general(hp_ref[1, g].astype(BF16), rb[:te].astype(BF16), nt,
                             preferred_element_type=F32)
        yt.append(y.T)
    dvec = d_ref[...]
    for t in range(T):
        blk = jnp.concatenate([yt[g][t * e:(t + 1) * e, :] for g in range(gpb)], axis=0).T
        for b in range(batch):
            rows = pl.ds(b * ncl * T + t, ncl, stride=T)
            y_ref[rows, :] = (blk[lat0 + b * pl_:lat0 + b * pl_ + ncl, :] + dvec * u_ref[rows, :])


def _s5(u, uc, d, ops, batch):
    dec = ops[-1]
    N, W = u.shape
    Nc = uc.shape[0]
    G = dec.shape[1]
    E = W // G
    gpb = LANES // E
    TE = SSM_CHUNK * E
    P2 = dec.shape[-1]
    assert P2 == LANES and W % LANES == 0

    def table(a):
        return pl.BlockSpec((2, gpb) + a.shape[2:], lambda j: (0, j, 0, 0))
    r = batch * (_row_pitch(N // SSM_CHUNK // batch) + _row_pitch(Nc // SSM_CHUNK // batch))
    rp = -(-r // LANES) * LANES
    return pl.pallas_call(
        functools.partial(_ssm_kernel, batch=batch, e=E),
        grid=(W // LANES,),
        in_specs=[pl.BlockSpec((N, LANES), lambda j: (0, j)),
                  pl.BlockSpec((Nc, LANES), lambda j: (0, j)),
                  pl.BlockSpec((1, LANES), lambda j: (0, j))] + [table(a) for a in ops],
        out_specs=pl.BlockSpec((N, LANES), lambda j: (0, j)),
        out_shape=jax.ShapeDtypeStruct((N, W), F32),
        scratch_shapes=[pltpu.VMEM((gpb, rp, TE), BF16),
                        pltpu.VMEM((2, gpb, rp, P2), F32),
                        pltpu.VMEM((2, gpb, rp, P2), F32),
                        pltpu.VMEM((2, gpb, rp, P2), F32)],
        compiler_params=_params("arbitrary"),
        name="s5",
    )(u, uc, d, *ops)


def _mixer_out_kernel(attn_ref, y_ref, ga_ref, gs_ref, x_ref, g2_ref, wglu_ref, bglu_ref,
                      wa_ref, ws_ref, wo_ref, o_ref, y2_ref, m_ref, *, sub, tpb):
    j = pl.program_id(1)
    tm, tn = o_ref.shape
    nd = m_ref.shape[1] // tn
    half = tn // 2

    def merge(rows, attn, y2):
        for c in (0, half):
            pa = jnp.dot(attn, wa_ref[:, c:c + half], preferred_element_type=F32)
            ps = jnp.dot(y2, ws_ref[:, c:c + half], preferred_element_type=F32)
            merged = (jax.nn.sigmoid(ga_ref[rows, c:c + half].astype(F32)) * pa
                      + jax.nn.sigmoid(gs_ref[rows, c:c + half].astype(F32)) * ps)
            m_ref[rows, pl.ds(pl.multiple_of(j * tn + c, half), half)] = merged.astype(BF16)

    @pl.when(j == 0)
    def _():
        for r in range(0, tm, sub):
            rows = slice(r, r + sub)
            y = jax.nn.gelu(y_ref[rows, :])
            z = jnp.dot(y.astype(BF16), wglu_ref[...], preferred_element_type=F32) + bglu_ref[...]
            y2 = (y * jax.nn.sigmoid(z)).astype(BF16)
            y2_ref[rows, :] = y2
            merge(rows, attn_ref[rows, :], y2)

    @pl.when((j > 0) & (j < nd))
    def _():
        merge(slice(None), attn_ref[...], y2_ref[...])

    @pl.when(j >= nd)
    def _():
        for c in (0, half):
            col = pl.ds(pl.multiple_of((j - nd) * tn + c, half), half)
            acc = jnp.dot(m_ref[...], wo_ref[:, col], preferred_element_type=F32)
            g2 = _mod_row(g2_ref, 0, tpb)
            o_ref[:, c:c + half] = x_ref[:, c:c + half] + g2[:, c:c + half] * acc


def _mixer_out(attn, y_ssm, gate, x, mod, k_gate, w_glu, b_glu, wa, ws, wo, rows_per_mod):
    N, D = x.shape
    WA, WS = attn.shape[1], y_ssm.shape[1]
    tm = _tile(rows_per_mod, TOKEN_TILE)
    tn = _tile(D, COL_TILE, MXU_WIDTH)
    nd = D // tn
    tpb = rows_per_mod // tm
    lo = lambda j: jnp.minimum(j, nd - 1)
    hi = lambda j: jnp.maximum(j - nd, 0)
    return pl.pallas_call(
        functools.partial(_mixer_out_kernel, sub=_tile(tm, ROW_BLOCK), tpb=tpb),
        grid=(N // tm, 2 * nd),
        in_specs=[pl.BlockSpec((tm, WA), lambda i, j: (i, 0)),
                  pl.BlockSpec((tm, WS), lambda i, j: (i, 0)),
                  pl.BlockSpec((tm, tn), lambda i, j: (i, lo(j))),
                  pl.BlockSpec((tm, tn), lambda i, j: (i, nd + lo(j))),
                  pl.BlockSpec((tm, tn), lambda i, j: (i, hi(j))),
                  pl.BlockSpec((mod.shape[0], tn), lambda i, j: (0, k_gate * nd + hi(j))),
                  pl.BlockSpec((WS, WS), lambda i, j: (0, 0), pipeline_mode=pl.Buffered(1)),
                  pl.BlockSpec((1, WS), lambda i, j: (0, 0)),
                  pl.BlockSpec((WA, tn), lambda i, j: (0, lo(j))),
                  pl.BlockSpec((WS, tn), lambda i, j: (0, lo(j))),
                  pl.BlockSpec((D, D), lambda i, j: (0, 0), pipeline_mode=pl.Buffered(1))],
        out_specs=pl.BlockSpec((tm, tn), lambda i, j: (i, hi(j))),
        out_shape=jax.ShapeDtypeStruct((N, D), F32),
        scratch_shapes=[pltpu.VMEM((tm, WS), BF16), pltpu.VMEM((tm, D), BF16)],
        compiler_params=_params("arbitrary", "arbitrary"),
        name="mixer_out",
    )(attn, y_ssm, gate, gate, x, mod, w_glu, b_glu, wa, ws, wo)


def kernel(x, c, ctx, c_ctx, w_mod, b_mod, norm_g, w_ffn1_gate, w_ffn1_up, w_ffn1_down, w_in, q_norm_g, k_norm_g, ssm_a_re, ssm_a_im, ssm_log_dt, ssm_b_re, ssm_b_im, ssm_c_re, ssm_c_im, ssm_d, w_glu, b_glu, w_br_attn, w_br_ssm, w_out, w_ffn2_gate, w_ffn2_up, w_ffn2_down):
    B, L, D = x.shape
    Lc = ctx.shape[1]
    assert w_mod.shape[0] == 1, "only the single (last) layer configuration is implemented"
    assert w_mod.shape[2] == N_MOD * D
    hd = q_norm_g.shape[1]
    ssm_w = w_glu.shape[1]
    attn_w = w_br_attn.shape[1]
    kv_w = attn_w // Q_PER_KV
    widths = (kv_w, ssm_w, attn_w, 2 * D)
    assert L % GRID_W == 0 and L % SSM_CHUNK == 0 and Lc % SSM_CHUNK == 0 and B < SUBLANES
    N, Nc = B * L, B * Lc
    l = 0

    cc = jnp.zeros((SUBLANES, D), F32).at[:B].set(c).at[B].set(c_ctx)
    mod = _modulation(cc, w_mod[l], b_mod[l][None, :])
    ng = norm_g[l][:, None, :]

    x2 = x.reshape(N, D)
    c2 = ctx.reshape(Nc, D)

    x2, wd1, (w_in_b,) = _ffn(x2, ng[0], mod, 0, 0, w_ffn1_gate[l], w_ffn1_up[l],
                              w_ffn1_down[l], L, side_cast=(w_in[l],))
    c2, _, _ = _ffn(c2, ng[0], mod, 0, B, w_ffn1_gate[l], w_ffn1_up[l], wd1, Nc)

    cos, sin = _rope_tables(L, hd)
    kg = k_norm_g[l][None, :]
    qg = q_norm_g[l][None, :] * (hd ** -0.5 * math.log2(math.e))
    k, v, u, q, gate = _proj(x2, ng[1], mod, 3, 0, w_in_b, cos, sin, kg, L, widths, True)
    kc, vc, uc = _proj(c2, ng[1], mod, 3, B, w_in_b, cos, sin, kg, Nc, widths, False)

    attn = _attention(q, cos, sin, qg, kc, k, vc, v, B, hd)
    ops = _ssm_operators(ssm_a_re[l], ssm_a_im[l], ssm_log_dt[l], ssm_b_re[l], ssm_b_im[l],
                         ssm_c_re[l], ssm_c_im[l])
    y_ssm = _s5(u, uc, ssm_d[l][None, :], ops, B)

    x2 = _mixer_out(attn, y_ssm, gate, x2, mod, 5, w_glu[l].astype(BF16), b_glu[l][None, :],
                    w_br_attn[l].astype(BF16), w_br_ssm[l].astype(BF16), w_out[l].astype(BF16), L)

    x2, _, _ = _ffn(x2, ng[2], mod, 6, 0, w_ffn2_gate[l], w_ffn2_up[l], w_ffn2_down[l], L)
    return x2.reshape(B, L, D)
```

```python
import functools
import math

import jax
import jax.numpy as jnp
import numpy as np
from jax import lax
from jax.experimental import pallas as pl
from jax.experimental.pallas import tpu as pltpu

F32 = jnp.float32
BF16 = jnp.bfloat16

GRID_W = 64
ROPE_THETA = 10000.0
NORM_EPS = 1e-6
N_MOD = 9
Q_PER_KV = 4
SSM_CHUNK = 16
LANES = 128
SUBLANES = 8
BF16_ROWS = 16
MXU_WIDTH = 256
VMEM_LIMIT_BYTES = 58 * 1024 * 1024
TOKEN_TILE = 1024
COL_TILE = 512
ROW_BLOCK = 256
Q_TILE = 512
MOD_COL_TILE = 1024


def _tile(n, pref, mult=SUBLANES):
    if n <= pref:
        return n
    for t in range(pref, 0, -1):
        if n % t == 0 and t % mult == 0:
            return t
    return n


def _params(*sem):
    return pltpu.CompilerParams(dimension_semantics=sem, vmem_limit_bytes=VMEM_LIMIT_BYTES)


def _as_bf16(w):
    return w if w.dtype == BF16 else w.astype(BF16)


def _norm_mod(x, g, sh, sc):
    ms = jnp.mean(x * x, axis=-1, keepdims=True)
    h = (x * lax.rsqrt(ms + NORM_EPS)) * g
    return (h * (1.0 + sc) + sh).astype(BF16)


def _mod_kernel(c_ref, w_ref, b_ref, o_ref):
    c = c_ref[...]
    s = c * jax.nn.sigmoid(c)
    o_ref[...] = jnp.dot(s.astype(BF16), w_ref[...].astype(BF16),
                         preferred_element_type=F32) + b_ref[...]


def _modulation(cc, w_mod, b_mod):
    D, ncols = w_mod.shape
    tn = _tile(ncols, MOD_COL_TILE, LANES)
    return pl.pallas_call(
        _mod_kernel,
        grid=(ncols // tn,),
        in_specs=[pl.BlockSpec((cc.shape[0], D), lambda j: (0, 0)),
                  pl.BlockSpec((D, tn), lambda j: (0, j)),
                  pl.BlockSpec((1, tn), lambda j: (0, j))],
        out_specs=pl.BlockSpec((cc.shape[0], tn), lambda j: (0, j)),
        out_shape=jax.ShapeDtypeStruct((cc.shape[0], ncols), F32),
        compiler_params=_params("arbitrary"),
        name="modulation",
    )(cc, w_mod, b_mod)


def _mod_row(ref, row0, tiles_per_row):
    return ref[pl.ds(row0 + pl.program_id(0) // tiles_per_row, 1), :]


def _ffn_act_kernel(x_ref, ng_ref, sh_ref, sc_ref, wg_ref, wu_ref, *rest, sub, row0, tpb):
    if len(rest) == 4:
        wd_ref, a_ref, wdb_ref, h_ref = rest

        @pl.when(pl.program_id(0) == 0)
        def _():
            wdb_ref[...] = wd_ref[...].astype(BF16)
    else:
        a_ref, h_ref = rest

    half = a_ref.shape[1] // 2
    j = pl.program_id(1)

    def emit(row_blocks):
        halves = [(c, _as_bf16(wg_ref[:, c:c + half]), _as_bf16(wu_ref[:, c:c + half]))
                  for c in (0, half)]
        for rows, make_h in row_blocks:
            h = make_h()
            for c, wg, wu in halves:
                g = jnp.dot(h, wg, preferred_element_type=F32)
                u = jnp.dot(h, wu, preferred_element_type=F32)
                a_ref[rows, c:c + half] = ((g * jax.nn.sigmoid(g)) * u).astype(a_ref.dtype)

    def normed(rows):
        def make_h():
            h = _norm_mod(x_ref[rows, :], ng_ref[...], _mod_row(sh_ref, row0, tpb),
                          _mod_row(sc_ref, row0, tpb))
            h_ref[rows, :] = h
            return h
        return make_h

    @pl.when(j == 0)
    def _():
        emit([(slice(r, r + sub), normed(slice(r, r + sub))) for r in range(0, x_ref.shape[0], sub)])

    @pl.when(j > 0)
    def _():
        emit([(slice(None), lambda: h_ref[...])])


def _side_cast_specs(arrays, nsteps, step_of):
    specs, shapes = [], []
    for a in arrays:
        rows = a.shape[0] // nsteps
        assert a.shape[0] % nsteps == 0 and rows % BF16_ROWS == 0, (a.shape, nsteps)
        specs.append(pl.BlockSpec((rows, a.shape[1]), lambda *ids: (step_of(*ids), 0)))
        shapes.append(jax.ShapeDtypeStruct(a.shape, BF16))
    return specs, shapes


def _side_cast(src_refs, dst_refs):
    for src, dst in zip(src_refs, dst_refs):
        dst[...] = src[...].astype(BF16)


def _ffn_down_kernel(a_ref, x_ref, gate_ref, wd_ref, *rest, row0, tpb):
    n_cast = len(rest) // 2
    o_ref = rest[n_cast]
    _side_cast(rest[:n_cast], rest[n_cast + 1:])
    half = o_ref.shape[1] // 2
    half_gate = 0.5 * _mod_row(gate_ref, row0, tpb)
    for c in (0, half):
        acc = jnp.dot(a_ref[...], wd_ref[:, c:c + half], preferred_element_type=F32)
        o_ref[:, c:c + half] = x_ref[:, c:c + half] + half_gate[:, c:c + half] * acc


def _ffn(x, ng, mod, k0, row0, wg, wu, wd, rows_per_mod, side_cast=()):
    N, D = x.shape
    F = wg.shape[1]
    tm = _tile(rows_per_mod, TOKEN_TILE)
    tf = _tile(F, COL_TILE, MXU_WIDTH)
    tn = _tile(D, COL_TILE, MXU_WIDTH)
    tpb = rows_per_mod // tm
    nj = F // tf
    nrow = mod.shape[0]
    in_specs = [pl.BlockSpec((tm, D), lambda i, j: (i, 0)),
                pl.BlockSpec((1, D), lambda i, j: (0, 0)),
                pl.BlockSpec((nrow, D), lambda i, j: (0, k0)),
                pl.BlockSpec((nrow, D), lambda i, j: (0, k0 + 1)),
                pl.BlockSpec((D, tf), lambda i, j: (0, j)),
                pl.BlockSpec((D, tf), lambda i, j: (0, j))]
    out_specs = [pl.BlockSpec((tm, tf), lambda i, j: (i, j))]
    out_shape = [jax.ShapeDtypeStruct((N, F), BF16)]
    operands = [x, ng, mod, mod, wg, wu]
    if wd.dtype != BF16:
        wd_rows = pl.BlockSpec((tf, D), lambda i, j: (jnp.where(i == 0, j, nj - 1), 0))
        in_specs.append(wd_rows)
        out_specs.append(wd_rows)
        out_shape.append(jax.ShapeDtypeStruct((F, D), BF16))
        operands.append(wd)
    res = pl.pallas_call(
        functools.partial(_ffn_act_kernel, sub=_tile(tm, ROW_BLOCK), row0=row0, tpb=tpb),
        grid=(N // tm, nj),
        in_specs=in_specs,
        out_specs=out_specs,
        out_shape=out_shape,
        scratch_shapes=[pltpu.VMEM((tm, D), BF16)],
        compiler_params=_params("arbitrary", "arbitrary"),
        name="ffn_act",
    )(*operands)
    a = res[0]
    if wd.dtype != BF16:
        wd = res[1]
    nd = D // tn
    cast_specs, cast_shapes = _side_cast_specs(side_cast, (N // tm) * nd, lambda i, j: i * nd + j)
    res = pl.pallas_call(
        functools.partial(_ffn_down_kernel, row0=row0, tpb=tpb),
        grid=(N // tm, nd),
        in_specs=[pl.BlockSpec((tm, F), lambda i, j: (i, 0)),
                  pl.BlockSpec((tm, tn), lambda i, j: (i, j)),
                  pl.BlockSpec((nrow, tn), lambda i, j: (0, (k0 + 2) * nd + j)),
                  pl.BlockSpec((F, tn), lambda i, j: (0, j))] + cast_specs,
        out_specs=[pl.BlockSpec((tm, tn), lambda i, j: (i, j))] + cast_specs,
        out_shape=[jax.ShapeDtypeStruct((N, D), F32)] + cast_shapes,
        compiler_params=_params("arbitrary", "arbitrary"),
        name="ffn_down",
    )(a, x, mod, wd, *side_cast)
    return res[0], wd, res[1:]


def _head_norm(acc, g, cos, sin, head_dim):
    quarter = head_dim // 4
    out = []
    for hh in range(acc.shape[1] // head_dim):
        xh = acc[:, hh * head_dim:(hh + 1) * head_dim]
        ms = jnp.mean(xh * xh, axis=-1, keepdims=True)
        xh = (xh * lax.rsqrt(ms + NORM_EPS)) * g
        if cos is not None:
            lane = lax.broadcasted_iota(jnp.int32, xh.shape, 1)
            partner = jnp.where((lane % (2 * quarter)) < quarter,
                                pltpu.roll(xh, head_dim - quarter, 1),
                                pltpu.roll(xh, quarter, 1))
            xh = xh * cos + partner * sin
        out.append(xh.astype(BF16))
    return out[0] if len(out) == 1 else jnp.concatenate(out, axis=1)


def _proj_kernel(x_ref, ng_ref, sh_ref, sc_ref, w_ref, cos_ref, sin_ref, kg_ref,
                 k_ref, v_ref, u_ref, *rest, bounds, rope, head_dim, sub, row0, tpb):
    h_ref = rest[-1]
    j = pl.program_id(1)
    tm = x_ref.shape[0]
    jv, ju, jq, jg = bounds
    assert jv == 1

    @pl.when(j == 0)
    def _():
        for r in range(0, tm, sub):
            rows = slice(r, r + sub)
            h = _norm_mod(x_ref[rows, :], ng_ref[...], _mod_row(sh_ref, row0, tpb),
                          _mod_row(sc_ref, row0, tpb))
            h_ref[rows, :] = h
            acc = jnp.dot(h, w_ref[...], preferred_element_type=F32)
            cos = cos_ref[rows, :] if rope else None
            sin = sin_ref[rows, :] if rope else None
            k_ref[rows, :] = _head_norm(acc, kg_ref[...], cos, sin, head_dim)

    @pl.when(j > 0)
    def _():
        acc = jnp.dot(h_ref[...], w_ref[...], preferred_element_type=F32)

        @pl.when(j < ju)
        def _():
            v_ref[...] = acc.astype(BF16)

        @pl.when((j >= ju) & (j < jq))
        def _():
            u_ref[...] = acc

        if len(rest) == 3:
            q_ref, gate_ref = rest[0], rest[1]

            @pl.when((j >= jq) & (j < jg))
            def _():
                q_ref[...] = acc

            @pl.when(j >= jg)
            def _():
                gate_ref[...] = acc.astype(BF16)


def _proj(x, ng, mod, k0, row0, w, cos, sin, kg, rows_per_seq, widths, latent):
    N, D = x.shape
    kv_w, ssm_w, attn_w, gate_w = widths
    head_dim = kg.shape[1]
    tn = kv_w
    assert tn % LANES == 0 and ssm_w % tn == 0 and attn_w % tn == 0 and gate_w % tn == 0
    jv, ju = 1, 2
    jq = ju + ssm_w // tn
    jg = jq + attn_w // tn
    nj = jg + gate_w // tn if latent else jq
    tm = _tile(rows_per_seq, TOKEN_TILE)
    tpb = rows_per_seq // tm
    nrow = mod.shape[0]
    tab = pl.BlockSpec((tm, head_dim), lambda i, j: (i % tpb, 0))
    hvec = pl.BlockSpec((1, head_dim), lambda i, j: (0, 0))

    def out_spec(j0, nblk):
        return pl.BlockSpec((tm, tn), lambda i, j: (i, jnp.clip(j - j0, 0, nblk - 1)))

    out_specs = [out_spec(0, 1), out_spec(jv, 1), out_spec(ju, jq - ju)]
    out_shape = [jax.ShapeDtypeStruct((N, kv_w), BF16), jax.ShapeDtypeStruct((N, kv_w), BF16),
                 jax.ShapeDtypeStruct((N, ssm_w), F32)]
    if latent:
        out_specs += [out_spec(jq, jg - jq), out_spec(jg, nj - jg)]
        out_shape += [jax.ShapeDtypeStruct((N, attn_w), F32), jax.ShapeDtypeStruct((N, gate_w), BF16)]
    return pl.pallas_call(
        functools.partial(_proj_kernel, bounds=(jv, ju, jq, jg), rope=latent, head_dim=head_dim,
                          sub=_tile(tm, ROW_BLOCK), row0=row0, tpb=tpb),
        grid=(N // tm, nj),
        in_specs=[pl.BlockSpec((tm, D), lambda i, j: (i, 0)),
                  pl.BlockSpec((1, D), lambda i, j: (0, 0)),
                  pl.BlockSpec((nrow, D), lambda i, j: (0, k0)),
                  pl.BlockSpec((nrow, D), lambda i, j: (0, k0 + 1)),
                  pl.BlockSpec((D, tn), lambda i, j: (0, j)),
                  tab, tab, hvec],
        out_specs=out_specs,
        out_shape=out_shape,
        scratch_shapes=[pltpu.VMEM((tm, D), BF16)],
        compiler_params=_params("arbitrary", "arbitrary"),
        name="proj_latent" if latent else "proj_context",
    )(x, ng, mod, mod, w, cos, sin, kg)


def _rope_tables(L, head_dim):
    t = np.arange(L)
    row = (t // GRID_W).astype(np.float64)
    col = (t % GRID_W).astype(np.float64)
    half = head_dim // 4
    inv_freq = ROPE_THETA ** (-np.arange(half, dtype=np.float64) / half)
    ar = row[:, None] * inv_freq
    ac = col[:, None] * inv_freq
    cos = np.concatenate([np.cos(ar), np.cos(ar), np.cos(ac), np.cos(ac)], axis=-1)
    sin = np.concatenate([-np.sin(ar), np.sin(ar), -np.sin(ac), np.sin(ac)], axis=-1)
    return jnp.asarray(cos, F32), jnp.asarray(sin, F32)


def _attn_kernel(q_ref, cos_ref, sin_ref, qn_ref, cosn_ref, sinn_ref, qg_ref,
                 kc_ref, k_ref, vc_ref, v_ref, o_ref, kall_ref, vext_ref, qs_ref, *, head_dim):
    lc = kc_ref.shape[0]
    i = pl.program_id(2)
    step = (pl.program_id(0) * pl.num_programs(1) + pl.program_id(1)) * pl.num_programs(2) + i

    @pl.when(step == 0)
    def _():
        qs_ref[0] = _head_norm(q_ref[...], qg_ref[...], cos_ref[...], sin_ref[...], head_dim)

    @pl.when(i == 0)
    def _():
        kall_ref[0:lc, :] = kc_ref[...]
        kall_ref[lc:, :] = k_ref[...]
        vext_ref[0:lc, 0:head_dim] = vc_ref[...]
        vext_ref[lc:, 0:head_dim] = v_ref[...]
        vext_ref[:, head_dim:] = jnp.ones((vext_ref.shape[0], head_dim), BF16)

    k = kall_ref[...]
    vext = vext_ref[...]
    n_rep = q_ref.shape[1] // head_dim
    heads = [slice(r * head_dim, (r + 1) * head_dim) for r in range(n_rep)]

    def scores(sl):
        return lax.dot_general(qs_ref[step % 2, :, sl], k, (((1,), (1,)), ((), ())),
                               preferred_element_type=F32)

    s_next = scores(heads[0])
    for r, sl in enumerate(heads):
        s = s_next
        if r + 1 < n_rep:
            s_next = scores(heads[r + 1])
        qs_ref[(step + 1) % 2, :, sl] = _head_norm(qn_ref[:, sl], qg_ref[...], cosn_ref[...],
                                                   sinn_ref[...], head_dim)
        m = jnp.max(s, axis=-1, keepdims=True)
        p = jnp.exp2(s - m).astype(BF16)
        oe = jnp.dot(p, vext, preferred_element_type=F32)
        o_ref[:, sl] = (oe[:, :head_dim] / oe[:, head_dim:]).astype(o_ref.dtype)


def _attention(q, cos, sin, qg, kc, k, vc, v, B, head_dim):
    N, W = q.shape
    L, Lc = N // B, kc.shape[0] // B
    kvh = k.shape[1] // head_dim
    gw = W // kvh
    tq = _tile(L, Q_TILE)
    tpb = L // tq
    kv_lat = pl.BlockSpec((L, head_dim), lambda b, h, i: (b, h))
    kv_ctx = pl.BlockSpec((Lc, head_dim), lambda b, h, i: (b, h))

    def nxt(b, h, i):
        wrap = i + 1 >= tpb
        g = jnp.minimum(b * kvh + h + 1, B * kvh - 1)
        return (jnp.where(wrap, g // kvh, b), jnp.where(wrap, g % kvh, h),
                jnp.where(wrap, 0, i + 1))

    def q_next(b, h, i):
        nb, nh, ni = nxt(b, h, i)
        return nb * tpb + ni, nh

    tab = pl.BlockSpec((tq, head_dim), lambda b, h, i: (i, 0))
    tab_next = pl.BlockSpec((tq, head_dim), lambda b, h, i: (nxt(b, h, i)[2], 0))
    return pl.pallas_call(
        functools.partial(_attn_kernel, head_dim=head_dim),
        grid=(B, kvh, tpb),
        in_specs=[pl.BlockSpec((tq, gw), lambda b, h, i: (b * tpb + i, h)), tab, tab,
                  pl.BlockSpec((tq, gw), q_next), tab_next, tab_next,
                  pl.BlockSpec((1, head_dim), lambda b, h, i: (0, 0)),
                  kv_ctx, kv_lat, kv_ctx, kv_lat],
        out_specs=pl.BlockSpec((tq, gw), lambda b, h, i: (b * tpb + i, h)),
        out_shape=jax.ShapeDtypeStruct((N, W), BF16),
        scratch_shapes=[pltpu.VMEM((L + Lc, head_dim), BF16),
                        pltpu.VMEM((L + Lc, 2 * head_dim), BF16),
                        pltpu.VMEM((2, tq, gw), BF16)],
        compiler_params=_params("arbitrary", "arbitrary", "arbitrary"),
        name="attention",
    )(q, cos, sin, q, cos, sin, qg, kc, k, vc, v)


def _ssm_operators(a_re, a_im, log_dt, b_re, b_im, c_re, c_im):
    T = SSM_CHUNK
    a_re, a_im = a_re.astype(F32), a_im.astype(F32)
    dt = jnp.exp(log_dt.astype(F32))[..., None]
    mag = jnp.exp(a_re * dt)
    lr = mag * jnp.cos(a_im * dt)
    li = mag * jnp.sin(a_im * dt)
    den = a_re * a_re + a_im * a_im
    cr = ((lr - 1.0) * a_re + li * a_im) / den
    ci = (li * a_re - (lr - 1.0) * a_im) / den
    pr, pi = [jnp.ones_like(lr)], [jnp.zeros_like(lr)]
    for _ in range(T):
        pr.append(pr[-1] * lr - pi[-1] * li)
        pi.append(pr[-2] * li + pi[-1] * lr)
    pw_r, pw_i = jnp.stack(pr), jnp.stack(pi)
    _, G, P, E = b_re.shape
    bt_r = jnp.swapaxes(b_re.astype(F32), 2, 3)
    bt_i = jnp.swapaxes(b_im.astype(F32), 2, 3)
    bb_r = cr[:, :, None, :] * bt_r - ci[:, :, None, :] * bt_i
    bb_i = cr[:, :, None, :] * bt_i + ci[:, :, None, :] * bt_r
    c_r, c_i = c_re.astype(F32), c_im.astype(F32)

    def lag_table(pw, fwd_lags, bwd_lags, reps):
        tab = jnp.stack([jnp.moveaxis(pw[fwd_lags, 0], 0, 1), jnp.moveaxis(pw[bwd_lags, 1], 0, 1)])
        return jnp.concatenate([tab] * reps, axis=-1)

    steps = np.arange(T)
    lags = np.arange(T + 1)
    p1 = lag_table(pw_r, T - 1 - steps, steps, 4)
    p2 = lag_table(pw_i, T - 1 - steps, steps, 4)
    q1 = lag_table(pw_r, lags, T - lags, 2)
    q2 = lag_table(pw_i, lags, T - lags, 2)
    bb1 = jnp.concatenate([bb_r, bb_i, bb_i, bb_r], axis=-1)
    bb2 = jnp.concatenate([-bb_i, bb_r, bb_r, -bb_i], axis=-1)
    cc1 = jnp.concatenate([c_r, -c_i], axis=-1)
    cc2 = jnp.concatenate([-c_i, -c_r], axis=-1)
    ar, ai = pw_r[T], pw_i[T]
    dec = jnp.stack([jnp.concatenate([ar, ar], -1), jnp.concatenate([-ai, ai], -1)], axis=2)
    return p1, p2, bb1, bb2, q1, q2, cc1, cc2, dec


def _state_in_operator(p1_ref, p2_ref, bb1_ref, bb2_ref, d, g):
    bb1, bb2 = bb1_ref[d, g], bb2_ref[d, g]
    rows = [p1_ref[d, g, s:s + 1, :] * bb1 + p2_ref[d, g, s:s + 1, :] * bb2
            for s in range(SSM_CHUNK)]
    return jnp.concatenate(rows, axis=0).astype(BF16)


def _readout_table(q1_ref, q2_ref, cc1_ref, cc2_ref, d, g):
    cc1, cc2 = cc1_ref[d, g], cc2_ref[d, g]
    rows = [q1_ref[d, g, j:j + 1, :] * cc1 + q2_ref[d, g, j:j + 1, :] * cc2
            for j in range(SSM_CHUNK + 1)]
    return jnp.concatenate(rows, axis=0)


def _intra_operator(wf, wb, e):
    T = SSM_CHUNK
    t_of_lane = lax.broadcasted_iota(jnp.int32, wf.shape, 1) // e
    rows = []
    for s in range(T):
        f = wf if s == 0 else jnp.where(t_of_lane >= s, pltpu.roll(wf, s * e, 1), 0.0)
        b = wb if s == T - 1 else jnp.where(t_of_lane <= s, pltpu.roll(wb, (s + 1) * e, 1), 0.0)
        rows.append(f + b)
    return jnp.concatenate(rows, axis=0).astype(BF16)


def _row_pitch(n):
    p = -(-n // SUBLANES)
    return SUBLANES * (p if p % 2 else p + 1)


def _ssm_kernel(u_ref, uc_ref, d_ref, p1_ref, p2_ref, bb1_ref, bb2_ref, q1_ref, q2_ref,
                cc1_ref, cc2_ref, dec_ref, y_ref, z_ref, s1_ref, s2_ref, hp_ref, *, batch, e):
    T = SSM_CHUNK
    gpb = dec_ref.shape[1]
    rp = z_ref.shape[1]
    n_lat, n_ctx = u_ref.shape[0] // T, uc_ref.shape[0] // T
    ncl, ncc = n_lat // batch, n_ctx // batch
    pl_, pc_ = _row_pitch(ncl), _row_pitch(ncc)
    lat0 = batch * pc_
    r = lat0 + batch * pl_
    lanes = u_ref.shape[1]
    pw = hp_ref.shape[3]

    def padded(rows, n, pitch):
        if pitch == n:
            return [rows]
        out = []
        for b in range(batch):
            out += [rows[b * n:(b + 1) * n, :], jnp.zeros((pitch - n, lanes), F32)]
        return out

    xt, u_lat = [], []
    for s in range(T):
        u_lat.append(u_ref[pl.ds(s, n_lat, stride=T), :])
        parts = (padded(uc_ref[pl.ds(s, n_ctx, stride=T), :], ncc, pc_)
                 + padded(u_lat[s], ncl, pl_))
        if rp > r:
            parts.append(jnp.zeros((rp - r, lanes), F32))
        xt.append(jnp.concatenate(parts, axis=0).astype(BF16).T)
    for g in range(gpb):
        zt = jnp.concatenate([xt[s][g * e:(g + 1) * e, :] for s in range(T)], axis=0)
        z_ref[g] = zt.T
    for d in range(2):
        for g in range(gpb):
            st = jnp.dot(z_ref[g], _state_in_operator(p1_ref, p2_ref, bb1_ref, bb2_ref, d, g),
                         preferred_element_type=F32)
            s1_ref[d, g] = st[:, :pw]
            s2_ref[d, g] = st[:, pw:]

    def sweep(base, count, stride, carry):
        def body(it, carry):
            out = []
            for d in range(2):
                n = it if d == 0 else count - 1 - it
                rows = pl.ds(base + n, batch, stride=stride)
                for g in range(gpb):
                    h1, h2 = carry[2 * (d * gpb + g)], carry[2 * (d * gpb + g) + 1]
                    hp_ref[d, g, rows, :] = h1
                    a1 = dec_ref[d, g, 0:1, :]
                    a2 = dec_ref[d, g, 1:2, :]
                    out.append(a1 * h1 + a2 * h2 + s1_ref[d, g, rows, :])
                    out.append(a1 * h2 - a2 * h1 + s2_ref[d, g, rows, :])
            return tuple(out)
        return lax.fori_loop(0, count, body, carry)

    for d in range(2):
        for g in range(gpb):
            for b in range(batch):
                if pc_ > ncc:
                    hp_ref[d, g, b * pc_ + ncc:(b + 1) * pc_, :] = jnp.zeros((pc_ - ncc, pw), F32)
                if pl_ > ncl:
                    hp_ref[d, g, lat0 + b * pl_ + ncl:lat0 + (b + 1) * pl_, :] = jnp.zeros((pl_ - ncl, pw), F32)
            if rp > r:
                hp_ref[d, g, r:rp, :] = jnp.zeros((rp - r, pw), F32)
    zero = jnp.zeros((batch, pw), F32)
    carry = sweep(0, ncc, pc_, tuple([zero] * (4 * gpb)))
    sweep(lat0, ncl, pl_, carry)
    nt = (((1,), (1,)), ((), ()))
    te = T * e
    yt = []
    for g in range(gpb):
        rf = _readout_table(q1_ref, q2_ref, cc1_ref, cc2_ref, 0, g)
        rb = _readout_table(q1_ref, q2_ref, cc1_ref, cc2_ref, 1, g)
        wf = lax.dot_general(bb1_ref[0, g][:, :pw], rf[:te], nt, precision=lax.Precision.HIGHEST,
                             preferred_element_type=F32)
        wb = lax.dot_general(bb1_ref[1, g][:, :pw], rb[e:], nt, precision=lax.Precision.HIGHEST,
                             preferred_element_type=F32)
        y = jnp.dot(z_ref[g], _intra_operator(wf, wb, e), preferred_element_type=F32)
        y += lax.dot_general(hp_ref[0, g].astype(BF16), rf[e:].astype(BF16), nt,
                             preferred_element_type=F32)
        y += lax.dot_general(hp_ref[1, g].astype(BF16), rb[:te].astype(BF16), nt,
                             preferred_element_type=F32)
        yt.append(y.T)
    dvec = d_ref[...]
    for t in range(T):
        blk = jnp.concatenate([yt[g][t * e:(t + 1) * e, :] for g in range(gpb)], axis=0).T
        for b in range(batch):
            rows = pl.ds(b * ncl * T + t, ncl, stride=T)
            y_ref[rows, :] = (blk[lat0 + b * pl_:lat0 + b * pl_ + ncl, :]
                              + dvec * u_lat[t][b * ncl:(b + 1) * ncl, :])


def _s5(u, uc, d, ops, batch):
    dec = ops[-1]
    N, W = u.shape
    Nc = uc.shape[0]
    G = dec.shape[1]
    E = W // G
    gpb = LANES // E
    TE = SSM_CHUNK * E
    P2 = dec.shape[-1]
    assert P2 == LANES and W % LANES == 0

    def table(a):
        return pl.BlockSpec((2, gpb) + a.shape[2:], lambda j: (0, j, 0, 0))
    r = batch * (_row_pitch(N // SSM_CHUNK // batch) + _row_pitch(Nc // SSM_CHUNK // batch))
    rp = -(-r // LANES) * LANES
    return pl.pallas_call(
        functools.partial(_ssm_kernel, batch=batch, e=E),
        grid=(W // LANES,),
        in_specs=[pl.BlockSpec((N, LANES), lambda j: (0, j)),
                  pl.BlockSpec((Nc, LANES), lambda j: (0, j)),
                  pl.BlockSpec((1, LANES), lambda j: (0, j))] + [table(a) for a in ops],
        out_specs=pl.BlockSpec((N, LANES), lambda j: (0, j)),
        out_shape=jax.ShapeDtypeStruct((N, W), F32),
        scratch_shapes=[pltpu.VMEM((gpb, rp, TE), BF16),
                        pltpu.VMEM((2, gpb, rp, P2), F32),
                        pltpu.VMEM((2, gpb, rp, P2), F32),
                        pltpu.VMEM((2, gpb, rp, P2), F32)],
        compiler_params=_params("arbitrary"),
        name="s5",
    )(u, uc, d, *ops)


def _mixer_out_kernel(attn_ref, y_ref, ga_ref, gs_ref, x_ref, g2_ref, wglu_ref, bglu_ref,
                      wa_ref, ws_ref, wo_ref, o_ref, y2_ref, m_ref, *, sub, tpb):
    j = pl.program_id(1)
    tm, tn = o_ref.shape
    nd = m_ref.shape[1] // tn
    half = tn // 2

    def merge(rows, attn, y2):
        for c in (0, half):
            pa = jnp.dot(attn, wa_ref[:, c:c + half], preferred_element_type=F32)
            ps = jnp.dot(y2, ws_ref[:, c:c + half], preferred_element_type=F32)
            merged = (jax.nn.sigmoid(ga_ref[rows, c:c + half].astype(F32)) * pa
                      + jax.nn.sigmoid(gs_ref[rows, c:c + half].astype(F32)) * ps)
            m_ref[rows, pl.ds(pl.multiple_of(j * tn + c, half), half)] = merged.astype(BF16)

    @pl.when(j == 0)
    def _():
        for r in range(0, tm, sub):
            rows = slice(r, r + sub)
            y = jax.nn.gelu(y_ref[rows, :])
            z = jnp.dot(y.astype(BF16), wglu_ref[...], preferred_element_type=F32) + bglu_ref[...]
            y2 = (y * jax.nn.sigmoid(z)).astype(BF16)
            y2_ref[rows, :] = y2
            merge(rows, attn_ref[rows, :], y2)

    @pl.when((j > 0) & (j < nd))
    def _():
        merge(slice(None), attn_ref[...], y2_ref[...])

    @pl.when(j >= nd)
    def _():
        for c in (0, half):
            col = pl.ds(pl.multiple_of((j - nd) * tn + c, half), half)
            acc = jnp.dot(m_ref[...], wo_ref[:, col], preferred_element_type=F32)
            g2 = _mod_row(g2_ref, 0, tpb)
            o_ref[:, c:c + half] = x_ref[:, c:c + half] + g2[:, c:c + half] * acc


def _mixer_out(attn, y_ssm, gate, x, mod, k_gate, w_glu, b_glu, wa, ws, wo, rows_per_mod):
    N, D = x.shape
    WA, WS = attn.shape[1], y_ssm.shape[1]
    tm = _tile(rows_per_mod, TOKEN_TILE)
    tn = _tile(D, COL_TILE, MXU_WIDTH)
    nd = D // tn
    tpb = rows_per_mod // tm
    lo = lambda j: jnp.minimum(j, nd - 1)
    hi = lambda j: jnp.maximum(j - nd, 0)
    return pl.pallas_call(
        functools.partial(_mixer_out_kernel, sub=_tile(tm, ROW_BLOCK), tpb=tpb),
        grid=(N // tm, 2 * nd),
        in_specs=[pl.BlockSpec((tm, WA), lambda i, j: (i, 0)),
                  pl.BlockSpec((tm, WS), lambda i, j: (i, 0)),
                  pl.BlockSpec((tm, tn), lambda i, j: (i, lo(j))),
                  pl.BlockSpec((tm, tn), lambda i, j: (i, nd + lo(j))),
                  pl.BlockSpec((tm, tn), lambda i, j: (i, hi(j))),
                  pl.BlockSpec((mod.shape[0], tn), lambda i, j: (0, k_gate * nd + hi(j))),
                  pl.BlockSpec((WS, WS), lambda i, j: (0, 0), pipeline_mode=pl.Buffered(1)),
                  pl.BlockSpec((1, WS), lambda i, j: (0, 0)),
                  pl.BlockSpec((WA, tn), lambda i, j: (0, lo(j))),
                  pl.BlockSpec((WS, tn), lambda i, j: (0, lo(j))),
                  pl.BlockSpec((D, D), lambda i, j: (0, 0), pipeline_mode=pl.Buffered(1))],
        out_specs=pl.BlockSpec((tm, tn), lambda i, j: (i, hi(j))),
        out_shape=jax.ShapeDtypeStruct((N, D), F32),
        scratch_shapes=[pltpu.VMEM((tm, WS), BF16), pltpu.VMEM((tm, D), BF16)],
        compiler_params=_params("arbitrary", "arbitrary"),
        name="mixer_out",
    )(attn, y_ssm, gate, gate, x, mod, w_glu, b_glu, wa, ws, wo)


def kernel(x, c, ctx, c_ctx, w_mod, b_mod, norm_g, w_ffn1_gate, w_ffn1_up, w_ffn1_down, w_in, q_norm_g, k_norm_g, ssm_a_re, ssm_a_im, ssm_log_dt, ssm_b_re, ssm_b_im, ssm_c_re, ssm_c_im, ssm_d, w_glu, b_glu, w_br_attn, w_br_ssm, w_out, w_ffn2_gate, w_ffn2_up, w_ffn2_down):
    B, L, D = x.shape
    Lc = ctx.shape[1]
    assert w_mod.shape[0] == 1, "only the single (last) layer configuration is implemented"
    assert w_mod.shape[2] == N_MOD * D
    hd = q_norm_g.shape[1]
    ssm_w = w_glu.shape[1]
    attn_w = w_br_attn.shape[1]
    kv_w = attn_w // Q_PER_KV
    widths = (kv_w, ssm_w, attn_w, 2 * D)
    assert L % GRID_W == 0 and L % SSM_CHUNK == 0 and Lc % SSM_CHUNK == 0 and B < SUBLANES
    N, Nc = B * L, B * Lc
    l = 0

    cc = jnp.zeros((SUBLANES, D), F32).at[:B].set(c).at[B].set(c_ctx)
    mod = _modulation(cc, w_mod[l], b_mod[l][None, :])
    ng = norm_g[l][:, None, :]

    x2 = x.reshape(N, D)
    c2 = ctx.reshape(Nc, D)

    x2, wd1, (w_in_b,) = _ffn(x2, ng[0], mod, 0, 0, w_ffn1_gate[l], w_ffn1_up[l],
                              w_ffn1_down[l], L, side_cast=(w_in[l],))
    c2, _, _ = _ffn(c2, ng[0], mod, 0, B, w_ffn1_gate[l], w_ffn1_up[l], wd1, Nc)

    cos, sin = _rope_tables(L, hd)
    kg = k_norm_g[l][None, :]
    qg = q_norm_g[l][None, :] * (hd ** -0.5 * math.log2(math.e))
    k, v, u, q, gate = _proj(x2, ng[1], mod, 3, 0, w_in_b, cos, sin, kg, L, widths, True)
    kc, vc, uc = _proj(c2, ng[1], mod, 3, B, w_in_b, cos, sin, kg, Nc, widths, False)

    attn = _attention(q, cos, sin, qg, kc, k, vc, v, B, hd)
    ops = _ssm_operators(ssm_a_re[l], ssm_a_im[l], ssm_log_dt[l], ssm_b_re[l], ssm_b_im[l],
                         ssm_c_re[l], ssm_c_im[l])
    y_ssm = _s5(u, uc, ssm_d[l][None, :], ops, B)

    x2 = _mixer_out(attn, y_ssm, gate, x2, mod, 5, w_glu[l].astype(BF16), b_glu[l][None, :],
                    w_br_attn[l].astype(BF16), w_br_ssm[l].astype(BF16), w_out[l].astype(BF16), L)

    x2, _, _ = _ffn(x2, ng[2], mod, 6, 0, w_ffn2_gate[l], w_ffn2_up[l], w_ffn2_down[l], L)
    return x2.reshape(B, L, D)
```
